```python
import jax, jax.numpy as jnp
from jax import lax
import numpy as np

D_MODEL = 4096
BATCH = 1
SEQ = 8192
DEPTH = 1

D_MIX = D_MODEL
A_WIDTH = D_MIX // 2
A_CHUNK = 128
A_GROUP_DIM = 128
A_GROUPS = A_WIDTH // A_GROUP_DIM
B_WIDTH = D_MIX - A_WIDTH
B_HEAD_DIM = 64
B_HEADS = B_WIDTH // B_HEAD_DIM
DECAY_LORA = 96
ICLR_LORA = 96
GATE_LORA = 256
B_COLS = 3 * B_WIDTH + DECAY_LORA + ICLR_LORA + GATE_LORA
IN_COLS = 2 * A_WIDTH + B_COLS
N_EXPERTS = 128
TOP_K = 8
N_GROUPS = 8
TOPK_GROUPS = 4
D_EXPERT = 384
D_SHARED = 384
ROUTED_SCALE = 2.5
MOE_BLOCK = 128
NORM_EPS = 1e-6
LN_EPS = 1e-5
GN_EPS = 64e-5
N_MOD = 6

kernel_name = "hymba_gmlp_rwkv7_moe_adaln"


def rmsnorm(x, g):
    xf = x.astype(jnp.float32)
    y = xf * lax.rsqrt(jnp.mean(xf * xf, axis=-1, keepdims=True) + NORM_EPS)
    return (y * g.astype(jnp.float32)).astype(x.dtype)


def modulate(h, shift, scale):
    return h * (1 + scale[:, None, :]) + shift[:, None, :]


def spatial_gating(p, ln_g, ln_b, spatial_w, spatial_b):
    B, T, _ = p.shape
    u, v = jnp.split(jax.nn.gelu(p, approximate=False), 2, axis=-1)
    vf = v.reshape(B, T, A_GROUPS, A_GROUP_DIM).astype(jnp.float32)
    mu = jnp.mean(vf, axis=-1, keepdims=True)
    var = jnp.mean(jnp.square(vf - mu), axis=-1, keepdims=True)
    vf = (vf - mu) * lax.rsqrt(var + LN_EPS)
    vn = (vf * ln_g.reshape(A_GROUPS, A_GROUP_DIM).astype(jnp.float32)
          + ln_b.reshape(A_GROUPS, A_GROUP_DIM).astype(jnp.float32)).astype(p.dtype)
    n_chunks = T // A_CHUNK
    vn = vn.reshape(B, n_chunks, A_CHUNK, A_GROUPS, A_GROUP_DIM)
    ws = spatial_w * jnp.tril(jnp.ones((A_CHUNK, A_CHUNK), spatial_w.dtype))
    mixed = jnp.einsum('gij,bnjgc->bnigc', ws, vn) + spatial_b.T[None, None, :, :, None]
    return u * mixed.reshape(B, T, A_WIDTH)


def rwkv7_scan(r, w, k, v, a_vec, b_vec):
    B, T, H, N = r.shape

    def step(S, inp):
        r_t, w_t, k_t, v_t, a_t, b_t = inp
        sa = jnp.einsum('bhvk,bhk->bhv', S, a_t)
        S = S * w_t[:, :, None, :] + sa[..., None] * b_t[:, :, None, :] + v_t[..., None] * k_t[:, :, None, :]
        return S, jnp.einsum('bhvk,bhk->bhv', S, r_t)

    xs = tuple(jnp.moveaxis(t, 1, 0) for t in (r, w, k, v, a_vec, b_vec))
    S0 = jnp.zeros((B, H, N, N), jnp.float32)
    _, ys = lax.scan(step, S0, xs)
    return jnp.moveaxis(ys, 0, 1)


def rwkv7_mix(p, shift_mu, decay_up, decay_base, iclr_up, iclr_base, gate_up,
              kk_scale, ka_scale, bonus, gn_g, gn_b):
    B, T, _ = p.shape
    dt = p.dtype
    prev = jnp.pad(p, ((0, 0), (1, 0), (0, 0)))[:, :-1]
    p = p + shift_mu * (prev - p)
    cuts = [B_WIDTH, 2 * B_WIDTH, 3 * B_WIDTH, 3 * B_WIDTH + DECAY_LORA, 3 * B_WIDTH + DECAY_LORA + ICLR_LORA]
    r, k, v, wd, ad, gd = jnp.split(p, cuts, axis=-1)
    w_log = -jax.nn.softplus(-(decay_base + jnp.tanh(wd) @ decay_up).astype(jnp.float32)) - 0.5
    decay = jnp.exp(-jnp.exp(w_log))
    a = jax.nn.sigmoid(iclr_base + ad @ iclr_up)
    g = jax.nn.sigmoid(gd) @ gate_up
    heads = lambda t: t.reshape(B, T, B_HEADS, B_HEAD_DIM)
    kk = heads(k * kk_scale).astype(jnp.float32)
    kk = kk / jnp.maximum(jnp.sqrt(jnp.sum(kk * kk, axis=-1, keepdims=True)), 1e-12)
    k = k * (1 + (a - 1) * ka_scale)
    rh, kh, vh = heads(r), heads(k), heads(v)
    ah = heads(a).astype(jnp.float32)
    y = rwkv7_scan(rh.astype(jnp.float32), heads(decay), kh.astype(jnp.float32),
                   vh.astype(jnp.float32), -kk, kk * ah)
    mu = jnp.mean(y, axis=-1, keepdims=True)
    var = jnp.mean(jnp.square(y - mu), axis=-1, keepdims=True)
    y = (y - mu) * lax.rsqrt(var + GN_EPS)
    y = (y * gn_g.reshape(B_HEADS, B_HEAD_DIM).astype(jnp.float32)
         + gn_b.reshape(B_HEADS, B_HEAD_DIM).astype(jnp.float32))
    y = y.astype(dt)
    y = y + jnp.sum(rh * kh * bonus, axis=-1, keepdims=True) * vh
    return y.reshape(B, T, B_WIDTH) * g


def moe_ffn(h, router_w, router_bias, exp_w_gate, exp_w_up, exp_w_down,
            sh_w_gate, sh_w_up, sh_w_down):
    B, T, D = h.shape
    n_tok = B * T
    xt = h.reshape(n_tok, D)
    scores = jax.nn.sigmoid(jnp.dot(xt, router_w).astype(jnp.float32))
    sel = scores + router_bias.astype(jnp.float32)
    grp = sel.reshape(n_tok, N_GROUPS, N_EXPERTS // N_GROUPS)
    grp_score = jnp.sum(lax.top_k(grp, 2)[0], axis=-1)
    _, top_grp = lax.top_k(grp_score, TOPK_GROUPS)
    grp_mask = jnp.any(top_grp[..., None] == jnp.arange(N_GROUPS), axis=-2)
    exp_mask = jnp.repeat(grp_mask, N_EXPERTS // N_GROUPS, axis=-1)
    _, top_e = lax.top_k(jnp.where(exp_mask, sel, -jnp.inf), TOP_K)
    gate = jnp.take_along_axis(scores, top_e, axis=-1)
    gate = gate / jnp.sum(gate, axis=-1, keepdims=True) * ROUTED_SCALE
    n_assign = n_tok * TOP_K
    n_rows = n_assign + N_EXPERTS * MOE_BLOCK
    n_blocks = n_rows // MOE_BLOCK
    e_flat = top_e.reshape(-1).astype(jnp.int32)
    tok_flat = jnp.arange(n_assign, dtype=jnp.int32) // TOP_K
    w_flat = gate.reshape(-1).astype(h.dtype)
    order = jnp.argsort(e_flat)
    e_sorted = e_flat[order]
    counts = jnp.bincount(e_flat, length=N_EXPERTS).astype(jnp.int32)
    padded = (counts + MOE_BLOCK - 1) // MOE_BLOCK * MOE_BLOCK
    start = jnp.cumsum(counts) - counts
    pstart = jnp.cumsum(padded) - padded
    dest = pstart[e_sorted] + jnp.arange(n_assign, dtype=jnp.int32) - start[e_sorted]
    row_tok = jnp.full((n_rows,), n_tok, jnp.int32).at[dest].set(tok_flat[order])
    row_w = jnp.zeros((n_rows,), h.dtype).at[dest].set(w_flat[order])
    block_e = jnp.searchsorted(jnp.cumsum(padded), jnp.arange(n_blocks, dtype=jnp.int32) * MOE_BLOCK, side='right')
    block_e = jnp.minimum(block_e, N_EXPERTS - 1).astype(jnp.int32)
    x_pad = jnp.concatenate([xt, jnp.zeros((1, D), xt.dtype)], axis=0)

    def block_body(acc, blk):
        idx, wts, e = blk
        xb = x_pad[idx]
        hb = jax.nn.silu(xb @ exp_w_gate[e]) * (xb @ exp_w_up[e])
        upd = ((hb @ exp_w_down[e]) * wts[:, None]).astype(acc.dtype)
        return acc.at[idx].add(upd), None

    acc, _ = lax.scan(block_body, jnp.zeros_like(x_pad),
                      (row_tok.reshape(n_blocks, MOE_BLOCK), row_w.reshape(n_blocks, MOE_BLOCK), block_e))
    routed = acc[:n_tok]
    shared = (jax.nn.silu(xt @ sh_w_gate) * (xt @ sh_w_up)) @ sh_w_down
    return (shared + routed).reshape(B, T, D)


def setup_inputs(seed: int = 0) -> dict:
    key = jax.random.key(seed)
    ks = jax.random.split(key, 32)
    f32 = jnp.float32
    nrm = lambda k, shape, s: jax.random.normal(k, shape, f32) * s
    D, L = D_MODEL, DEPTH
    return {
        "x": nrm(ks[0], (BATCH, SEQ, D), 1.0),
        "c": nrm(ks[1], (BATCH, D), 1.0),
        "mod_w": nrm(ks[2], (L, D, N_MOD * D), 0.5 * D ** -0.5),
        "mod_b": nrm(ks[3], (L, N_MOD * D), 0.02),
        "norm1_g": 1.0 + nrm(ks[4], (L, D), 0.02),
        "norm2_g": 1.0 + nrm(ks[5], (L, D), 0.02),
        "w_in": nrm(ks[6], (L, D, IN_COLS), D ** -0.5),
        "w_out": nrm(ks[7], (L, D_MIX, D), D_MIX ** -0.5),
        "a_ln_g": 1.0 + nrm(ks[8], (L, A_WIDTH), 0.02),
        "a_ln_b": nrm(ks[9], (L, A_WIDTH), 0.02),
        "a_spatial_w": nrm(ks[10], (L, A_GROUPS, A_CHUNK, A_CHUNK), A_CHUNK ** -0.5),
        "a_spatial_b": 1.0 + nrm(ks[11], (L, A_GROUPS, A_CHUNK), 0.02),
        "b_shift_mu": jax.random.uniform(ks[12], (L, B_COLS), f32),
        "b_decay_up": nrm(ks[13], (L, DECAY_LORA, B_WIDTH), DECAY_LORA ** -0.5),
        "b_decay_base": nrm(ks[14], (L, B_WIDTH), 1.0),
        "b_iclr_up": nrm(ks[15], (L, ICLR_LORA, B_WIDTH), ICLR_LORA ** -0.5),
        "b_iclr_base": nrm(ks[16], (L, B_WIDTH), 0.1),
        "b_gate_up": nrm(ks[17], (L, GATE_LORA, B_WIDTH), GATE_LORA ** -0.5),
        "b_kk_scale": 0.85 + nrm(ks[18], (L, B_WIDTH), 0.02),
        "b_ka_scale": 1.0 + nrm(ks[19], (L, B_WIDTH), 0.02),
        "b_bonus": nrm(ks[20], (L, B_HEADS, B_HEAD_DIM), 0.1),
        "b_gn_g": 1.0 + nrm(ks[21], (L, B_WIDTH), 0.02),
        "b_gn_b": nrm(ks[22], (L, B_WIDTH), 0.02),
        "router_w": nrm(ks[23], (L, D, N_EXPERTS), D ** -0.5),
        "router_bias": nrm(ks[24], (L, N_EXPERTS), 0.01),
        "exp_w_gate": nrm(ks[25], (L, N_EXPERTS, D, D_EXPERT), D ** -0.5),
        "exp_w_up": nrm(ks[26], (L, N_EXPERTS, D, D_EXPERT), D ** -0.5),
        "exp_w_down": nrm(ks[27], (L, N_EXPERTS, D_EXPERT, D), D_EXPERT ** -0.5),
        "sh_w_gate": nrm(ks[28], (L, D, D_SHARED), D ** -0.5),
        "sh_w_up": nrm(ks[29], (L, D, D_SHARED), D ** -0.5),
        "sh_w_down": nrm(ks[30], (L, D_SHARED, D), D_SHARED ** -0.5),
        "final_g": 1.0 + nrm(ks[31], (D,), 0.02),
    }


def reference(x, c, mod_w, mod_b, norm1_g, norm2_g, w_in, w_out,
              a_ln_g, a_ln_b, a_spatial_w, a_spatial_b,
              b_shift_mu, b_decay_up, b_decay_base, b_iclr_up, b_iclr_base, b_gate_up,
              b_kk_scale, b_ka_scale, b_bonus, b_gn_g, b_gn_b,
              router_w, router_bias, exp_w_gate, exp_w_up, exp_w_down,
              sh_w_gate, sh_w_up, sh_w_down, final_g):
    for l in range(DEPTH):
        mod = jnp.dot(jax.nn.silu(c), mod_w[l]) + mod_b[l]
        sh1, sc1, g1, sh2, sc2, g2 = jnp.split(mod, N_MOD, axis=-1)
        h = modulate(rmsnorm(x, norm1_g[l]), sh1, sc1)
        proj = jnp.einsum('btd,dc->btc', h, w_in[l])
        ya = spatial_gating(proj[..., :2 * A_WIDTH], a_ln_g[l], a_ln_b[l], a_spatial_w[l], a_spatial_b[l])
        yb = rwkv7_mix(proj[..., 2 * A_WIDTH:], b_shift_mu[l], b_decay_up[l], b_decay_base[l],
                       b_iclr_up[l], b_iclr_base[l], b_gate_up[l], b_kk_scale[l], b_ka_scale[l],
                       b_bonus[l], b_gn_g[l], b_gn_b[l])
        mix = jnp.einsum('btm,md->btd', jnp.concatenate([ya, yb], axis=-1), w_out[l])
        x = x + g1[:, None, :] * mix
        h = modulate(rmsnorm(x, norm2_g[l]), sh2, sc2)
        ffn = moe_ffn(h, router_w[l], router_bias[l], exp_w_gate[l], exp_w_up[l], exp_w_down[l],
                      sh_w_gate[l], sh_w_up[l], sh_w_down[l])
        x = x + g2[:, None, :] * ffn
    return rmsnorm(x, final_g)
```

```python
import functools

import jax
import jax.numpy as jnp
from jax import lax
from jax.experimental import pallas as pl
from jax.experimental.pallas import tpu as pltpu

A_GROUP_DIM = 128
B_HEAD_DIM = 64
DECAY_LORA = 96
ICLR_LORA = 96
GATE_LORA = 256
TOP_K = 8
N_GROUPS = 8
TOPK_GROUPS = 4
ROUTED_SCALE = 2.5
NORM_EPS = 1e-6
LN_EPS = 1e-5
GN_EPS = 64e-5
KK_EPS = 1e-12

LANES = 128
VMEM_LIMIT_BYTES = 58 * 1024 * 1024
SCAN_CHUNK = 64
MOE_ROWS = 256
COMBINE_TOKENS = 64

F32 = jnp.float32
BF16 = jnp.bfloat16
HIGHEST = lax.Precision.HIGHEST


def _params(sem, vmem=VMEM_LIMIT_BYTES):
    return pltpu.CompilerParams(dimension_semantics=sem, vmem_limit_bytes=vmem)


def _dot(a, b, precision=None):
    return jnp.dot(a, b, preferred_element_type=F32, precision=precision)


def _mod_body(c_ref, w_ref, b_ref, o_ref, s_ref):
    @pl.when(pl.program_id(0) == 0)
    def _():
        c = c_ref[...]
        s_ref[...] = jnp.broadcast_to(c * jax.nn.sigmoid(c), s_ref.shape)

    s = s_ref[...]
    cols = [jnp.sum(w_ref[:, j * LANES:(j + 1) * LANES] * s, axis=0, keepdims=True)
            for j in range(w_ref.shape[1] // LANES)]
    o_ref[...] = jnp.concatenate(cols, axis=1) + b_ref[...]


def _mod_vector(c, mod_w, mod_b, tn=512):
    d, n = mod_w.shape
    return pl.pallas_call(
        _mod_body,
        out_shape=jax.ShapeDtypeStruct((1, n), F32),
        grid=(n // tn,),
        in_specs=[pl.BlockSpec((d, 1), lambda j: (0, 0)),
                  pl.BlockSpec((d, tn), lambda j: (0, j)),
                  pl.BlockSpec((1, tn), lambda j: (0, j))],
        out_specs=pl.BlockSpec((1, tn), lambda j: (0, j)),
        scratch_shapes=[pltpu.VMEM((d, LANES), F32)],
        compiler_params=_params(("arbitrary",)),
        name="mod_vector",
    )(c.reshape(d, 1), mod_w, mod_b.reshape(1, n))


def _norm_modulate(x, g, shift, scale):
    ms = jnp.mean(x * x, axis=-1, keepdims=True)
    y = x * lax.rsqrt(ms + NORM_EPS) * g
    return y * (1.0 + scale) + shift


def _normmod_body(x_ref, g_ref, sh_ref, sc_ref, o_ref):
    o_ref[...] = _norm_modulate(x_ref[...], g_ref[...], sh_ref[...], sc_ref[...]).astype(o_ref.dtype)


def _normmod(x, g, shift, scale, tm=256):
    t, d = x.shape
    tm = min(tm, t)
    row = pl.BlockSpec((tm, d), lambda i: (i, 0))
    vec = pl.BlockSpec((1, d), lambda i: (0, 0))
    return pl.pallas_call(
        _normmod_body,
        out_shape=jax.ShapeDtypeStruct((t, d), BF16),
        grid=(t // tm,),
        in_specs=[row, vec, vec, vec],
        out_specs=row,
        compiler_params=_params(("arbitrary",)),
        name="norm_modulate",
    )(x, g, shift, scale)


def _proj_body(h_ref, w_ref, o_ref, wb_ref, *, n_valid):
    @pl.when(pl.program_id(1) == 0)
    def _():
        w = w_ref[...]
        col = pl.program_id(0) * w.shape[1] + lax.broadcasted_iota(jnp.int32, w.shape, 1)
        wb_ref[...] = jnp.where(col < n_valid, w, 0.0).astype(wb_ref.dtype)

    o_ref[...] = _dot(h_ref[...], wb_ref[...])


def _in_projection(h, w_in, tm=1024, tn=512):
    t, d = h.shape
    n = w_in.shape[1]
    tm = min(tm, t)
    n_pad = pl.cdiv(n, tn) * tn
    return pl.pallas_call(
        functools.partial(_proj_body, n_valid=n),
        out_shape=jax.ShapeDtypeStruct((t, n_pad), F32),
        grid=(n_pad // tn, t // tm),
        in_specs=[pl.BlockSpec((tm, d), lambda j, i: (i, 0)),
                  pl.BlockSpec((d, tn), lambda j, i: (0, j))],
        out_specs=pl.BlockSpec((tm, tn), lambda j, i: (i, j)),
        scratch_shapes=[pltpu.VMEM((d, tn), BF16)],
        compiler_params=_params(("arbitrary", "arbitrary")),
        name="in_projection",
    )(h, w_in)


def _gelu(x):
    return 0.5 * x * (1.0 + lax.erf(x * (2.0 ** -0.5)))


def _mixer_a_body(u_ref, v_ref, lng_ref, lnb_ref, ws_ref, sb_ref, o_ref):
    tm = u_ref.shape[0]
    gd = A_GROUP_DIM
    row = lax.broadcasted_iota(jnp.int32, (gd, gd), 0)
    col = lax.broadcasted_iota(jnp.int32, (gd, gd), 1)
    causal = col <= row
    for g in range(u_ref.shape[1] // gd):
        sl = slice(g * gd, (g + 1) * gd)
        u = _gelu(u_ref[:, sl])
        v = _gelu(v_ref[:, sl])
        mu = jnp.mean(v, axis=-1, keepdims=True)
        dv = v - mu
        var = jnp.mean(dv * dv, axis=-1, keepdims=True)
        vn = ((dv * lax.rsqrt(var + LN_EPS)) * lng_ref[:, sl] + lnb_ref[:, sl]).astype(BF16)
        w = jnp.where(causal, ws_ref[g], 0.0).astype(BF16)
        for n in range(tm // gd):
            rs = slice(n * gd, (n + 1) * gd)
            mixed = _dot(w, vn[rs]) + sb_ref[g]
            o_ref[rs, sl] = (u[rs] * mixed).astype(o_ref.dtype)


def _mixer_a(proj, ln_g, ln_b, spatial_w, spatial_b, a_width, tm=256):
    t = proj.shape[0]
    tm = min(tm, t)
    groups = a_width // A_GROUP_DIM
    sb = jnp.broadcast_to(spatial_b[:, :, None], (groups, A_GROUP_DIM, A_GROUP_DIM))
    vec = pl.BlockSpec((1, a_width), lambda i: (0, 0))
    mat = pl.BlockSpec((groups, A_GROUP_DIM, A_GROUP_DIM), lambda i: (0, 0, 0))
    return pl.pallas_call(
        _mixer_a_body,
        out_shape=jax.ShapeDtypeStruct((t, a_width), BF16),
        grid=(t // tm,),
        in_specs=[pl.BlockSpec((tm, a_width), lambda i: (i, 0)),
                  pl.BlockSpec((tm, a_width), lambda i: (i, 1)),
                  vec, vec, mat, mat],
        out_specs=pl.BlockSpec((tm, a_width), lambda i: (i, 0)),
        compiler_params=_params(("arbitrary",)),
        name="mixer_a",
    )(proj, proj, ln_g.reshape(1, -1), ln_b.reshape(1, -1), spatial_w, sb)


def _head_sums(x, ones_bd):
    parts = [_dot(x[:, s * LANES:(s + 1) * LANES], ones_bd, precision=HIGHEST)
             for s in range(x.shape[1] // LANES)]
    return jnp.concatenate(parts, axis=1)


def _softplus(x):
    return jnp.maximum(x, 0.0) + jnp.log1p(jnp.exp(-jnp.abs(x)))


def _mixer_b_prep_body(r_ref, k_ref, v_ref, l_ref, pr_ref, pk_ref, pv_ref, pl_ref,
                       mur_ref, muk_ref, muv_ref, mul_ref, dup_ref, iup_ref, gup_ref,
                       dbase_ref, ibase_ref, kks_ref, kas_ref, bonus_ref, ones_ref,
                       ro_ref, wo_ref, ko_ref, vo_ref, ao_ref, bo_ref, go_ref, bvo_ref):
    first = pl.program_id(0) == 0

    def shifted(p_ref, prev_ref, mu_ref):
        p = p_ref[...]
        last = prev_ref.shape[0] - 1
        prev_row = jnp.where(first, 0.0, prev_ref[last:last + 1, :])
        rowid = lax.broadcasted_iota(jnp.int32, p.shape, 0)
        prev = jnp.where(rowid == 0, prev_row, pltpu.roll(p, 1, 0))
        return p + mu_ref[...] * (prev - p)

    r = shifted(r_ref, pr_ref, mur_ref)
    k = shifted(k_ref, pk_ref, muk_ref)
    v = shifted(v_ref, pv_ref, muv_ref)
    lora = shifted(l_ref, pl_ref, mul_ref)
    ones_bd = ones_ref[...]

    dec_in = dbase_ref[...] + _dot(jnp.tanh(lora).astype(BF16), dup_ref[...])
    w_log = -_softplus(-dec_in) - 0.5
    log_decay = -jnp.exp(w_log)
    a = jax.nn.sigmoid(ibase_ref[...] + _dot(lora.astype(BF16), iup_ref[...]))
    g = _dot(jax.nn.sigmoid(lora).astype(BF16), gup_ref[...])

    kk = k * kks_ref[...]
    kk = kk / jnp.maximum(jnp.sqrt(_head_sums(kk * kk, ones_bd)), KK_EPS)
    k = k * (1.0 + (a - 1.0) * kas_ref[...])

    ro_ref[...] = r
    wo_ref[...] = log_decay
    ko_ref[...] = k
    vo_ref[...] = v
    ao_ref[...] = -kk
    bo_ref[...] = kk * a
    go_ref[...] = g
    bvo_ref[...] = _head_sums(r * k * bonus_ref[...], ones_bd) * v


def _ones_block_diag():
    i = lax.broadcasted_iota(jnp.int32, (LANES, LANES), 0) // B_HEAD_DIM
    j = lax.broadcasted_iota(jnp.int32, (LANES, LANES), 1) // B_HEAD_DIM
    return (i == j).astype(F32)


def _mixer_b_prep(proj, a_cols, bw, lora_col, lora_w, shift_mu, decay_up, decay_base, iclr_up,
                  iclr_base, gate_up, kk_scale, ka_scale, bonus, tm=256):
    t = proj.shape[0]
    tm = min(tm, t)
    cb = a_cols // bw
    lb = lora_col // lora_w
    n_lora = DECAY_LORA + ICLR_LORA + GATE_LORA

    def pad_rows(w, start):
        return jnp.zeros((lora_w, bw), F32).at[start:start + w.shape[0]].set(w).astype(BF16)

    dup = pad_rows(decay_up, 0)
    iup = pad_rows(iclr_up, DECAY_LORA)
    gup = pad_rows(gate_up, DECAY_LORA + ICLR_LORA)
    mu = shift_mu.reshape(1, -1)
    mu_l = jnp.zeros((1, lora_w), F32).at[:, :n_lora].set(mu[:, 3 * bw:])

    def cur(width, blk):
        return pl.BlockSpec((tm, width), lambda i: (i, blk))

    def prev(width, blk):
        return pl.BlockSpec((8, width), lambda i: (jnp.maximum(i * (tm // 8) - 1, 0), blk))

    vec = pl.BlockSpec((1, bw), lambda i: (0, 0))
    vecl = pl.BlockSpec((1, lora_w), lambda i: (0, 0))
    up = pl.BlockSpec((lora_w, bw), lambda i: (0, 0))
    out = pl.BlockSpec((tm, bw), lambda i: (i, 0))
    row = lambda x: x.reshape(1, -1)
    return pl.pallas_call(
        _mixer_b_prep_body,
        out_shape=[jax.ShapeDtypeStruct((t, bw), F32)] * 8,
        grid=(t // tm,),
        in_specs=[cur(bw, cb), cur(bw, cb + 1), cur(bw, cb + 2), cur(lora_w, lb),
                  prev(bw, cb), prev(bw, cb + 1), prev(bw, cb + 2), prev(lora_w, lb),
                  vec, vec, vec, vecl, up, up, up, vec, vec, vec, vec, vec,
                  pl.BlockSpec((LANES, LANES), lambda i: (0, 0))],
        out_specs=[out] * 8,
        compiler_params=_params(("arbitrary",)),
        name="mixer_b_prep",
    )(proj, proj, proj, proj, proj, proj, proj, proj,
      mu[:, :bw], mu[:, bw:2 * bw], mu[:, 2 * bw:3 * bw], mu_l, dup, iup, gup,
      row(decay_base), row(iclr_base), row(kk_scale), row(ka_scale), row(bonus), _ones_block_diag())


def _scan_pair(r, w, k, v, a, b, s0, consts):
    c = SCAN_CHUNK
    tri_incl, head0, eye, strict_bd, incl_cat = consts

    cum = _dot(tri_incl, w, precision=HIGHEST)
    cum_last = cum[c - 1:c, :]
    e_pos = jnp.exp(cum)
    e_neg = jnp.exp(-cum)
    e_tail = jnp.exp(cum_last - cum)
    r_t = r * e_pos
    a_t = a * jnp.exp(cum - w)
    g_c = jnp.exp(cum_last)

    def stack(x):
        return jnp.concatenate([jnp.where(head0, x, 0.0), jnp.where(head0, 0.0, x)], axis=0)

    a_st = stack(a_t)
    v_bd = stack(v).astype(BF16)
    lhs = jnp.concatenate([a_st, r_t], axis=0).astype(BF16)
    rhs = jnp.concatenate([stack(b * e_neg), stack(k * e_neg)], axis=0).astype(BF16)
    big = lax.dot_general(lhs, rhs, (((1,), (1,)), ((), ())), preferred_element_type=F32)
    a_ab = jnp.where(strict_bd, big[:2 * c, :2 * c], 0.0)
    a_ak = jnp.where(strict_bd, big[:2 * c, 2 * c:], 0.0)
    a_rb = jnp.where(incl_cat, big[2 * c:, :2 * c], 0.0).astype(BF16)
    a_rk = jnp.where(incl_cat, big[2 * c:, 2 * c:], 0.0).astype(BF16)

    x = a_ab
    t_inv = eye + x
    p = 2
    while p < c:
        xb = x.astype(BF16)
        x = _dot(xb, xb)
        t_inv = t_inv + _dot(t_inv.astype(BF16), x.astype(BF16))
        p *= 2

    akv = _dot(a_ak.astype(BF16), v_bd)
    tu = _dot(t_inv.astype(BF16), jnp.concatenate([a_st, akv], axis=1).astype(BF16))
    tub = tu.astype(BF16)
    ry = _dot(a_rb, tub)
    r_hat = r_t + ry[:, :LANES]
    y0 = ry[:, LANES:] + _dot(a_rk, v_bd)
    bg = stack(b * e_tail).astype(BF16)
    kg = stack(k * e_tail).astype(BF16)
    contract0 = (((0,), (0,)), ((), ()))
    mn = lax.dot_general(bg, tub, contract0, preferred_element_type=F32)
    m = eye * g_c + mn[:, :LANES]
    n = mn[:, LANES:] + lax.dot_general(kg, v_bd, contract0, preferred_element_type=F32)

    y = _dot(r_hat, s0, precision=HIGHEST) + y0
    s1 = _dot(m, s0, precision=HIGHEST) + n
    return y, s1


def _scan_body(r_ref, w_ref, k_ref, v_ref, a_ref, b_ref, y_ref, s_ref):
    @pl.when(pl.program_id(1) == 0)
    def _():
        s_ref[...] = jnp.zeros(s_ref.shape, F32)

    c = SCAN_CHUNK
    ti = lax.broadcasted_iota(jnp.int32, (c, c), 0)
    tj = lax.broadcasted_iota(jnp.int32, (c, c), 1)
    tri_incl = (tj <= ti).astype(F32)
    head0 = lax.broadcasted_iota(jnp.int32, (c, LANES), 1) < B_HEAD_DIM
    i2 = lax.broadcasted_iota(jnp.int32, (2 * c, 2 * c), 0)
    j2 = lax.broadcasted_iota(jnp.int32, (2 * c, 2 * c), 1)
    eye = (i2 == j2).astype(F32)
    strict_bd = (i2 // c == j2 // c) & (j2 < i2)
    ic = lax.broadcasted_iota(jnp.int32, (c, 2 * c), 0)
    jc = lax.broadcasted_iota(jnp.int32, (c, 2 * c), 1)
    incl_cat = (jc % c) <= ic
    consts = (tri_incl, head0, eye, strict_bd, incl_cat)

    for p in range(r_ref.shape[1] // LANES):
        sl = slice(p * LANES, (p + 1) * LANES)
        y, s1 = _scan_pair(r_ref[:, sl], w_ref[:, sl], k_ref[:, sl], v_ref[:, sl],
                           a_ref[:, sl], b_ref[:, sl], s_ref[p], consts)
        y_ref[:, sl] = y
        s_ref[p] = s1


def _rwkv_scan(r, w, k, v, a, b, pairs_per_step=4):
    t, bw = r.shape
    pw = pairs_per_step * LANES
    blk = pl.BlockSpec((SCAN_CHUNK, pw), lambda g, c: (c, g))
    return pl.pallas_call(
        _scan_body,
        out_shape=jax.ShapeDtypeStruct((t, bw), F32),
        grid=(bw // pw, t // SCAN_CHUNK),
        in_specs=[blk] * 6,
        out_specs=blk,
        scratch_shapes=[pltpu.VMEM((pairs_per_step, LANES, LANES), F32)],
        compiler_params=_params(("arbitrary", "arbitrary")),
        name="rwkv_scan",
    )(r, w, k, v, a, b)


def _mixer_b_post_body(y_ref, bv_ref, g_ref, gng_ref, gnb_ref, ones_ref, o_ref):
    y = y_ref[...]
    ones_bd = ones_ref[...]
    inv_n = 1.0 / B_HEAD_DIM
    mu = _head_sums(y, ones_bd) * inv_n
    d = y - mu
    var = _head_sums(d * d, ones_bd) * inv_n
    yn = (d * lax.rsqrt(var + GN_EPS)) * gng_ref[...] + gnb_ref[...]
    o_ref[...] = ((yn + bv_ref[...]) * g_ref[...]).astype(o_ref.dtype)


def _mixer_b_post(y, bv, g, gn_g, gn_b, tm=256):
    t, bw = y.shape
    tm = min(tm, t)
    row = pl.BlockSpec((tm, bw), lambda i: (i, 0))
    vec = pl.BlockSpec((1, bw), lambda i: (0, 0))
    return pl.pallas_call(
        _mixer_b_post_body,
        out_shape=jax.ShapeDtypeStruct((t, bw), BF16),
        grid=(t // tm,),
        in_specs=[row, row, row, vec, vec, pl.BlockSpec((LANES, LANES), lambda i: (0, 0))],
        out_specs=row,
        compiler_params=_params(("arbitrary",)),
        name="mixer_b_post",
    )(y, bv, g, gn_g.reshape(1, -1), gn_b.reshape(1, -1), _ones_block_diag())


def _out_proj_body(ya_ref, yb_ref, w_ref, x_ref, g1_ref, o_ref, wb_ref):
    @pl.when(pl.program_id(1) == 0)
    def _():
        wb_ref[...] = w_ref[...].astype(wb_ref.dtype)

    half = ya_ref.shape[1]
    mix = _dot(ya_ref[...], wb_ref[:half, :]) + _dot(yb_ref[...], wb_ref[half:, :])
    o_ref[...] = x_ref[...] + g1_ref[...] * mix


def _out_projection(ya, yb, w_out, x, g1, tm=1024, tn=512):
    t, d = x.shape
    tm = min(tm, t)
    half = ya.shape[1]
    act = pl.BlockSpec((tm, half), lambda j, i: (i, 0))
    tile = pl.BlockSpec((tm, tn), lambda j, i: (i, j))
    return pl.pallas_call(
        _out_proj_body,
        out_shape=jax.ShapeDtypeStruct((t, d), F32),
        grid=(d // tn, t // tm),
        in_specs=[act, act, pl.BlockSpec((2 * half, tn), lambda j, i: (0, j)), tile,
                  pl.BlockSpec((1, tn), lambda j, i: (0, j))],
        out_specs=tile,
        scratch_shapes=[pltpu.VMEM((2 * half, tn), BF16)],
        compiler_params=_params(("arbitrary", "arbitrary")),
        name="out_projection",
    )(ya, yb, w_out, x, g1)


def _moe_front_body(x_ref, g_ref, sh_ref, sc_ref, rw_ref, rb_ref, sgu_ref, sd_ref, g2_ref,
                    h_ref, base_ref, te_ref, gate_ref, pos_ref, cnt_ref, carry_ref):
    @pl.when(pl.program_id(0) == 0)
    def _():
        carry_ref[...] = jnp.zeros(carry_ref.shape, F32)

    x = x_ref[...]
    h = _norm_modulate(x, g_ref[...], sh_ref[...], sc_ref[...])
    h_ref[...] = h
    tm = x.shape[0]
    n_exp = rw_ref.shape[1]
    per_group = n_exp // N_GROUPS
    neg_inf = -jnp.inf

    scores = jax.nn.sigmoid(_dot(h, rw_ref[...], precision=HIGHEST))
    sel = scores + rb_ref[...]
    lane = lax.broadcasted_iota(jnp.int32, (tm, n_exp), 1)
    gid = lane // per_group

    gscore = []
    for gi in range(N_GROUPS):
        in_g = gid == gi
        m1 = jnp.max(jnp.where(in_g, sel, neg_inf), axis=-1, keepdims=True)
        i1 = jnp.min(jnp.where(in_g & (sel == m1), lane, n_exp), axis=-1, keepdims=True)
        m2 = jnp.max(jnp.where(in_g & (lane != i1), sel, neg_inf), axis=-1, keepdims=True)
        gscore.append(m1 + m2)
    allowed = jnp.zeros((tm, n_exp), jnp.bool_)
    for gi in range(N_GROUPS):
        beaten = jnp.zeros((tm, 1), jnp.int32)
        for gj in range(N_GROUPS):
            if gj == gi:
                continue
            wins = (gscore[gj] > gscore[gi]) | ((gscore[gj] == gscore[gi]) & (gj < gi))
            beaten = beaten + wins.astype(jnp.int32)
        allowed = allowed | ((gid == gi) & (beaten < TOPK_GROUPS))

    masked = jnp.where(allowed, sel, neg_inf)
    chosen = jnp.zeros((tm, n_exp), F32)
    idxs, gates = [], []
    for _ in range(TOP_K):
        m = jnp.max(masked, axis=-1, keepdims=True)
        idx = jnp.min(jnp.where(masked == m, lane, n_exp), axis=-1, keepdims=True)
        hit = lane == idx
        gates.append(jnp.sum(jnp.where(hit, scores, 0.0), axis=-1, keepdims=True))
        idxs.append(idx)
        chosen = jnp.where(hit, 1.0, chosen)
        masked = jnp.where(hit, neg_inf, masked)
    gsum = gates[0]
    for gk in gates[1:]:
        gsum = gsum + gk

    ri = lax.broadcasted_iota(jnp.int32, (tm, tm), 0)
    ci = lax.broadcasted_iota(jnp.int32, (tm, tm), 1)
    before = (ci < ri).astype(BF16)
    pos = carry_ref[...] + _dot(before, chosen.astype(BF16))
    carry_ref[...] = carry_ref[...] + jnp.sum(chosen, axis=0, keepdims=True)
    cnt_ref[...] = jnp.broadcast_to(carry_ref[...], cnt_ref.shape).astype(jnp.int32)

    te = jnp.zeros((tm, n_exp), jnp.int32)
    gt = jnp.zeros((tm, n_exp), F32)
    ps = jnp.zeros((tm, n_exp), jnp.int32)
    for kk in range(TOP_K):
        slot = lane == kk
        hit = lane == idxs[kk]
        pk = jnp.sum(jnp.where(hit, pos, 0.0), axis=-1, keepdims=True)
        te = jnp.where(slot, idxs[kk], te)
        gt = jnp.where(slot, gates[kk] / gsum * ROUTED_SCALE, gt)
        ps = jnp.where(slot, pk.astype(jnp.int32), ps)
    te_ref[...] = te
    gate_ref[...] = gt
    pos_ref[...] = ps

    hb = h.astype(BF16)
    gu = _dot(hb, sgu_ref[...])
    ds = sd_ref.shape[0]
    act = (jax.nn.silu(gu[:, :ds]) * gu[:, ds:]).astype(BF16)
    base_ref[...] = x + g2_ref[...] * _dot(act, sd_ref[...])


def _moe_front(x1, norm_g, shift, scale, router_w, router_bias, sh_w_gate, sh_w_up, sh_w_down, g2, tm=256):
    t, d = x1.shape
    tm = min(tm, t)
    n_exp = router_w.shape[1]
    ds = sh_w_gate.shape[1]
    sgu = jnp.concatenate([sh_w_gate, sh_w_up], axis=1).astype(BF16)
    row = pl.BlockSpec((tm, d), lambda i: (i, 0))
    vec = pl.BlockSpec((1, d), lambda i: (0, 0))
    small = pl.BlockSpec((tm, n_exp), lambda i: (i, 0))
    return pl.pallas_call(
        _moe_front_body,
        out_shape=[jax.ShapeDtypeStruct((t, d), F32), jax.ShapeDtypeStruct((t, d), F32),
                   jax.ShapeDtypeStruct((t, n_exp), jnp.int32), jax.ShapeDtypeStruct((t, n_exp), F32),
                   jax.ShapeDtypeStruct((t, n_exp), jnp.int32), jax.ShapeDtypeStruct((8, n_exp), jnp.int32)],
        grid=(t // tm,),
        in_specs=[row, vec, vec, vec,
                  pl.BlockSpec((d, n_exp), lambda i: (0, 0)), pl.BlockSpec((1, n_exp), lambda i: (0, 0)),
                  pl.BlockSpec((d, 2 * ds), lambda i: (0, 0)), pl.BlockSpec((ds, d), lambda i: (0, 0)), vec],
        out_specs=[row, row, small, small, small, pl.BlockSpec((8, n_exp), lambda i: (0, 0))],
        scratch_shapes=[pltpu.VMEM((1, n_exp), F32)],
        compiler_params=_params(("arbitrary",)),
        name="moe_front",
    )(x1, norm_g, shift, scale, router_w, router_bias.reshape(1, -1), sgu, sh_w_down.astype(BF16), g2)


def _expert_body(be_ref, nb_ref, tok_ref, tokn_ref, h_hbm, wg_ref, wu_ref, wd_ref, o_ref, xbuf, sem):
    b = pl.program_id(0)
    nb = nb_ref[0]
    rows = xbuf.shape[1]

    def row_copy(tref, slot, r):
        return pltpu.make_async_copy(h_hbm.at[pl.ds(tref[0, 0, r], 1), :],
                                     xbuf.at[slot, pl.ds(r, 1), :], sem.at[slot])

    def issue(tref, slot):
        def body(r, carry):
            row_copy(tref, slot, r).start()
            return carry
        lax.fori_loop(0, rows, body, 0)

    @pl.when(b == 0)
    def _():
        issue(tok_ref, 0)

    @pl.when(b + 1 < nb)
    def _():
        issue(tokn_ref, (b + 1) % 2)

    @pl.when(b < nb)
    def _():
        slot = b % 2
        pltpu.make_async_copy(h_hbm.at[pl.ds(0, rows), :], xbuf.at[slot], sem.at[slot]).wait()
        x = xbuf[slot]
        act = jax.nn.silu(_dot(x, wg_ref[0])) * _dot(x, wu_ref[0])
        o_ref[...] = _dot(act, wd_ref[0])

    @pl.when(b >= nb)
    def _():
        o_ref[...] = jnp.zeros(o_ref.shape, o_ref.dtype)


def _routed_experts(h2, row_tok, blk_expert, n_used, w_gate, w_up, w_down):
    t, d = h2.shape
    n_exp, _, de = w_gate.shape
    rows = MOE_ROWS
    nb = row_tok.shape[0] // rows
    tok3 = row_tok.reshape(nb, 1, rows)

    def last_used(b, nbr):
        return jnp.minimum(b, nbr[0] - 1)

    grid_spec = pltpu.PrefetchScalarGridSpec(
        num_scalar_prefetch=2,
        grid=(nb,),
        in_specs=[
            pl.BlockSpec((1, 1, rows), lambda b, be, nbr: (last_used(b, nbr), 0, 0), memory_space=pltpu.SMEM),
            pl.BlockSpec((1, 1, rows), lambda b, be, nbr: (last_used(b + 1, nbr), 0, 0), memory_space=pltpu.SMEM),
            pl.BlockSpec(memory_space=pl.ANY),
            pl.BlockSpec((1, d, de), lambda b, be, nbr: (be[last_used(b, nbr)], 0, 0)),
            pl.BlockSpec((1, d, de), lambda b, be, nbr: (be[last_used(b, nbr)], 0, 0)),
            pl.BlockSpec((1, de, d), lambda b, be, nbr: (be[last_used(b, nbr)], 0, 0)),
        ],
        out_specs=pl.BlockSpec((rows, d), lambda b, be, nbr: (b, 0)),
        scratch_shapes=[pltpu.VMEM((2, rows, d), F32), pltpu.SemaphoreType.DMA((2,))],
    )
    return pl.pallas_call(
        _expert_body,
        out_shape=jax.ShapeDtypeStruct((nb * rows, d), F32),
        grid_spec=grid_spec,
        compiler_params=_params(("arbitrary",)),
        name="routed_experts",
    )(blk_expert, n_used, tok3, tok3, h2, w_gate, w_up, w_down)


def _combine_body(dst_ref, dstn_ref, y_hbm, gate_ref, base_ref, g2_ref, fg_ref, o_ref, ybuf, sem):
    i = pl.program_id(0)
    n = pl.num_programs(0)
    tm = base_ref.shape[0]

    def issue(dref, slot):
        def body(j, carry):
            tk = j // TOP_K
            kk = j % TOP_K
            pltpu.make_async_copy(y_hbm.at[pl.ds(dref[0, 0, j], 1), :],
                                  ybuf.at[slot, kk, pl.ds(tk, 1), :], sem.at[slot]).start()
            return carry
        lax.fori_loop(0, tm * TOP_K, body, 0)

    @pl.when(i == 0)
    def _():
        issue(dst_ref, 0)

    @pl.when(i + 1 < n)
    def _():
        issue(dstn_ref, (i + 1) % 2)

    slot = i % 2
    for kk in range(TOP_K):
        pltpu.make_async_copy(y_hbm.at[pl.ds(0, tm), :], ybuf.at[slot, kk], sem.at[slot]).wait()
    routed = gate_ref[:, 0:1] * ybuf[slot, 0]
    for kk in range(1, TOP_K):
        routed = routed + gate_ref[:, kk:kk + 1] * ybuf[slot, kk]
    x = base_ref[...] + g2_ref[...] * routed
    ms = jnp.mean(x * x, axis=-1, keepdims=True)
    o_ref[...] = x * lax.rsqrt(ms + NORM_EPS) * fg_ref[...]


def _combine(y_sorted, dest, gate, base, g2, final_g):
    t, d = base.shape
    tm = min(COMBINE_TOKENS, t)
    nt = t // tm
    dest3 = dest.reshape(nt, 1, tm * TOP_K)
    n_exp = gate.shape[1]
    vec = pl.BlockSpec((1, d), lambda i: (0, 0))
    return pl.pallas_call(
        _combine_body,
        out_shape=jax.ShapeDtypeStruct((t, d), F32),
        grid=(nt,),
        in_specs=[pl.BlockSpec((1, 1, tm * TOP_K), lambda i: (i, 0, 0), memory_space=pltpu.SMEM),
                  pl.BlockSpec((1, 1, tm * TOP_K), lambda i: (jnp.minimum(i + 1, nt - 1), 0, 0),
                               memory_space=pltpu.SMEM),
                  pl.BlockSpec(memory_space=pl.ANY),
                  pl.BlockSpec((tm, n_exp), lambda i: (i, 0)),
                  pl.BlockSpec((tm, d), lambda i: (i, 0)), vec, vec],
        out_specs=pl.BlockSpec((tm, d), lambda i: (i, 0)),
        scratch_shapes=[pltpu.VMEM((2, TOP_K, tm, d), F32), pltpu.SemaphoreType.DMA((2,))],
        compiler_params=_params(("arbitrary",)),
        name="moe_combine",
    )(dest3, dest3, y_sorted, gate, base, g2, final_g.reshape(1, -1))


def _routing_tables(top_e, pos_sel, counts, n_tok):
    n_exp = counts.shape[0]
    rows = MOE_ROWS
    padded = (counts + rows - 1) // rows * rows
    ends = jnp.cumsum(padded)
    dest = (ends - padded)[top_e] + pos_sel
    n_rows = n_tok * TOP_K // rows * rows + n_exp * rows
    nb = n_rows // rows
    tok = jnp.broadcast_to(jnp.arange(n_tok, dtype=jnp.int32)[:, None], dest.shape)
    row_tok = jnp.zeros((n_rows,), jnp.int32).at[dest.reshape(-1)].set(tok.reshape(-1))
    blk_expert = jnp.searchsorted(ends, jnp.arange(nb, dtype=jnp.int32) * rows, side="right")
    blk_expert = jnp.minimum(blk_expert, n_exp - 1).astype(jnp.int32)
    n_used = (ends[-1] // rows).astype(jnp.int32).reshape(1)
    return dest.astype(jnp.int32), row_tok, blk_expert, n_used


def _layer(x, c, mod_w, mod_b, norm1_g, norm2_g, w_in, w_out, a_ln_g, a_ln_b, a_spatial_w, a_spatial_b,
           b_shift_mu, b_decay_up, b_decay_base, b_iclr_up, b_iclr_base, b_gate_up, b_kk_scale,
           b_ka_scale, b_bonus, b_gn_g, b_gn_b, router_w, router_bias, exp_w_gate, exp_w_up,
           exp_w_down, sh_w_gate, sh_w_up, sh_w_down):
    t, d = x.shape
    a_width = a_ln_g.shape[0]
    bw = b_decay_base.shape[0]
    lora_w = 512
    lora_col = 2 * a_width + 3 * bw
    assert lora_col % lora_w == 0 and w_in.shape[1] - lora_col <= lora_w

    mod = _mod_vector(c, mod_w, mod_b)
    sh1, sc1, g1, sh2, sc2, g2 = [mod[:, i * d:(i + 1) * d] for i in range(6)]

    h1 = _normmod(x, norm1_g.reshape(1, d), sh1, sc1)
    proj = _in_projection(h1, w_in, tn=lora_w)
    ya = _mixer_a(proj, a_ln_g, a_ln_b, a_spatial_w, a_spatial_b, a_width)
    r, w, k, v, av, bv_, g, bonus_v = _mixer_b_prep(
        proj, 2 * a_width, bw, lora_col, lora_w, b_shift_mu, b_decay_up, b_decay_base, b_iclr_up,
        b_iclr_base, b_gate_up, b_kk_scale, b_ka_scale, b_bonus.reshape(-1))
    y = _rwkv_scan(r, w, k, v, av, bv_)
    yb = _mixer_b_post(y, bonus_v, g, b_gn_g, b_gn_b)
    x1 = _out_projection(ya, yb, w_out, x, g1)

    h2, base, top_e, gate, pos_sel, counts = _moe_front(
        x1, norm2_g.reshape(1, d), sh2, sc2, router_w, router_bias, sh_w_gate, sh_w_up, sh_w_down, g2)
    dest, row_tok, blk_expert, n_used = _routing_tables(top_e[:, :TOP_K], pos_sel[:, :TOP_K], counts[0], t)
    y_sorted = _routed_experts(h2, row_tok, blk_expert, n_used, exp_w_gate, exp_w_up, exp_w_down)
    return base, g2, y_sorted, dest, gate


def kernel(x, c, mod_w, mod_b, norm1_g, norm2_g, w_in, w_out, a_ln_g, a_ln_b, a_spatial_w, a_spatial_b, b_shift_mu, b_decay_up, b_decay_base, b_iclr_up, b_iclr_base, b_gate_up, b_kk_scale, b_ka_scale, b_bonus, b_gn_g, b_gn_b, router_w, router_bias, exp_w_gate, exp_w_up, exp_w_down, sh_w_gate, sh_w_up, sh_w_down, final_g):
    batch, seq, d = x.shape
    assert batch == 1 and mod_w.shape[0] == 1, "single sequence, single layer"
    layer = [p[0] for p in (mod_w, mod_b, norm1_g, norm2_g, w_in, w_out, a_ln_g, a_ln_b, a_spatial_w,
                            a_spatial_b, b_shift_mu, b_decay_up, b_decay_base, b_iclr_up, b_iclr_base,
                            b_gate_up, b_kk_scale, b_ka_scale, b_bonus, b_gn_g, b_gn_b, router_w,
                            router_bias, exp_w_gate, exp_w_up, exp_w_down, sh_w_gate, sh_w_up, sh_w_down)]
    base, g2, y_sorted, dest, gate = _layer(x[0], c, *layer)
    out = _combine(y_sorted, dest, gate, base, g2, final_g)
    return out.reshape(batch, seq, d)
```

```python
import functools

import jax
import jax.numpy as jnp
from jax import lax
from jax.experimental import pallas as pl
from jax.experimental.pallas import tpu as pltpu

A_GROUP_DIM = 128
B_HEAD_DIM = 64
DECAY_LORA = 96
ICLR_LORA = 96
GATE_LORA = 256
TOP_K = 8
N_GROUPS = 8
TOPK_GROUPS = 4
ROUTED_SCALE = 2.5
NORM_EPS = 1e-6
LN_EPS = 1e-5
GN_EPS = 64e-5
KK_EPS = 1e-12

LANES = 128
VMEM_LIMIT_BYTES = 58 * 1024 * 1024
SCAN_CHUNK = 64
MOE_ROWS = 256
COMBINE_TOKENS = 64

F32 = jnp.float32
BF16 = jnp.bfloat16
HIGHEST = lax.Precision.HIGHEST


def _params(sem, vmem=VMEM_LIMIT_BYTES):
    return pltpu.CompilerParams(dimension_semantics=sem, vmem_limit_bytes=vmem)


def _dot(a, b, precision=None):
    return jnp.dot(a, b, preferred_element_type=F32, precision=precision)


def _mod_body(c_ref, w_ref, b_ref, o_ref, s_ref):
    @pl.when(pl.program_id(0) == 0)
    def _():
        c = c_ref[...]
        s_ref[...] = jnp.broadcast_to(c * jax.nn.sigmoid(c), s_ref.shape)

    s = s_ref[...]
    cols = [jnp.sum(w_ref[:, j * LANES:(j + 1) * LANES] * s, axis=0, keepdims=True)
            for j in range(w_ref.shape[1] // LANES)]
    o_ref[...] = jnp.concatenate(cols, axis=1) + b_ref[...]


def _mod_vector(c, mod_w, mod_b, tn=512):
    d, n = mod_w.shape
    return pl.pallas_call(
        _mod_body,
        out_shape=jax.ShapeDtypeStruct((1, n), F32),
        grid=(n // tn,),
        in_specs=[pl.BlockSpec((d, 1), lambda j: (0, 0)),
                  pl.BlockSpec((d, tn), lambda j: (0, j)),
                  pl.BlockSpec((1, tn), lambda j: (0, j))],
        out_specs=pl.BlockSpec((1, tn), lambda j: (0, j)),
        scratch_shapes=[pltpu.VMEM((d, LANES), F32)],
        compiler_params=_params(("arbitrary",)),
        name="mod_vector",
    )(c.reshape(d, 1), mod_w, mod_b.reshape(1, n))


def _norm_modulate(x, g, shift, scale):
    ms = jnp.mean(x * x, axis=-1, keepdims=True)
    y = x * lax.rsqrt(ms + NORM_EPS) * g
    return y * (1.0 + scale) + shift


def _normmod_body(x_ref, g_ref, sh_ref, sc_ref, o_ref):
    o_ref[...] = _norm_modulate(x_ref[...], g_ref[...], sh_ref[...], sc_ref[...]).astype(o_ref.dtype)


def _normmod(x, g, shift, scale, tm=256):
    t, d = x.shape
    tm = min(tm, t)
    row = pl.BlockSpec((tm, d), lambda i: (i, 0))
    vec = pl.BlockSpec((1, d), lambda i: (0, 0))
    return pl.pallas_call(
        _normmod_body,
        out_shape=jax.ShapeDtypeStruct((t, d), BF16),
        grid=(t // tm,),
        in_specs=[row, vec, vec, vec],
        out_specs=row,
        compiler_params=_params(("arbitrary",)),
        name="norm_modulate",
    )(x, g, shift, scale)


def _proj_body(h_ref, wt_ref, o_ref, wb_ref, *, n_valid):
    @pl.when(pl.program_id(1) == 0)
    def _():
        wt = wt_ref[...]
        col = pl.program_id(0) * wt.shape[0] + lax.broadcasted_iota(jnp.int32, wt.shape, 0)
        wb_ref[...] = jnp.where(col < n_valid, wt, 0.0).T.astype(wb_ref.dtype)

    o_ref[...] = _dot(h_ref[...], wb_ref[...])


def _in_projection(h, w_in_t, tm=1024, tn=512):
    t, d = h.shape
    n = w_in_t.shape[0]
    tm = min(tm, t)
    n_pad = pl.cdiv(n, tn) * tn
    return pl.pallas_call(
        functools.partial(_proj_body, n_valid=n),
        out_shape=jax.ShapeDtypeStruct((t, n_pad), F32),
        grid=(n_pad // tn, t // tm),
        in_specs=[pl.BlockSpec((tm, d), lambda j, i: (i, 0)),
                  pl.BlockSpec((tn, d), lambda j, i: (j, 0))],
        out_specs=pl.BlockSpec((tm, tn), lambda j, i: (i, j)),
        scratch_shapes=[pltpu.VMEM((d, tn), BF16)],
        compiler_params=_params(("arbitrary", "arbitrary")),
        name="in_projection",
    )(h, w_in_t)


def _gelu(x):
    return 0.5 * x * (1.0 + lax.erf(x * (2.0 ** -0.5)))


def _mixer_a_body(u_ref, v_ref, lng_ref, lnb_ref, ws_ref, sb_ref, o_ref):
    tm = u_ref.shape[0]
    gd = A_GROUP_DIM
    row = lax.broadcasted_iota(jnp.int32, (gd, gd), 0)
    col = lax.broadcasted_iota(jnp.int32, (gd, gd), 1)
    causal = col <= row
    for g in range(u_ref.shape[1] // gd):
        sl = slice(g * gd, (g + 1) * gd)
        u = _gelu(u_ref[:, sl])
        v = _gelu(v_ref[:, sl])
        mu = jnp.mean(v, axis=-1, keepdims=True)
        dv = v - mu
        var = jnp.mean(dv * dv, axis=-1, keepdims=True)
        vn = ((dv * lax.rsqrt(var + LN_EPS)) * lng_ref[:, sl] + lnb_ref[:, sl]).astype(BF16)
        w = jnp.where(causal, ws_ref[g], 0.0).astype(BF16)
        for n in range(tm // gd):
            rs = slice(n * gd, (n + 1) * gd)
            mixed = _dot(w, vn[rs]) + sb_ref[g]
            o_ref[rs, sl] = (u[rs] * mixed).astype(o_ref.dtype)


def _mixer_a(proj, ln_g, ln_b, spatial_w, spatial_b, a_width, tm=256):
    t = proj.shape[0]
    tm = min(tm, t)
    groups = a_width // A_GROUP_DIM
    sb = jnp.broadcast_to(spatial_b[:, :, None], (groups, A_GROUP_DIM, A_GROUP_DIM))
    vec = pl.BlockSpec((1, a_width), lambda i: (0, 0))
    mat = pl.BlockSpec((groups, A_GROUP_DIM, A_GROUP_DIM), lambda i: (0, 0, 0))
    return pl.pallas_call(
        _mixer_a_body,
        out_shape=jax.ShapeDtypeStruct((t, a_width), BF16),
        grid=(t // tm,),
        in_specs=[pl.BlockSpec((tm, a_width), lambda i: (i, 0)),
                  pl.BlockSpec((tm, a_width), lambda i: (i, 1)),
                  vec, vec, mat, mat],
        out_specs=pl.BlockSpec((tm, a_width), lambda i: (i, 0)),
        compiler_params=_params(("arbitrary",)),
        name="mixer_a",
    )(proj, proj, ln_g.reshape(1, -1), ln_b.reshape(1, -1), spatial_w, sb)


def _head_sums(x, ones_bd):
    parts = [_dot(x[:, s * LANES:(s + 1) * LANES], ones_bd, precision=HIGHEST)
             for s in range(x.shape[1] // LANES)]
    return jnp.concatenate(parts, axis=1)


def _softplus(x):
    return jnp.maximum(x, 0.0) + jnp.log1p(jnp.exp(-jnp.abs(x)))


def _mixer_b_prep_body(r_ref, k_ref, v_ref, l_ref, pr_ref, pk_ref, pv_ref, pl_ref,
                       mur_ref, muk_ref, muv_ref, mul_ref, dup_ref, iup_ref, gup_ref,
                       dbase_ref, ibase_ref, kks_ref, kas_ref, bonus_ref, ones_ref,
                       ro_ref, wo_ref, ko_ref, vo_ref, ao_ref, bo_ref, go_ref, bvo_ref):
    first = pl.program_id(0) == 0

    def shifted(p_ref, prev_ref, mu_ref):
        p = p_ref[...]
        last = prev_ref.shape[0] - 1
        prev_row = jnp.where(first, 0.0, prev_ref[last:last + 1, :])
        rowid = lax.broadcasted_iota(jnp.int32, p.shape, 0)
        prev = jnp.where(rowid == 0, prev_row, pltpu.roll(p, 1, 0))
        return p + mu_ref[...] * (prev - p)

    r = shifted(r_ref, pr_ref, mur_ref)
    k = shifted(k_ref, pk_ref, muk_ref)
    v = shifted(v_ref, pv_ref, muv_ref)
    lora = shifted(l_ref, pl_ref, mul_ref)
    ones_bd = ones_ref[...]

    dec_in = dbase_ref[...] + _dot(jnp.tanh(lora).astype(BF16), dup_ref[...])
    w_log = -_softplus(-dec_in) - 0.5
    log_decay = -jnp.exp(w_log)
    a = jax.nn.sigmoid(ibase_ref[...] + _dot(lora.astype(BF16), iup_ref[...]))
    g = _dot(jax.nn.sigmoid(lora).astype(BF16), gup_ref[...])

    kk = k * kks_ref[...]
    kk = kk / jnp.maximum(jnp.sqrt(_head_sums(kk * kk, ones_bd)), KK_EPS)
    k = k * (1.0 + (a - 1.0) * kas_ref[...])

    ro_ref[...] = r
    wo_ref[...] = log_decay
    ko_ref[...] = k
    vo_ref[...] = v
    ao_ref[...] = -kk
    bo_ref[...] = kk * a
    go_ref[...] = g
    bvo_ref[...] = _head_sums(r * k * bonus_ref[...], ones_bd) * v


def _ones_block_diag():
    i = lax.broadcasted_iota(jnp.int32, (LANES, LANES), 0) // B_HEAD_DIM
    j = lax.broadcasted_iota(jnp.int32, (LANES, LANES), 1) // B_HEAD_DIM
    return (i == j).astype(F32)


def _mixer_b_prep(proj, a_cols, bw, lora_col, lora_w, shift_mu, decay_up, decay_base, iclr_up,
                  iclr_base, gate_up, kk_scale, ka_scale, bonus, tm=256):
    t = proj.shape[0]
    tm = min(tm, t)
    cb = a_cols // bw
    lb = lora_col // lora_w
    n_lora = DECAY_LORA + ICLR_LORA + GATE_LORA

    def pad_rows(w, start):
        return jnp.zeros((lora_w, bw), F32).at[start:start + w.shape[0]].set(w).astype(BF16)

    dup = pad_rows(decay_up, 0)
    iup = pad_rows(iclr_up, DECAY_LORA)
    gup = pad_rows(gate_up, DECAY_LORA + ICLR_LORA)
    mu = shift_mu.reshape(1, -1)
    mu_l = jnp.zeros((1, lora_w), F32).at[:, :n_lora].set(mu[:, 3 * bw:])

    def cur(width, blk):
        return pl.BlockSpec((tm, width), lambda i: (i, blk))

    def prev(width, blk):
        return pl.BlockSpec((8, width), lambda i: (jnp.maximum(i * (tm // 8) - 1, 0), blk))

    vec = pl.BlockSpec((1, bw), lambda i: (0, 0))
    vecl = pl.BlockSpec((1, lora_w), lambda i: (0, 0))
    up = pl.BlockSpec((lora_w, bw), lambda i: (0, 0))
    out = pl.BlockSpec((tm, bw), lambda i: (i, 0))
    row = lambda x: x.reshape(1, -1)
    return pl.pallas_call(
        _mixer_b_prep_body,
        out_shape=[jax.ShapeDtypeStruct((t, bw), F32)] * 8,
        grid=(t // tm,),
        in_specs=[cur(bw, cb), cur(bw, cb + 1), cur(bw, cb + 2), cur(lora_w, lb),
                  prev(bw, cb), prev(bw, cb + 1), prev(bw, cb + 2), prev(lora_w, lb),
                  vec, vec, vec, vecl, up, up, up, vec, vec, vec, vec, vec,
                  pl.BlockSpec((LANES, LANES), lambda i: (0, 0))],
        out_specs=[out] * 8,
        compiler_params=_params(("arbitrary",)),
        name="mixer_b_prep",
    )(proj, proj, proj, proj, proj, proj, proj, proj,
      mu[:, :bw], mu[:, bw:2 * bw], mu[:, 2 * bw:3 * bw], mu_l, dup, iup, gup,
      row(decay_base), row(iclr_base), row(kk_scale), row(ka_scale), row(bonus), _ones_block_diag())


def _split_bf16(x, parts):
    out = []
    for _ in range(parts - 1):
        hi = x.astype(BF16)
        out.append(hi)
        x = x - hi.astype(F32)
    out.append(x.astype(BF16))
    return out


def _scan_body(r_ref, w_ref, k_ref, v_ref, a_ref, b_ref, y_ref, s_ref):
    @pl.when(pl.program_id(1) == 0)
    def _():
        s_ref[...] = jnp.zeros(s_ref.shape, F32)

    c = SCAN_CHUNK
    ti = lax.broadcasted_iota(jnp.int32, (c, c), 0)
    tj = lax.broadcasted_iota(jnp.int32, (c, c), 1)
    tri_incl = (tj <= ti).astype(BF16)
    head0 = lax.broadcasted_iota(jnp.int32, (c, LANES), 1) < B_HEAD_DIM
    i2 = lax.broadcasted_iota(jnp.int32, (2 * c, 2 * c), 0)
    j2 = lax.broadcasted_iota(jnp.int32, (2 * c, 2 * c), 1)
    eye = (i2 == j2).astype(F32)
    strict_bd = (i2 // c == j2 // c) & (j2 < i2)
    ic = lax.broadcasted_iota(jnp.int32, (c, 2 * c), 0)
    jc = lax.broadcasted_iota(jnp.int32, (c, 2 * c), 1)
    incl_cat = (jc % c) <= ic
    contract0 = (((0,), (0,)), ((), ()))
    contract1 = (((1,), (1,)), ((), ()))

    def stack(x):
        return jnp.concatenate([jnp.where(head0, x, 0.0), jnp.where(head0, 0.0, x)], axis=0)

    def cat(xs, axis):
        return jnp.concatenate(xs, axis=axis)

    pairs = range(r_ref.shape[1] // LANES)
    sls = [slice(p * LANES, (p + 1) * LANES) for p in pairs]
    w = [w_ref[:, sl] for sl in sls]
    cum3 = [_dot(tri_incl, cat(_split_bf16(w[p], 3), 1)) for p in pairs]
    cum = [x[:, :LANES] + x[:, LANES:2 * LANES] + x[:, 2 * LANES:] for x in cum3]
    cum_last = [x[c - 1:c, :] for x in cum]
    e_neg = [jnp.exp(-cum[p]) for p in pairs]
    e_tail = [jnp.exp(cum_last[p] - cum[p]) for p in pairs]
    r_t = [r_ref[:, sls[p]] * jnp.exp(cum[p]) for p in pairs]
    a_st = [stack(a_ref[:, sls[p]] * jnp.exp(cum[p] - w[p])) for p in pairs]
    g_c = [jnp.exp(x) for x in cum_last]
    b = [b_ref[:, sl] for sl in sls]
    k = [k_ref[:, sl] for sl in sls]
    v_bd = [stack(v_ref[:, sl]).astype(BF16) for sl in sls]
    lhs = [cat([a_st[p], r_t[p]], 0).astype(BF16) for p in pairs]
    rhs = [cat([stack(b[p] * e_neg[p]), stack(k[p] * e_neg[p])], 0).astype(BF16) for p in pairs]
    big = [lax.dot_general(lhs[p], rhs[p], contract1, preferred_element_type=F32) for p in pairs]
    a_ak = [jnp.where(strict_bd, x[:2 * c, 2 * c:], 0.0).astype(BF16) for x in big]
    a_rb = [jnp.where(incl_cat, x[2 * c:, :2 * c], 0.0).astype(BF16) for x in big]
    a_rk = [jnp.where(incl_cat, x[2 * c:, 2 * c:], 0.0).astype(BF16) for x in big]
    akv = [_dot(a_ak[p], v_bd[p]) for p in pairs]

    x = [jnp.where(strict_bd, y[:2 * c, :2 * c], 0.0) for y in big]
    t_inv = [eye + y for y in x]
    span = 2
    while span < c:
        xb = [y.astype(BF16) for y in x]
        x = [_dot(y, y) for y in xb]
        t_inv = [t_inv[p] + _dot(t_inv[p].astype(BF16), x[p].astype(BF16)) for p in pairs]
        span *= 2

    tub = [_dot(t_inv[p].astype(BF16), cat([a_st[p], akv[p]], 1).astype(BF16)).astype(BF16)
           for p in pairs]
    ry = [_dot(a_rb[p], tub[p]) for p in pairs]
    y0 = [ry[p][:, LANES:] + _dot(a_rk[p], v_bd[p]) for p in pairs]
    r_hat = [r_t[p] + ry[p][:, :LANES] for p in pairs]
    bg = [stack(b[p] * e_tail[p]).astype(BF16) for p in pairs]
    kg = [stack(k[p] * e_tail[p]).astype(BF16) for p in pairs]
    mn = [lax.dot_general(bg[p], tub[p], contract0, preferred_element_type=F32) for p in pairs]
    n = [mn[p][:, LANES:] + lax.dot_general(kg[p], v_bd[p], contract0, preferred_element_type=F32)
         for p in pairs]

    s0 = [s_ref[p] for p in pairs]
    fin = [_dot(cat([r_hat[p], mn[p][:, :LANES]], 0).astype(BF16), cat(_split_bf16(s0[p], 2), 1))
           for p in pairs]
    g_col = [jnp.sum(eye * g_c[p], axis=1, keepdims=True) for p in pairs]
    for p in pairs:
        y_ref[:, sls[p]] = fin[p][:c, :LANES] + fin[p][:c, LANES:] + y0[p]
        s_ref[p] = g_col[p] * s0[p] + (fin[p][c:, :LANES] + fin[p][c:, LANES:]) + n[p]


def _rwkv_scan(r, w, k, v, a, b, pairs_per_step=8):
    t, bw = r.shape
    pw = pairs_per_step * LANES
    blk = pl.BlockSpec((SCAN_CHUNK, pw), lambda g, c: (c, g))
    return pl.pallas_call(
        _scan_body,
        out_shape=jax.ShapeDtypeStruct((t, bw), F32),
        grid=(bw // pw, t // SCAN_CHUNK),
        in_specs=[blk] * 6,
        out_specs=blk,
        scratch_shapes=[pltpu.VMEM((pairs_per_step, LANES, LANES), F32)],
        compiler_params=_params(("arbitrary", "arbitrary")),
        name="rwkv_scan",
    )(r, w, k, v, a, b)


def _mixer_b_post_body(y_ref, bv_ref, g_ref, gng_ref, gnb_ref, ones_ref, o_ref):
    y = y_ref[...]
    ones_bd = ones_ref[...]
    inv_n = 1.0 / B_HEAD_DIM
    mu = _head_sums(y, ones_bd) * inv_n
    d = y - mu
    var = _head_sums(d * d, ones_bd) * inv_n
    yn = (d * lax.rsqrt(var + GN_EPS)) * gng_ref[...] + gnb_ref[...]
    o_ref[...] = ((yn + bv_ref[...]) * g_ref[...]).astype(o_ref.dtype)


def _mixer_b_post(y, bv, g, gn_g, gn_b, tm=256):
    t, bw = y.shape
    tm = min(tm, t)
    row = pl.BlockSpec((tm, bw), lambda i: (i, 0))
    vec = pl.BlockSpec((1, bw), lambda i: (0, 0))
    return pl.pallas_call(
        _mixer_b_post_body,
        out_shape=jax.ShapeDtypeStruct((t, bw), BF16),
        grid=(t // tm,),
        in_specs=[row, row, row, vec, vec, pl.BlockSpec((LANES, LANES), lambda i: (0, 0))],
        out_specs=row,
        compiler_params=_params(("arbitrary",)),
        name="mixer_b_post",
    )(y, bv, g, gn_g.reshape(1, -1), gn_b.reshape(1, -1), _ones_block_diag())


def _out_proj_body(ya_ref, yb_ref, w_ref, x_ref, g1_ref, o_ref, wb_ref):
    @pl.when(pl.program_id(1) == 0)
    def _():
        wb_ref[...] = w_ref[...].astype(wb_ref.dtype)

    half = ya_ref.shape[1]
    mix = _dot(ya_ref[...], wb_ref[:half, :]) + _dot(yb_ref[...], wb_ref[half:, :])
    o_ref[...] = x_ref[...] + g1_ref[...] * mix


def _out_projection(ya, yb, w_out, x, g1, tm=1024, tn=512):
    t, d = x.shape
    tm = min(tm, t)
    half = ya.shape[1]
    act = pl.BlockSpec((tm, half), lambda j, i: (i, 0))
    tile = pl.BlockSpec((tm, tn), lambda j, i: (i, j))
    return pl.pallas_call(
        _out_proj_body,
        out_shape=jax.ShapeDtypeStruct((t, d), F32),
        grid=(d // tn, t // tm),
        in_specs=[act, act, pl.BlockSpec((2 * half, tn), lambda j, i: (0, j)), tile,
                  pl.BlockSpec((1, tn), lambda j, i: (0, j))],
        out_specs=tile,
        scratch_shapes=[pltpu.VMEM((2 * half, tn), BF16)],
        compiler_params=_params(("arbitrary", "arbitrary")),
        name="out_projection",
    )(ya, yb, w_out, x, g1)


def _moe_front_body(x_ref, g_ref, sh_ref, sc_ref, rw_ref, rb_ref, sgu_ref, sd_ref, g2_ref,
                    h_ref, base_ref, te_ref, gate_ref, pos_ref, cnt_ref, carry_ref):
    @pl.when(pl.program_id(0) == 0)
    def _():
        carry_ref[...] = jnp.zeros(carry_ref.shape, F32)

    x = x_ref[...]
    h = _norm_modulate(x, g_ref[...], sh_ref[...], sc_ref[...])
    h_ref[...] = h
    tm = x.shape[0]
    n_exp = rw_ref.shape[1]
    per_group = n_exp // N_GROUPS
    neg_inf = -jnp.inf

    scores = jax.nn.sigmoid(_dot(h, rw_ref[...], precision=HIGHEST))
    sel = scores + rb_ref[...]
    lane = lax.broadcasted_iota(jnp.int32, (tm, n_exp), 1)
    gid = lane // per_group

    gscore = []
    for gi in range(N_GROUPS):
        in_g = gid == gi
        m1 = jnp.max(jnp.where(in_g, sel, neg_inf), axis=-1, keepdims=True)
        i1 = jnp.min(jnp.where(in_g & (sel == m1), lane, n_exp), axis=-1, keepdims=True)
        m2 = jnp.max(jnp.where(in_g & (lane != i1), sel, neg_inf), axis=-1, keepdims=True)
        gscore.append(m1 + m2)
    allowed = jnp.zeros((tm, n_exp), jnp.bool_)
    for gi in range(N_GROUPS):
        beaten = jnp.zeros((tm, 1), jnp.int32)
        for gj in range(N_GROUPS):
            if gj == gi:
                continue
            wins = (gscore[gj] > gscore[gi]) | ((gscore[gj] == gscore[gi]) & (gj < gi))
            beaten = beaten + wins.astype(jnp.int32)
        allowed = allowed | ((gid == gi) & (beaten < TOPK_GROUPS))

    masked = jnp.where(allowed, sel, neg_inf)
    chosen = jnp.zeros((tm, n_exp), F32)
    idxs, gates = [], []
    for _ in range(TOP_K):
        m = jnp.max(masked, axis=-1, keepdims=True)
        idx = jnp.min(jnp.where(masked == m, lane, n_exp), axis=-1, keepdims=True)
        hit = lane == idx
        gates.append(jnp.sum(jnp.where(hit, scores, 0.0), axis=-1, keepdims=True))
        idxs.append(idx)
        chosen = jnp.where(hit, 1.0, chosen)
        masked = jnp.where(hit, neg_inf, masked)
    gsum = gates[0]
    for gk in gates[1:]:
        gsum = gsum + gk

    ri = lax.broadcasted_iota(jnp.int32, (tm, tm), 0)
    ci = lax.broadcasted_iota(jnp.int32, (tm, tm), 1)
    before = (ci < ri).astype(BF16)
    pos = carry_ref[...] + _dot(before, chosen.astype(BF16))
    carry_ref[...] = carry_ref[...] + jnp.sum(chosen, axis=0, keepdims=True)
    cnt_ref[...] = jnp.broadcast_to(carry_ref[...], cnt_ref.shape).astype(jnp.int32)

    te = jnp.zeros((tm, n_exp), jnp.int32)
    gt = jnp.zeros((tm, n_exp), F32)
    ps = jnp.zeros((tm, n_exp), jnp.int32)
    for kk in range(TOP_K):
        slot = lane == kk
        hit = lane == idxs[kk]
        pk = jnp.sum(jnp.where(hit, pos, 0.0), axis=-1, keepdims=True)
        te = jnp.where(slot, idxs[kk], te)
        gt = jnp.where(slot, gates[kk] / gsum * ROUTED_SCALE, gt)
        ps = jnp.where(slot, pk.astype(jnp.int32), ps)
    te_ref[...] = te
    gate_ref[...] = gt
    pos_ref[...] = ps

    hb = h.astype(BF16)
    gu = _dot(hb, sgu_ref[...])
    ds = sd_ref.shape[0]
    act = (jax.nn.silu(gu[:, :ds]) * gu[:, ds:]).astype(BF16)
    base_ref[...] = x + g2_ref[...] * _dot(act, sd_ref[...])


def _moe_front(x1, norm_g, shift, scale, router_w, router_bias, sh_w_gate, sh_w_up, sh_w_down, g2, tm=256):
    t, d = x1.shape
    tm = min(tm, t)
    n_exp = router_w.shape[1]
    ds = sh_w_gate.shape[1]
    sgu = jnp.concatenate([sh_w_gate, sh_w_up], axis=1).astype(BF16)
    row = pl.BlockSpec((tm, d), lambda i: (i, 0))
    vec = pl.BlockSpec((1, d), lambda i: (0, 0))
    small = pl.BlockSpec((tm, n_exp), lambda i: (i, 0))
    return pl.pallas_call(
        _moe_front_body,
        out_shape=[jax.ShapeDtypeStruct((t, d), F32), jax.ShapeDtypeStruct((t, d), F32),
                   jax.ShapeDtypeStruct((t, n_exp), jnp.int32), jax.ShapeDtypeStruct((t, n_exp), F32),
                   jax.ShapeDtypeStruct((t, n_exp), jnp.int32), jax.ShapeDtypeStruct((8, n_exp), jnp.int32)],
        grid=(t // tm,),
        in_specs=[row, vec, vec, vec,
                  pl.BlockSpec((d, n_exp), lambda i: (0, 0)), pl.BlockSpec((1, n_exp), lambda i: (0, 0)),
                  pl.BlockSpec((d, 2 * ds), lambda i: (0, 0)), pl.BlockSpec((ds, d), lambda i: (0, 0)), vec],
        out_specs=[row, row, small, small, small, pl.BlockSpec((8, n_exp), lambda i: (0, 0))],
        scratch_shapes=[pltpu.VMEM((1, n_exp), F32)],
        compiler_params=_params(("arbitrary",)),
        name="moe_front",
    )(x1, norm_g, shift, scale, router_w, router_bias.reshape(1, -1), sgu, sh_w_down.astype(BF16), g2)


def _expert_body(be_ref, nb_ref, tok_ref, tokn_ref, h_hbm, wg_ref, wu_ref, wd_ref, o_ref, xbuf, sem):
    b = pl.program_id(0)
    nb = nb_ref[0]
    rows = xbuf.shape[1]

    def row_copy(tref, slot, r):
        return pltpu.make_async_copy(h_hbm.at[pl.ds(tref[0, 0, r], 1), :],
                                     xbuf.at[slot, pl.ds(r, 1), :], sem.at[slot])

    def issue(tref, slot):
        def body(r, carry):
            row_copy(tref, slot, r).start()
            return carry
        lax.fori_loop(0, rows, body, 0, unroll=8)

    @pl.when(b == 0)
    def _():
        issue(tok_ref, 0)

    @pl.when(b + 1 < nb)
    def _():
        issue(tokn_ref, (b + 1) % 2)

    @pl.when(b < nb)
    def _():
        slot = b % 2
        pltpu.make_async_copy(h_hbm.at[pl.ds(0, rows), :], xbuf.at[slot], sem.at[slot]).wait()
        x = xbuf[slot]
        act = jax.nn.silu(_dot(x, wg_ref[0])) * _dot(x, wu_ref[0])
        o_ref[...] = _dot(act, wd_ref[0])

    @pl.when(b >= nb)
    def _():
        o_ref[...] = jnp.zeros(o_ref.shape, o_ref.dtype)


def _routed_experts(h2, row_tok, blk_expert, n_used, w_gate, w_up, w_down):
    t, d = h2.shape
    n_exp, _, de = w_gate.shape
    rows = MOE_ROWS
    nb = row_tok.shape[0] // rows
    tok3 = row_tok.reshape(nb, 1, rows)

    def last_used(b, nbr):
        return jnp.minimum(b, nbr[0] - 1)

    grid_spec = pltpu.PrefetchScalarGridSpec(
        num_scalar_prefetch=2,
        grid=(nb,),
        in_specs=[
            pl.BlockSpec((1, 1, rows), lambda b, be, nbr: (last_used(b, nbr), 0, 0), memory_space=pltpu.SMEM),
            pl.BlockSpec((1, 1, rows), lambda b, be, nbr: (last_used(b + 1, nbr), 0, 0), memory_space=pltpu.SMEM),
            pl.BlockSpec(memory_space=pl.ANY),
            pl.BlockSpec((1, d, de), lambda b, be, nbr: (be[last_used(b, nbr)], 0, 0)),
            pl.BlockSpec((1, d, de), lambda b, be, nbr: (be[last_used(b, nbr)], 0, 0)),
            pl.BlockSpec((1, de, d), lambda b, be, nbr: (be[last_used(b, nbr)], 0, 0)),
        ],
        out_specs=pl.BlockSpec((rows, d), lambda b, be, nbr: (b, 0)),
        scratch_shapes=[pltpu.VMEM((2, rows, d), F32), pltpu.SemaphoreType.DMA((2,))],
    )
    return pl.pallas_call(
        _expert_body,
        out_shape=jax.ShapeDtypeStruct((nb * rows, d), F32),
        grid_spec=grid_spec,
        compiler_params=_params(("arbitrary",)),
        name="routed_experts",
    )(blk_expert, n_used, tok3, tok3, h2, w_gate, w_up, w_down)


def _combine_body(dst_ref, dstn_ref, y_hbm, gate_ref, base_ref, g2_ref, fg_ref, o_ref, ybuf, sem):
    i = pl.program_id(0)
    n = pl.num_programs(0)
    tm = base_ref.shape[0]

    def issue(dref, slot):
        def body(tk, carry):
            for kk in range(TOP_K):
                pltpu.make_async_copy(y_hbm.at[pl.ds(dref[0, 0, tk * TOP_K + kk], 1), :],
                                      ybuf.at[slot, kk, pl.ds(tk, 1), :], sem.at[slot]).start()
            return carry
        lax.fori_loop(0, tm, body, 0)

    @pl.when(i == 0)
    def _():
        issue(dst_ref, 0)

    @pl.when(i + 1 < n)
    def _():
        issue(dstn_ref, (i + 1) % 2)

    slot = i % 2
    for kk in range(TOP_K):
        pltpu.make_async_copy(y_hbm.at[pl.ds(0, tm), :], ybuf.at[slot, kk], sem.at[slot]).wait()
    routed = gate_ref[:, 0:1] * ybuf[slot, 0]
    for kk in range(1, TOP_K):
        routed = routed + gate_ref[:, kk:kk + 1] * ybuf[slot, kk]
    x = base_ref[...] + g2_ref[...] * routed
    ms = jnp.mean(x * x, axis=-1, keepdims=True)
    o_ref[...] = x * lax.rsqrt(ms + NORM_EPS) * fg_ref[...]


def _combine(y_sorted, dest, gate, base, g2, final_g):
    t, d = base.shape
    tm = min(COMBINE_TOKENS, t)
    nt = t // tm
    dest3 = dest.reshape(nt, 1, tm * TOP_K)
    n_exp = gate.shape[1]
    vec = pl.BlockSpec((1, d), lambda i: (0, 0))
    return pl.pallas_call(
        _combine_body,
        out_shape=jax.ShapeDtypeStruct((t, d), F32),
        grid=(nt,),
        in_specs=[pl.BlockSpec((1, 1, tm * TOP_K), lambda i: (i, 0, 0), memory_space=pltpu.SMEM),
                  pl.BlockSpec((1, 1, tm * TOP_K), lambda i: (jnp.minimum(i + 1, nt - 1), 0, 0),
                               memory_space=pltpu.SMEM),
                  pl.BlockSpec(memory_space=pl.ANY),
                  pl.BlockSpec((tm, n_exp), lambda i: (i, 0)),
                  pl.BlockSpec((tm, d), lambda i: (i, 0)), vec, vec],
        out_specs=pl.BlockSpec((tm, d), lambda i: (i, 0)),
        scratch_shapes=[pltpu.VMEM((2, TOP_K, tm, d), F32), pltpu.SemaphoreType.DMA((2,))],
        compiler_params=_params(("arbitrary",)),
        name="moe_combine",
    )(dest3, dest3, y_sorted, gate, base, g2, final_g.reshape(1, -1))


def _routing_tables(top_e, pos_sel, counts, n_tok):
    n_exp = counts.shape[0]
    rows = MOE_ROWS
    padded = (counts + rows - 1) // rows * rows
    ends = jnp.cumsum(padded)
    starts = ends - padded
    onehot = top_e[..., None] == jnp.arange(n_exp, dtype=jnp.int32)
    dest = jnp.sum(jnp.where(onehot, starts, 0), axis=-1) + pos_sel
    n_rows = n_tok * TOP_K // rows * rows + n_exp * rows
    nb = n_rows // rows
    tok = jnp.broadcast_to(jnp.arange(n_tok, dtype=jnp.int32)[:, None], dest.shape)
    row_tok = jnp.zeros((n_rows,), jnp.int32).at[dest.reshape(-1)].set(tok.reshape(-1))
    blk_expert = jnp.searchsorted(ends, jnp.arange(nb, dtype=jnp.int32) * rows, side="right")
    blk_expert = jnp.minimum(blk_expert, n_exp - 1).astype(jnp.int32)
    n_used = (ends[-1] // rows).astype(jnp.int32).reshape(1)
    return dest.astype(jnp.int32), row_tok, blk_expert, n_used


def _layer(x, c, mod_w, mod_b, norm1_g, norm2_g, w_in, w_out, a_ln_g, a_ln_b, a_spatial_w, a_spatial_b,
           b_shift_mu, b_decay_up, b_decay_base, b_iclr_up, b_iclr_base, b_gate_up, b_kk_scale,
           b_ka_scale, b_bonus, b_gn_g, b_gn_b, router_w, router_bias, exp_w_gate, exp_w_up,
           exp_w_down, sh_w_gate, sh_w_up, sh_w_down):
    t, d = x.shape
    a_width = a_ln_g.shape[0]
    bw = b_decay_base.shape[0]
    lora_w = 512
    lora_col = 2 * a_width + 3 * bw
    assert lora_col % lora_w == 0 and w_in.shape[1] - lora_col <= lora_w

    mod = _mod_vector(c, mod_w, mod_b)
    sh1, sc1, g1, sh2, sc2, g2 = [mod[:, i * d:(i + 1) * d] for i in range(6)]

    h1 = _normmod(x, norm1_g.reshape(1, d), sh1, sc1)
    proj = _in_projection(h1, w_in.T, tn=lora_w)
    ya = _mixer_a(proj, a_ln_g, a_ln_b, a_spatial_w, a_spatial_b, a_width)
    r, w, k, v, av, bv_, g, bonus_v = _mixer_b_prep(
        proj, 2 * a_width, bw, lora_col, lora_w, b_shift_mu, b_decay_up, b_decay_base, b_iclr_up,
        b_iclr_base, b_gate_up, b_kk_scale, b_ka_scale, b_bonus.reshape(-1))
    y = _rwkv_scan(r, w, k, v, av, bv_)
    yb = _mixer_b_post(y, bonus_v, g, b_gn_g, b_gn_b)
    x1 = _out_projection(ya, yb, w_out, x, g1)

    h2, base, top_e, gate, pos_sel, counts = _moe_front(
        x1, norm2_g.reshape(1, d), sh2, sc2, router_w, router_bias, sh_w_gate, sh_w_up, sh_w_down, g2)
    dest, row_tok, blk_expert, n_used = _routing_tables(top_e[:, :TOP_K], pos_sel[:, :TOP_K], counts[0], t)
    y_sorted = _routed_experts(h2, row_tok, blk_expert, n_used, exp_w_gate, exp_w_up, exp_w_down)
    return base, g2, y_sorted, dest, gate


def kernel(x, c, mod_w, mod_b, norm1_g, norm2_g, w_in, w_out, a_ln_g, a_ln_b, a_spatial_w, a_spatial_b, b_shift_mu, b_decay_up, b_decay_base, b_iclr_up, b_iclr_base, b_gate_up, b_kk_scale, b_ka_scale, b_bonus, b_gn_g, b_gn_b, router_w, router_bias, exp_w_gate, exp_w_up, exp_w_down, sh_w_gate, sh_w_up, sh_w_down, final_g):
    batch, seq, d = x.shape
    assert batch == 1 and mod_w.shape[0] == 1, "single sequence, single layer"
    layer = [p[0] for p in (mod_w, mod_b, norm1_g, norm2_g, w_in, w_out, a_ln_g, a_ln_b, a_spatial_w,
                            a_spatial_b, b_shift_mu, b_decay_up, b_decay_base, b_iclr_up, b_iclr_base,
                            b_gate_up, b_kk_scale, b_ka_scale, b_bonus, b_gn_g, b_gn_b, router_w,
                            router_bias, exp_w_gate, exp_w_up, exp_w_down, sh_w_gate, sh_w_up, sh_w_down)]
    base, g2, y_sorted, dest, gate = _layer(x[0], c, *layer)
    out = _combine(y_sorted, dest, gate, base, g2, final_g)
    return out.reshape(batch, seq, d)
```

```python
import functools

import jax
import jax.numpy as jnp
from jax import lax
from jax.experimental import pallas as pl
from jax.experimental.pallas import tpu as pltpu

A_GROUP_DIM = 128
B_HEAD_DIM = 64
DECAY_LORA = 96
ICLR_LORA = 96
GATE_LORA = 256
TOP_K = 8
N_GROUPS = 8
TOPK_GROUPS = 4
ROUTED_SCALE = 2.5
NORM_EPS = 1e-6
LN_EPS = 1e-5
GN_EPS = 64e-5
KK_EPS = 1e-12

LANES = 128
VMEM_LIMIT_BYTES = 58 * 1024 * 1024
SCAN_CHUNK = 64
MOE_ROWS = 256
COMBINE_TOKENS = 64

F32 = jnp.float32
BF16 = jnp.bfloat16
HIGHEST = lax.Precision.HIGHEST


def _params(sem, vmem=VMEM_LIMIT_BYTES):
    return pltpu.CompilerParams(dimension_semantics=sem, vmem_limit_bytes=vmem)


def _dot(a, b, precision=None):
    return jnp.dot(a, b, preferred_element_type=F32, precision=precision)


def _mod_body(c_ref, w_ref, b_ref, o_ref, s_ref):
    @pl.when(pl.program_id(0) == 0)
    def _():
        c = c_ref[...]
        s_ref[...] = jnp.broadcast_to(c * jax.nn.sigmoid(c), s_ref.shape)

    s = s_ref[...]
    cols = [jnp.sum(w_ref[:, j * LANES:(j + 1) * LANES] * s, axis=0, keepdims=True)
            for j in range(w_ref.shape[1] // LANES)]
    o_ref[...] = jnp.concatenate(cols, axis=1) + b_ref[...]


def _mod_vector(c, mod_w, mod_b, tn=512):
    d, n = mod_w.shape
    return pl.pallas_call(
        _mod_body,
        out_shape=jax.ShapeDtypeStruct((1, n), F32),
        grid=(n // tn,),
        in_specs=[pl.BlockSpec((d, 1), lambda j: (0, 0)),
                  pl.BlockSpec((d, tn), lambda j: (0, j)),
                  pl.BlockSpec((1, tn), lambda j: (0, j))],
        out_specs=pl.BlockSpec((1, tn), lambda j: (0, j)),
        scratch_shapes=[pltpu.VMEM((d, LANES), F32)],
        compiler_params=_params(("arbitrary",)),
        name="mod_vector",
    )(c.reshape(d, 1), mod_w, mod_b.reshape(1, n))


def _norm_modulate(x, g, shift, scale):
    ms = jnp.mean(x * x, axis=-1, keepdims=True)
    y = x * lax.rsqrt(ms + NORM_EPS) * g
    return y * (1.0 + scale) + shift


def _normmod_body(x_ref, g_ref, sh_ref, sc_ref, o_ref):
    o_ref[...] = _norm_modulate(x_ref[...], g_ref[...], sh_ref[...], sc_ref[...]).astype(o_ref.dtype)


def _normmod(x, g, shift, scale, tm=256):
    t, d = x.shape
    tm = min(tm, t)
    row = pl.BlockSpec((tm, d), lambda i: (i, 0))
    vec = pl.BlockSpec((1, d), lambda i: (0, 0))
    return pl.pallas_call(
        _normmod_body,
        out_shape=jax.ShapeDtypeStruct((t, d), BF16),
        grid=(t // tm,),
        in_specs=[row, vec, vec, vec],
        out_specs=row,
        compiler_params=_params(("arbitrary",)),
        name="norm_modulate",
    )(x, g, shift, scale)


def _proj_body(h_ref, wt_ref, o_ref, wb_ref, *, n_valid):
    @pl.when(pl.program_id(1) == 0)
    def _():
        wt = wt_ref[...]
        col = pl.program_id(0) * wt.shape[0] + lax.broadcasted_iota(jnp.int32, wt.shape, 0)
        wb_ref[...] = jnp.where(col < n_valid, wt, 0.0).T.astype(wb_ref.dtype)

    o_ref[...] = _dot(h_ref[...], wb_ref[...])


def _in_projection(h, w_in_t, tm=1024, tn=512):
    t, d = h.shape
    n = w_in_t.shape[0]
    tm = min(tm, t)
    n_pad = pl.cdiv(n, tn) * tn
    return pl.pallas_call(
        functools.partial(_proj_body, n_valid=n),
        out_shape=jax.ShapeDtypeStruct((t, n_pad), F32),
        grid=(n_pad // tn, t // tm),
        in_specs=[pl.BlockSpec((tm, d), lambda j, i: (i, 0)),
                  pl.BlockSpec((tn, d), lambda j, i: (j, 0))],
        out_specs=pl.BlockSpec((tm, tn), lambda j, i: (i, j)),
        scratch_shapes=[pltpu.VMEM((d, tn), BF16)],
        compiler_params=_params(("arbitrary", "arbitrary")),
        name="in_projection",
    )(h, w_in_t)


def _gelu(x):
    return 0.5 * x * (1.0 + lax.erf(x * (2.0 ** -0.5)))


def _mixer_a_body(u_ref, v_ref, lng_ref, lnb_ref, ws_ref, sb_ref, o_ref):
    tm = u_ref.shape[0]
    gd = A_GROUP_DIM
    row = lax.broadcasted_iota(jnp.int32, (gd, gd), 0)
    col = lax.broadcasted_iota(jnp.int32, (gd, gd), 1)
    causal = col <= row
    for g in range(u_ref.shape[1] // gd):
        sl = slice(g * gd, (g + 1) * gd)
        u = _gelu(u_ref[:, sl])
        v = _gelu(v_ref[:, sl])
        mu = jnp.mean(v, axis=-1, keepdims=True)
        dv = v - mu
        var = jnp.mean(dv * dv, axis=-1, keepdims=True)
        vn = ((dv * lax.rsqrt(var + LN_EPS)) * lng_ref[:, sl] + lnb_ref[:, sl]).astype(BF16)
        w = jnp.where(causal, ws_ref[g], 0.0).astype(BF16)
        for n in range(tm // gd):
            rs = slice(n * gd, (n + 1) * gd)
            mixed = _dot(w, vn[rs]) + sb_ref[g]
            o_ref[rs, sl] = (u[rs] * mixed).astype(o_ref.dtype)


def _mixer_a(proj, ln_g, ln_b, spatial_w, spatial_b, a_width, tm=256):
    t = proj.shape[0]
    tm = min(tm, t)
    groups = a_width // A_GROUP_DIM
    sb = jnp.broadcast_to(spatial_b[:, :, None], (groups, A_GROUP_DIM, A_GROUP_DIM))
    vec = pl.BlockSpec((1, a_width), lambda i: (0, 0))
    mat = pl.BlockSpec((groups, A_GROUP_DIM, A_GROUP_DIM), lambda i: (0, 0, 0))
    return pl.pallas_call(
        _mixer_a_body,
        out_shape=jax.ShapeDtypeStruct((t, a_width), BF16),
        grid=(t // tm,),
        in_specs=[pl.BlockSpec((tm, a_width), lambda i: (i, 0)),
                  pl.BlockSpec((tm, a_width), lambda i: (i, 1)),
                  vec, vec, mat, mat],
        out_specs=pl.BlockSpec((tm, a_width), lambda i: (i, 0)),
        compiler_params=_params(("arbitrary",)),
        name="mixer_a",
    )(proj, proj, ln_g.reshape(1, -1), ln_b.reshape(1, -1), spatial_w, sb)


def _head_sums(x, ones_bd):
    parts = [_dot(x[:, s * LANES:(s + 1) * LANES], ones_bd, precision=HIGHEST)
             for s in range(x.shape[1] // LANES)]
    return jnp.concatenate(parts, axis=1)


def _softplus(x):
    return jnp.maximum(x, 0.0) + jnp.log1p(jnp.exp(-jnp.abs(x)))


def _mixer_b_prep_body(r_ref, k_ref, v_ref, l_ref, pr_ref, pk_ref, pv_ref, pl_ref,
                       mur_ref, muk_ref, muv_ref, mul_ref, dup_ref, iup_ref, gup_ref,
                       dbase_ref, ibase_ref, kks_ref, kas_ref, bonus_ref, ones_ref,
                       ro_ref, wo_ref, ko_ref, vo_ref, ao_ref, bo_ref, go_ref, bvo_ref):
    first = pl.program_id(0) == 0

    def shifted(p_ref, prev_ref, mu_ref):
        p = p_ref[...]
        last = prev_ref.shape[0] - 1
        prev_row = jnp.where(first, 0.0, prev_ref[last:last + 1, :])
        rowid = lax.broadcasted_iota(jnp.int32, p.shape, 0)
        prev = jnp.where(rowid == 0, prev_row, pltpu.roll(p, 1, 0))
        return p + mu_ref[...] * (prev - p)

    r = shifted(r_ref, pr_ref, mur_ref)
    k = shifted(k_ref, pk_ref, muk_ref)
    v = shifted(v_ref, pv_ref, muv_ref)
    lora = shifted(l_ref, pl_ref, mul_ref)
    ones_bd = ones_ref[...]

    dec_in = dbase_ref[...] + _dot(jnp.tanh(lora).astype(BF16), dup_ref[...])
    w_log = -_softplus(-dec_in) - 0.5
    log_decay = -jnp.exp(w_log)
    a = jax.nn.sigmoid(ibase_ref[...] + _dot(lora.astype(BF16), iup_ref[...]))
    g = _dot(jax.nn.sigmoid(lora).astype(BF16), gup_ref[...])

    kk = k * kks_ref[...]
    kk = kk / jnp.maximum(jnp.sqrt(_head_sums(kk * kk, ones_bd)), KK_EPS)
    k = k * (1.0 + (a - 1.0) * kas_ref[...])

    ro_ref[...] = r
    wo_ref[...] = log_decay
    ko_ref[...] = k
    vo_ref[...] = v
    ao_ref[...] = -kk
    bo_ref[...] = kk * a
    go_ref[...] = g
    bvo_ref[...] = _head_sums(r * k * bonus_ref[...], ones_bd) * v


def _ones_block_diag():
    i = lax.broadcasted_iota(jnp.int32, (LANES, LANES), 0) // B_HEAD_DIM
    j = lax.broadcasted_iota(jnp.int32, (LANES, LANES), 1) // B_HEAD_DIM
    return (i == j).astype(F32)


def _mixer_b_prep(proj, a_cols, bw, lora_col, lora_w, shift_mu, decay_up, decay_base, iclr_up,
                  iclr_base, gate_up, kk_scale, ka_scale, bonus, tm=256):
    t = proj.shape[0]
    tm = min(tm, t)
    cb = a_cols // bw
    lb = lora_col // lora_w
    n_lora = DECAY_LORA + ICLR_LORA + GATE_LORA

    def pad_rows(w, start):
        return jnp.zeros((lora_w, bw), F32).at[start:start + w.shape[0]].set(w).astype(BF16)

    dup = pad_rows(decay_up, 0)
    iup = pad_rows(iclr_up, DECAY_LORA)
    gup = pad_rows(gate_up, DECAY_LORA + ICLR_LORA)
    mu = shift_mu.reshape(1, -1)
    mu_l = jnp.zeros((1, lora_w), F32).at[:, :n_lora].set(mu[:, 3 * bw:])

    def cur(width, blk):
        return pl.BlockSpec((tm, width), lambda i: (i, blk))

    def prev(width, blk):
        return pl.BlockSpec((8, width), lambda i: (jnp.maximum(i * (tm // 8) - 1, 0), blk))

    vec = pl.BlockSpec((1, bw), lambda i: (0, 0))
    vecl = pl.BlockSpec((1, lora_w), lambda i: (0, 0))
    up = pl.BlockSpec((lora_w, bw), lambda i: (0, 0))
    out = pl.BlockSpec((tm, bw), lambda i: (i, 0))
    row = lambda x: x.reshape(1, -1)
    return pl.pallas_call(
        _mixer_b_prep_body,
        out_shape=[jax.ShapeDtypeStruct((t, bw), F32)] * 8,
        grid=(t // tm,),
        in_specs=[cur(bw, cb), cur(bw, cb + 1), cur(bw, cb + 2), cur(lora_w, lb),
                  prev(bw, cb), prev(bw, cb + 1), prev(bw, cb + 2), prev(lora_w, lb),
                  vec, vec, vec, vecl, up, up, up, vec, vec, vec, vec, vec,
                  pl.BlockSpec((LANES, LANES), lambda i: (0, 0))],
        out_specs=[out] * 8,
        compiler_params=_params(("arbitrary",)),
        name="mixer_b_prep",
    )(proj, proj, proj, proj, proj, proj, proj, proj,
      mu[:, :bw], mu[:, bw:2 * bw], mu[:, 2 * bw:3 * bw], mu_l, dup, iup, gup,
      row(decay_base), row(iclr_base), row(kk_scale), row(ka_scale), row(bonus), _ones_block_diag())


def _split_bf16(x, parts):
    out = []
    for _ in range(parts - 1):
        hi = x.astype(BF16)
        out.append(hi)
        x = x - hi.astype(F32)
    out.append(x.astype(BF16))
    return out


def _scan_body(r_ref, w_ref, k_ref, v_ref, a_ref, b_ref, y_ref, s_ref):
    @pl.when(pl.program_id(1) == 0)
    def _():
        s_ref[...] = jnp.zeros(s_ref.shape, F32)

    c = SCAN_CHUNK
    ti = lax.broadcasted_iota(jnp.int32, (c, c), 0)
    tj = lax.broadcasted_iota(jnp.int32, (c, c), 1)
    tri_incl = (tj <= ti).astype(BF16)
    head0 = lax.broadcasted_iota(jnp.int32, (c, LANES), 1) < B_HEAD_DIM
    i2 = lax.broadcasted_iota(jnp.int32, (2 * c, 2 * c), 0)
    j2 = lax.broadcasted_iota(jnp.int32, (2 * c, 2 * c), 1)
    eye = (i2 == j2).astype(F32)
    strict_bd = (i2 // c == j2 // c) & (j2 < i2)
    ic = lax.broadcasted_iota(jnp.int32, (c, 2 * c), 0)
    jc = lax.broadcasted_iota(jnp.int32, (c, 2 * c), 1)
    incl_cat = (jc % c) <= ic
    contract0 = (((0,), (0,)), ((), ()))
    contract1 = (((1,), (1,)), ((), ()))

    def stack(x):
        return jnp.concatenate([jnp.where(head0, x, 0.0), jnp.where(head0, 0.0, x)], axis=0)

    def cat(xs, axis):
        return jnp.concatenate(xs, axis=axis)

    pairs = range(r_ref.shape[1] // LANES)
    sls = [slice(p * LANES, (p + 1) * LANES) for p in pairs]
    w = [w_ref[:, sl] for sl in sls]
    cum3 = [_dot(tri_incl, cat(_split_bf16(w[p], 3), 1)) for p in pairs]
    cum = [x[:, :LANES] + x[:, LANES:2 * LANES] + x[:, 2 * LANES:] for x in cum3]
    cum_last = [x[c - 1:c, :] for x in cum]
    e_neg = [jnp.exp(-cum[p]) for p in pairs]
    e_tail = [jnp.exp(cum_last[p] - cum[p]) for p in pairs]
    r_t = [r_ref[:, sls[p]] * jnp.exp(cum[p]) for p in pairs]
    a_st = [stack(a_ref[:, sls[p]] * jnp.exp(cum[p] - w[p])) for p in pairs]
    g_c = [jnp.exp(x) for x in cum_last]
    b = [b_ref[:, sl] for sl in sls]
    k = [k_ref[:, sl] for sl in sls]
    v_bd = [stack(v_ref[:, sl]).astype(BF16) for sl in sls]
    lhs = [cat([a_st[p], r_t[p]], 0).astype(BF16) for p in pairs]
    rhs = [cat([stack(b[p] * e_neg[p]), stack(k[p] * e_neg[p])], 0).astype(BF16) for p in pairs]
    big = [lax.dot_general(lhs[p], rhs[p], contract1, preferred_element_type=F32) for p in pairs]
    a_ak = [jnp.where(strict_bd, x[:2 * c, 2 * c:], 0.0).astype(BF16) for x in big]
    a_rb = [jnp.where(incl_cat, x[2 * c:, :2 * c], 0.0).astype(BF16) for x in big]
    a_rk = [jnp.where(incl_cat, x[2 * c:, 2 * c:], 0.0).astype(BF16) for x in big]
    akv = [_dot(a_ak[p], v_bd[p]) for p in pairs]

    x = [jnp.where(strict_bd, y[:2 * c, :2 * c], 0.0) for y in big]
    t_inv = [eye + y for y in x]
    span = 2
    while span < c:
        xb = [y.astype(BF16) for y in x]
        x = [_dot(y, y) for y in xb]
        t_inv = [t_inv[p] + _dot(t_inv[p].astype(BF16), x[p].astype(BF16)) for p in pairs]
        span *= 2

    tub = [_dot(t_inv[p].astype(BF16), cat([a_st[p], akv[p]], 1).astype(BF16)).astype(BF16)
           for p in pairs]
    ry = [_dot(a_rb[p], tub[p]) for p in pairs]
    y0 = [ry[p][:, LANES:] + _dot(a_rk[p], v_bd[p]) for p in pairs]
    r_hat = [r_t[p] + ry[p][:, :LANES] for p in pairs]
    bg = [stack(b[p] * e_tail[p]).astype(BF16) for p in pairs]
    kg = [stack(k[p] * e_tail[p]).astype(BF16) for p in pairs]
    mn = [lax.dot_general(bg[p], tub[p], contract0, preferred_element_type=F32) for p in pairs]
    n = [mn[p][:, LANES:] + lax.dot_general(kg[p], v_bd[p], contract0, preferred_element_type=F32)
         for p in pairs]

    s0 = [s_ref[p] for p in pairs]
    fin = [_dot(cat([r_hat[p], mn[p][:, :LANES]], 0).astype(BF16), cat(_split_bf16(s0[p], 2), 1))
           for p in pairs]
    g_col = [jnp.sum(eye * g_c[p], axis=1, keepdims=True) for p in pairs]
    for p in pairs:
        y_ref[:, sls[p]] = fin[p][:c, :LANES] + fin[p][:c, LANES:] + y0[p]
        s_ref[p] = g_col[p] * s0[p] + (fin[p][c:, :LANES] + fin[p][c:, LANES:]) + n[p]


def _rwkv_scan(r, w, k, v, a, b, pairs_per_step=8):
    t, bw = r.shape
    pw = pairs_per_step * LANES
    blk = pl.BlockSpec((SCAN_CHUNK, pw), lambda g, c: (c, g))
    return pl.pallas_call(
        _scan_body,
        out_shape=jax.ShapeDtypeStruct((t, bw), F32),
        grid=(bw // pw, t // SCAN_CHUNK),
        in_specs=[blk] * 6,
        out_specs=blk,
        scratch_shapes=[pltpu.VMEM((pairs_per_step, LANES, LANES), F32)],
        compiler_params=_params(("arbitrary", "arbitrary")),
        name="rwkv_scan",
    )(r, w, k, v, a, b)


def _mixer_b_post_body(y_ref, bv_ref, g_ref, gng_ref, gnb_ref, ones_ref, o_ref):
    y = y_ref[...]
    ones_bd = ones_ref[...]
    inv_n = 1.0 / B_HEAD_DIM
    mu = _head_sums(y, ones_bd) * inv_n
    d = y - mu
    var = _head_sums(d * d, ones_bd) * inv_n
    yn = (d * lax.rsqrt(var + GN_EPS)) * gng_ref[...] + gnb_ref[...]
    o_ref[...] = ((yn + bv_ref[...]) * g_ref[...]).astype(o_ref.dtype)


def _mixer_b_post(y, bv, g, gn_g, gn_b, tm=256):
    t, bw = y.shape
    tm = min(tm, t)
    row = pl.BlockSpec((tm, bw), lambda i: (i, 0))
    vec = pl.BlockSpec((1, bw), lambda i: (0, 0))
    return pl.pallas_call(
        _mixer_b_post_body,
        out_shape=jax.ShapeDtypeStruct((t, bw), BF16),
        grid=(t // tm,),
        in_specs=[row, row, row, vec, vec, pl.BlockSpec((LANES, LANES), lambda i: (0, 0))],
        out_specs=row,
        compiler_params=_params(("arbitrary",)),
        name="mixer_b_post",
    )(y, bv, g, gn_g.reshape(1, -1), gn_b.reshape(1, -1), _ones_block_diag())


def _out_proj_body(ya_ref, yb_ref, w_ref, x_ref, g1_ref, o_ref, wb_ref):
    @pl.when(pl.program_id(1) == 0)
    def _():
        wb_ref[...] = w_ref[...].astype(wb_ref.dtype)

    half = ya_ref.shape[1]
    mix = _dot(ya_ref[...], wb_ref[:half, :]) + _dot(yb_ref[...], wb_ref[half:, :])
    o_ref[...] = x_ref[...] + g1_ref[...] * mix


def _out_projection(ya, yb, w_out, x, g1, tm=1024, tn=512):
    t, d = x.shape
    tm = min(tm, t)
    half = ya.shape[1]
    act = pl.BlockSpec((tm, half), lambda j, i: (i, 0))
    tile = pl.BlockSpec((tm, tn), lambda j, i: (i, j))
    return pl.pallas_call(
        _out_proj_body,
        out_shape=jax.ShapeDtypeStruct((t, d), F32),
        grid=(d // tn, t // tm),
        in_specs=[act, act, pl.BlockSpec((2 * half, tn), lambda j, i: (0, j)), tile,
                  pl.BlockSpec((1, tn), lambda j, i: (0, j))],
        out_specs=tile,
        scratch_shapes=[pltpu.VMEM((2 * half, tn), BF16)],
        compiler_params=_params(("arbitrary", "arbitrary")),
        name="out_projection",
    )(ya, yb, w_out, x, g1)


def _router_body(x_ref, g_ref, sh_ref, sc_ref, rw_ref, rb_ref,
                 te_ref, gate_ref, pos_ref, cnt_ref, carry_ref):
    @pl.when(pl.program_id(0) == 0)
    def _():
        carry_ref[...] = jnp.zeros(carry_ref.shape, F32)

    x = x_ref[...]
    h = _norm_modulate(x, g_ref[...], sh_ref[...], sc_ref[...])
    tm = x.shape[0]
    n_exp = rw_ref.shape[1]
    per_group = n_exp // N_GROUPS
    neg_inf = -jnp.inf

    scores = jax.nn.sigmoid(_dot(h, rw_ref[...], precision=HIGHEST))
    sel = scores + rb_ref[...]
    lane = lax.broadcasted_iota(jnp.int32, (tm, n_exp), 1)
    gid = lane // per_group

    gscore = []
    for gi in range(N_GROUPS):
        in_g = gid == gi
        m1 = jnp.max(jnp.where(in_g, sel, neg_inf), axis=-1, keepdims=True)
        i1 = jnp.min(jnp.where(in_g & (sel == m1), lane, n_exp), axis=-1, keepdims=True)
        m2 = jnp.max(jnp.where(in_g & (lane != i1), sel, neg_inf), axis=-1, keepdims=True)
        gscore.append(m1 + m2)
    allowed = jnp.zeros((tm, n_exp), jnp.bool_)
    for gi in range(N_GROUPS):
        beaten = jnp.zeros((tm, 1), jnp.int32)
        for gj in range(N_GROUPS):
            if gj == gi:
                continue
            wins = (gscore[gj] > gscore[gi]) | ((gscore[gj] == gscore[gi]) & (gj < gi))
            beaten = beaten + wins.astype(jnp.int32)
        allowed = allowed | ((gid == gi) & (beaten < TOPK_GROUPS))

    masked = jnp.where(allowed, sel, neg_inf)
    chosen = jnp.zeros((tm, n_exp), F32)
    idxs, gates = [], []
    for _ in range(TOP_K):
        m = jnp.max(masked, axis=-1, keepdims=True)
        idx = jnp.min(jnp.where(masked == m, lane, n_exp), axis=-1, keepdims=True)
        hit = lane == idx
        gates.append(jnp.sum(jnp.where(hit, scores, 0.0), axis=-1, keepdims=True))
        idxs.append(idx)
        chosen = jnp.where(hit, 1.0, chosen)
        masked = jnp.where(hit, neg_inf, masked)
    gsum = gates[0]
    for gk in gates[1:]:
        gsum = gsum + gk

    ri = lax.broadcasted_iota(jnp.int32, (tm, tm), 0)
    ci = lax.broadcasted_iota(jnp.int32, (tm, tm), 1)
    before = (ci < ri).astype(BF16)
    pos = carry_ref[...] + _dot(before, chosen.astype(BF16))
    carry_ref[...] = carry_ref[...] + jnp.sum(chosen, axis=0, keepdims=True)
    cnt_ref[...] = jnp.broadcast_to(carry_ref[...], cnt_ref.shape).astype(jnp.int32)

    te = jnp.zeros((tm, n_exp), jnp.int32)
    gt = jnp.zeros((tm, n_exp), F32)
    ps = jnp.zeros((tm, n_exp), jnp.int32)
    for kk in range(TOP_K):
        slot = lane == kk
        hit = lane == idxs[kk]
        pk = jnp.sum(jnp.where(hit, pos, 0.0), axis=-1, keepdims=True)
        te = jnp.where(slot, idxs[kk], te)
        gt = jnp.where(slot, gates[kk] / gsum * ROUTED_SCALE, gt)
        ps = jnp.where(slot, pk.astype(jnp.int32), ps)
    te_ref[...] = te
    gate_ref[...] = gt
    pos_ref[...] = ps


def _router(x1, norm_g, shift, scale, router_w, router_bias, tm=256):
    t, d = x1.shape
    tm = min(tm, t)
    n_exp = router_w.shape[1]
    row = pl.BlockSpec((tm, d), lambda i: (i, 0))
    vec = pl.BlockSpec((1, d), lambda i: (0, 0))
    small = pl.BlockSpec((tm, n_exp), lambda i: (i, 0))
    return pl.pallas_call(
        _router_body,
        out_shape=[jax.ShapeDtypeStruct((t, n_exp), jnp.int32), jax.ShapeDtypeStruct((t, n_exp), F32),
                   jax.ShapeDtypeStruct((t, n_exp), jnp.int32), jax.ShapeDtypeStruct((8, n_exp), jnp.int32)],
        grid=(t // tm,),
        in_specs=[row, vec, vec, vec,
                  pl.BlockSpec((d, n_exp), lambda i: (0, 0)), pl.BlockSpec((1, n_exp), lambda i: (0, 0))],
        out_specs=[small, small, small, pl.BlockSpec((8, n_exp), lambda i: (0, 0))],
        scratch_shapes=[pltpu.VMEM((1, n_exp), F32)],
        compiler_params=_params(("arbitrary",)),
        name="moe_router",
    )(x1, norm_g, shift, scale, router_w, router_bias.reshape(1, -1))


def _dest_body(starts_ref, te_ref, pos_ref, o_ref):
    te = te_ref[...]
    acc = jnp.zeros(te.shape, jnp.int32)
    for e in range(starts_ref.shape[0]):
        acc = jnp.where(te == e, starts_ref[e], acc)
    o_ref[...] = acc + pos_ref[...]


def _dest_rows(starts, top_e, pos_sel, tm=512):
    t, n_exp = top_e.shape
    tm = min(tm, t)
    blk = pl.BlockSpec((tm, n_exp), lambda i: (i, 0))
    return pl.pallas_call(
        _dest_body,
        out_shape=jax.ShapeDtypeStruct((t, n_exp), jnp.int32),
        grid=(t // tm,),
        in_specs=[pl.BlockSpec(memory_space=pltpu.SMEM), blk, blk],
        out_specs=blk,
        compiler_params=_params(("arbitrary",)),
        name="moe_dest",
    )(starts, top_e, pos_sel)


def _pack_halves(x):
    half = x.shape[1] // 2
    lo = pltpu.bitcast(x[:, :half].astype(BF16).astype(F32), jnp.uint32)
    hi = pltpu.bitcast(x[:, half:].astype(BF16).astype(F32), jnp.uint32)
    return (lo >> 16) | (hi & jnp.uint32(0xFFFF0000))


def _unpack_halves(p):
    lo = pltpu.bitcast(p << 16, F32)
    hi = pltpu.bitcast(p & jnp.uint32(0xFFFF0000), F32)
    return lo, hi


def _dispatch_body(end_ref, nb_ref, dst_ref, x_ref, g_ref, sh_ref, sc_ref, sgu_ref, sd_ref, g2_ref,
                   base_ref, xs_hbm, hp_buf, zero_buf, sem, zsem):
    i = pl.program_id(0)
    n = pl.num_programs(0)
    tm = x_ref.shape[0]
    rows = zero_buf.shape[0]
    n_exp = end_ref.shape[0]
    n_blocks = xs_hbm.shape[0] // rows

    def drain(slot):
        for _ in range(TOP_K):
            pltpu.make_async_copy(hp_buf.at[slot], xs_hbm.at[pl.ds(0, tm), :], sem.at[slot]).wait()

    @pl.when(i == 0)
    def _():
        zero_buf[...] = jnp.zeros(zero_buf.shape, zero_buf.dtype)
        nb = nb_ref[0]

        def zero_rows(start, count):
            first = start if count == 1 else pl.multiple_of(start, 8)
            return pltpu.make_async_copy(zero_buf.at[pl.ds(0, count), :], xs_hbm.at[pl.ds(first, count), :], zsem)

        def issue(start, stop, count):
            def body(j, carry):
                zero_rows(start + j * count, count).start()
                return carry
            n_copies = (stop - start) // count
            lax.fori_loop(0, n_copies, body, 0)
            return n_copies

        def fill(e, carry):
            n1, n8 = carry
            end = end_ref[e]
            aligned = (end + 7) // 8 * 8
            stop = (end + rows - 1) // rows * rows
            return n1 + issue(end, aligned, 1), n8 + issue(aligned, stop, 8)
        n1, n8 = lax.fori_loop(0, n_exp, fill, (0, 0))
        n_tail = issue(nb * rows, n_blocks * rows, rows)

        for count, n_copies in ((1, n1), (8, n8), (rows, n_tail)):
            def wait(j, carry, count=count):
                zero_rows(0, count).wait()
                return carry
            lax.fori_loop(0, n_copies, wait, 0)

    slot = i % 2

    @pl.when(i >= 2)
    def _():
        drain(slot)

    x = x_ref[...]
    h = _norm_modulate(x, g_ref[...], sh_ref[...], sc_ref[...])
    hp_buf[slot] = _pack_halves(h)

    def body(tk, carry):
        for kk in range(TOP_K):
            pltpu.make_async_copy(hp_buf.at[slot, pl.ds(tk, 1), :],
                                  xs_hbm.at[pl.ds(dst_ref[0, 0, tk * TOP_K + kk], 1), :], sem.at[slot]).start()
        return carry
    lax.fori_loop(0, tm, body, 0)

    gu = _dot(h.astype(BF16), sgu_ref[...])
    ds = sd_ref.shape[0]
    act = (jax.nn.silu(gu[:, :ds]) * gu[:, ds:]).astype(BF16)
    base_ref[...] = x + g2_ref[...] * _dot(act, sd_ref[...])

    @pl.when(i == n - 1)
    def _():
        drain(slot)

        @pl.when(n > 1)
        def _():
            drain(1 - slot)


def _dispatch(x1, norm_g, shift, scale, sh_w_gate, sh_w_up, sh_w_down, g2, seg_end, n_used, dest, n_rows, tm=256):
    t, d = x1.shape
    tm = min(tm, t)
    nt = t // tm
    ds = sh_w_gate.shape[1]
    sgu = jnp.concatenate([sh_w_gate, sh_w_up], axis=1).astype(BF16)
    dest3 = dest.reshape(nt, 1, tm * TOP_K)
    row = pl.BlockSpec((tm, d), lambda i, e, nb: (i, 0))
    vec = pl.BlockSpec((1, d), lambda i, e, nb: (0, 0))
    grid_spec = pltpu.PrefetchScalarGridSpec(
        num_scalar_prefetch=2,
        grid=(nt,),
        in_specs=[pl.BlockSpec((1, 1, tm * TOP_K), lambda i, e, nb: (i, 0, 0), memory_space=pltpu.SMEM),
                  row, vec, vec, vec,
                  pl.BlockSpec((d, 2 * ds), lambda i, e, nb: (0, 0)),
                  pl.BlockSpec((ds, d), lambda i, e, nb: (0, 0)), vec],
        out_specs=[row, pl.BlockSpec(memory_space=pl.ANY)],
        scratch_shapes=[pltpu.VMEM((2, tm, d // 2), jnp.uint32), pltpu.VMEM((MOE_ROWS, d // 2), jnp.uint32),
                        pltpu.SemaphoreType.DMA((2,)), pltpu.SemaphoreType.DMA(())],
    )
    return pl.pallas_call(
        _dispatch_body,
        out_shape=[jax.ShapeDtypeStruct((t, d), F32), jax.ShapeDtypeStruct((n_rows, d // 2), jnp.uint32)],
        grid_spec=grid_spec,
        compiler_params=_params(("arbitrary",)),
        name="moe_dispatch",
    )(seg_end, n_used, dest3, x1, norm_g, shift, scale, sgu, sh_w_down.astype(BF16), g2)


def _expert_body(be_ref, nb_ref, x_ref, wg_ref, wu_ref, wd_ref, o_ref):
    b = pl.program_id(0)
    nb = nb_ref[0]

    @pl.when(b < nb)
    def _():
        lo, hi = _unpack_halves(x_ref[...])
        lo = lo.astype(BF16)
        hi = hi.astype(BF16)
        half = lo.shape[1]
        gate = _dot(lo, wg_ref[0, :half, :].astype(BF16)) + _dot(hi, wg_ref[0, half:, :].astype(BF16))
        up = _dot(lo, wu_ref[0, :half, :].astype(BF16)) + _dot(hi, wu_ref[0, half:, :].astype(BF16))
        act = (jax.nn.silu(gate) * up).astype(BF16)
        o_ref[...] = _pack_halves(_dot(act, wd_ref[0].astype(BF16)))

    @pl.when(b >= nb)
    def _():
        o_ref[...] = jnp.zeros(o_ref.shape, o_ref.dtype)


def _routed_experts(x_sorted, blk_expert, n_used, w_gate, w_up, w_down):
    n_exp, d, de = w_gate.shape
    rows = MOE_ROWS
    nb = x_sorted.shape[0] // rows

    def last_used(b, nbr):
        return jnp.maximum(jnp.minimum(b, nbr[0] - 1), 0)

    grid_spec = pltpu.PrefetchScalarGridSpec(
        num_scalar_prefetch=2,
        grid=(nb,),
        in_specs=[
            pl.BlockSpec((rows, d // 2), lambda b, be, nbr: (last_used(b, nbr), 0)),
            pl.BlockSpec((1, d, de), lambda b, be, nbr: (be[last_used(b, nbr)], 0, 0)),
            pl.BlockSpec((1, d, de), lambda b, be, nbr: (be[last_used(b, nbr)], 0, 0)),
            pl.BlockSpec((1, de, d), lambda b, be, nbr: (be[last_used(b, nbr)], 0, 0)),
        ],
        out_specs=pl.BlockSpec((rows, d // 2), lambda b, be, nbr: (b, 0)),
    )
    return pl.pallas_call(
        _expert_body,
        out_shape=jax.ShapeDtypeStruct((nb * rows, d // 2), jnp.uint32),
        grid_spec=grid_spec,
        compiler_params=_params(("arbitrary",)),
        name="routed_experts",
    )(blk_expert, n_used, x_sorted, w_gate, w_up, w_down)


def _combine_body(dst_ref, dstn_ref, y_hbm, gate_ref, base_ref, g2_ref, fg_ref, o_ref, ybuf, sem):
    i = pl.program_id(0)
    n = pl.num_programs(0)
    tm = base_ref.shape[0]

    def issue(dref, slot):
        def body(tk, carry):
            for kk in range(TOP_K):
                pltpu.make_async_copy(y_hbm.at[pl.ds(dref[0, 0, tk * TOP_K + kk], 1), :],
                                      ybuf.at[slot, kk, pl.ds(tk, 1), :], sem.at[slot]).start()
            return carry
        lax.fori_loop(0, tm, body, 0)

    @pl.when(i == 0)
    def _():
        issue(dst_ref, 0)

    @pl.when(i + 1 < n)
    def _():
        issue(dstn_ref, (i + 1) % 2)

    slot = i % 2
    for kk in range(TOP_K):
        pltpu.make_async_copy(y_hbm.at[pl.ds(0, tm), :], ybuf.at[slot, kk], sem.at[slot]).wait()
    half = ybuf.shape[-1]
    r_lo = r_hi = None
    for kk in range(TOP_K):
        lo, hi = _unpack_halves(ybuf[slot, kk])
        gk = gate_ref[:, kk:kk + 1]
        r_lo = gk * lo if r_lo is None else r_lo + gk * lo
        r_hi = gk * hi if r_hi is None else r_hi + gk * hi
    x_lo = base_ref[:, :half] + g2_ref[:, :half] * r_lo
    x_hi = base_ref[:, half:] + g2_ref[:, half:] * r_hi
    ssq = jnp.sum(x_lo * x_lo, axis=-1, keepdims=True) + jnp.sum(x_hi * x_hi, axis=-1, keepdims=True)
    inv = lax.rsqrt(ssq / (2 * half) + NORM_EPS)
    o_ref[:, :half] = x_lo * inv * fg_ref[:, :half]
    o_ref[:, half:] = x_hi * inv * fg_ref[:, half:]


def _combine(y_sorted, dest, gate, base, g2, final_g):
    t, d = base.shape
    tm = min(COMBINE_TOKENS, t)
    nt = t // tm
    dest3 = dest.reshape(nt, 1, tm * TOP_K)
    n_exp = gate.shape[1]
    vec = pl.BlockSpec((1, d), lambda i: (0, 0))
    return pl.pallas_call(
        _combine_body,
        out_shape=jax.ShapeDtypeStruct((t, d), F32),
        grid=(nt,),
        in_specs=[pl.BlockSpec((1, 1, tm * TOP_K), lambda i: (i, 0, 0), memory_space=pltpu.SMEM),
                  pl.BlockSpec((1, 1, tm * TOP_K), lambda i: (jnp.minimum(i + 1, nt - 1), 0, 0),
                               memory_space=pltpu.SMEM),
                  pl.BlockSpec(memory_space=pl.ANY),
                  pl.BlockSpec((tm, n_exp), lambda i: (i, 0)),
                  pl.BlockSpec((tm, d), lambda i: (i, 0)), vec, vec],
        out_specs=pl.BlockSpec((tm, d), lambda i: (i, 0)),
        scratch_shapes=[pltpu.VMEM((2, TOP_K, tm, d // 2), jnp.uint32), pltpu.SemaphoreType.DMA((2,))],
        compiler_params=_params(("arbitrary",)),
        name="moe_combine",
    )(dest3, dest3, y_sorted, gate, base, g2, final_g.reshape(1, -1))


def _segment_tables(counts, n_tok):
    n_exp = counts.shape[0]
    rows = MOE_ROWS
    padded = (counts + rows - 1) // rows * rows
    ends = jnp.cumsum(padded)
    starts = ends - padded
    nb = n_tok * TOP_K // rows + n_exp + 1
    first_row = jnp.arange(nb, dtype=jnp.int32) * rows
    blk_expert = jnp.sum((ends[None, :] <= first_row[:, None]).astype(jnp.int32), axis=1)
    blk_expert = jnp.minimum(blk_expert, n_exp - 1)
    n_used = (ends[-1] // rows).astype(jnp.int32).reshape(1)
    return starts.astype(jnp.int32), (starts + counts).astype(jnp.int32), blk_expert, n_used, nb * rows


def _layer(x, c, mod_w, mod_b, norm1_g, norm2_g, w_in, w_out, a_ln_g, a_ln_b, a_spatial_w, a_spatial_b,
           b_shift_mu, b_decay_up, b_decay_base, b_iclr_up, b_iclr_base, b_gate_up, b_kk_scale,
           b_ka_scale, b_bonus, b_gn_g, b_gn_b, router_w, router_bias, exp_w_gate, exp_w_up,
           exp_w_down, sh_w_gate, sh_w_up, sh_w_down):
    t, d = x.shape
    a_width = a_ln_g.shape[0]
    bw = b_decay_base.shape[0]
    lora_w = 512
    lora_col = 2 * a_width + 3 * bw
    assert lora_col % lora_w == 0 and w_in.shape[1] - lora_col <= lora_w

    mod = _mod_vector(c, mod_w, mod_b)
    sh1, sc1, g1, sh2, sc2, g2 = [mod[:, i * d:(i + 1) * d] for i in range(6)]

    h1 = _normmod(x, norm1_g.reshape(1, d), sh1, sc1)
    proj = _in_projection(h1, w_in.T, tn=lora_w)
    ya = _mixer_a(proj, a_ln_g, a_ln_b, a_spatial_w, a_spatial_b, a_width)
    r, w, k, v, av, bv_, g, bonus_v = _mixer_b_prep(
        proj, 2 * a_width, bw, lora_col, lora_w, b_shift_mu, b_decay_up, b_decay_base, b_iclr_up,
        b_iclr_base, b_gate_up, b_kk_scale, b_ka_scale, b_bonus.reshape(-1))
    y = _rwkv_scan(r, w, k, v, av, bv_)
    yb = _mixer_b_post(y, bonus_v, g, b_gn_g, b_gn_b)
    x1 = _out_projection(ya, yb, w_out, x, g1)

    n2 = norm2_g.reshape(1, d)
    top_e, gate, pos_sel, counts = _router(x1, n2, sh2, sc2, router_w, router_bias)
    starts, seg_end, blk_expert, n_used, n_rows = _segment_tables(counts[0], t)
    dest = _dest_rows(starts, top_e, pos_sel)[:, :TOP_K]
    base, x_sorted = _dispatch(x1, n2, sh2, sc2, sh_w_gate, sh_w_up, sh_w_down, g2, seg_end, n_used, dest, n_rows)
    y_sorted = _routed_experts(x_sorted, blk_expert, n_used, exp_w_gate, exp_w_up, exp_w_down)
    return base, g2, y_sorted, dest, gate


def kernel(x, c, mod_w, mod_b, norm1_g, norm2_g, w_in, w_out, a_ln_g, a_ln_b, a_spatial_w, a_spatial_b, b_shift_mu, b_decay_up, b_decay_base, b_iclr_up, b_iclr_base, b_gate_up, b_kk_scale, b_ka_scale, b_bonus, b_gn_g, b_gn_b, router_w, router_bias, exp_w_gate, exp_w_up, exp_w_down, sh_w_gate, sh_w_up, sh_w_down, final_g):
    batch, seq, d = x.shape
    assert batch == 1 and mod_w.shape[0] == 1, "single sequence, single layer"
    layer = [p[0] for p in (mod_w, mod_b, norm1_g, norm2_g, w_in, w_out, a_ln_g, a_ln_b, a_spatial_w,
                            a_spatial_b, b_shift_mu, b_decay_up, b_decay_base, b_iclr_up, b_iclr_base,
                            b_gate_up, b_kk_scale, b_ka_scale, b_bonus, b_gn_g, b_gn_b, router_w,
                            router_bias, exp_w_gate, exp_w_up, exp_w_down, sh_w_gate, sh_w_up, sh_w_down)]
    base, g2, y_sorted, dest, gate = _layer(x[0], c, *layer)
    out = _combine(y_sorted, dest, gate, base, g2, final_g)
    return out.reshape(batch, seq, d)
```

```python
import functools

import jax
import jax.numpy as jnp
from jax import lax
from jax.experimental import pallas as pl
from jax.experimental.pallas import tpu as pltpu

A_GROUP_DIM = 128
B_HEAD_DIM = 64
DECAY_LORA = 96
ICLR_LORA = 96
GATE_LORA = 256
TOP_K = 8
N_GROUPS = 8
TOPK_GROUPS = 4
ROUTED_SCALE = 2.5
NORM_EPS = 1e-6
LN_EPS = 1e-5
GN_EPS = 64e-5
KK_EPS = 1e-12

LANES = 128
VMEM_LIMIT_BYTES = 58 * 1024 * 1024
SCAN_CHUNK = 64
MOE_ROWS = 256
COMBINE_TOKENS = 64

F32 = jnp.float32
BF16 = jnp.bfloat16


def _params(sem, vmem=VMEM_LIMIT_BYTES):
    return pltpu.CompilerParams(dimension_semantics=sem, vmem_limit_bytes=vmem)


def _dot(a, b):
    return jnp.dot(a, b, preferred_element_type=F32)


def _mod_body(c_ref, w_ref, b_ref, o_ref, s_ref):
    @pl.when(pl.program_id(0) == 0)
    def _():
        c = c_ref[...]
        s_ref[...] = jnp.broadcast_to(c * jax.nn.sigmoid(c), s_ref.shape)

    s = s_ref[...]
    cols = [jnp.sum(w_ref[:, j * LANES:(j + 1) * LANES] * s, axis=0, keepdims=True)
            for j in range(w_ref.shape[1] // LANES)]
    o_ref[...] = jnp.concatenate(cols, axis=1) + b_ref[...]


def _mod_vector(c, mod_w, mod_b, tn=512):
    d, n = mod_w.shape
    return pl.pallas_call(
        _mod_body,
        out_shape=jax.ShapeDtypeStruct((1, n), F32),
        grid=(n // tn,),
        in_specs=[pl.BlockSpec((d, 1), lambda j: (0, 0)),
                  pl.BlockSpec((d, tn), lambda j: (0, j)),
                  pl.BlockSpec((1, tn), lambda j: (0, j))],
        out_specs=pl.BlockSpec((1, tn), lambda j: (0, j)),
        scratch_shapes=[pltpu.VMEM((d, LANES), F32)],
        compiler_params=_params(("arbitrary",)),
        name="mod_vector",
    )(c.reshape(d, 1), mod_w, mod_b.reshape(1, n))


def _norm_modulate(x, g, shift, scale):
    ms = jnp.mean(x * x, axis=-1, keepdims=True)
    y = x * lax.rsqrt(ms + NORM_EPS) * g
    return y * (1.0 + scale) + shift


def _normmod_body(x_ref, g_ref, sh_ref, sc_ref, o_ref):
    o_ref[...] = _norm_modulate(x_ref[...], g_ref[...], sh_ref[...], sc_ref[...]).astype(o_ref.dtype)


def _normmod(x, g, shift, scale, tm=256):
    t, d = x.shape
    tm = min(tm, t)
    row = pl.BlockSpec((tm, d), lambda i: (i, 0))
    vec = pl.BlockSpec((1, d), lambda i: (0, 0))
    return pl.pallas_call(
        _normmod_body,
        out_shape=jax.ShapeDtypeStruct((t, d), BF16),
        grid=(t // tm,),
        in_specs=[row, vec, vec, vec],
        out_specs=row,
        compiler_params=_params(("arbitrary",)),
        name="norm_modulate",
    )(x, g, shift, scale)


def _proj_body(h_ref, wt_ref, o_ref, wb_ref, *, n_valid):
    @pl.when(pl.program_id(1) == 0)
    def _():
        wt = wt_ref[...]
        col = pl.program_id(0) * wt.shape[0] + lax.broadcasted_iota(jnp.int32, wt.shape, 0)
        wb_ref[...] = jnp.where(col < n_valid, wt, 0.0).T.astype(wb_ref.dtype)

    o_ref[...] = _dot(h_ref[...], wb_ref[...])


def _in_projection(h, w_in_t, tm=1024, tn=512):
    t, d = h.shape
    n = w_in_t.shape[0]
    tm = min(tm, t)
    n_pad = pl.cdiv(n, tn) * tn
    return pl.pallas_call(
        functools.partial(_proj_body, n_valid=n),
        out_shape=jax.ShapeDtypeStruct((t, n_pad), F32),
        grid=(n_pad // tn, t // tm),
        in_specs=[pl.BlockSpec((tm, d), lambda j, i: (i, 0)),
                  pl.BlockSpec((tn, d), lambda j, i: (j, 0))],
        out_specs=pl.BlockSpec((tm, tn), lambda j, i: (i, j)),
        scratch_shapes=[pltpu.VMEM((d, tn), BF16)],
        compiler_params=_params(("arbitrary", "arbitrary")),
        name="in_projection",
    )(h, w_in_t)


def _gelu(x):
    return 0.5 * x * (1.0 + lax.erf(x * (2.0 ** -0.5)))


def _mixer_a_body(u_ref, v_ref, lng_ref, lnb_ref, ws_ref, sb_ref, o_ref):
    tm = u_ref.shape[0]
    gd = A_GROUP_DIM
    row = lax.broadcasted_iota(jnp.int32, (gd, gd), 0)
    col = lax.broadcasted_iota(jnp.int32, (gd, gd), 1)
    causal = col <= row
    for g in range(u_ref.shape[1] // gd):
        sl = slice(g * gd, (g + 1) * gd)
        u = _gelu(u_ref[:, sl])
        v = _gelu(v_ref[:, sl])
        mu = jnp.mean(v, axis=-1, keepdims=True)
        dv = v - mu
        var = jnp.mean(dv * dv, axis=-1, keepdims=True)
        vn = ((dv * lax.rsqrt(var + LN_EPS)) * lng_ref[:, sl] + lnb_ref[:, sl]).astype(BF16)
        w = jnp.where(causal, ws_ref[g], 0.0).astype(BF16)
        for n in range(tm // gd):
            rs = slice(n * gd, (n + 1) * gd)
            mixed = _dot(w, vn[rs]) + sb_ref[g]
            o_ref[rs, sl] = (u[rs] * mixed).astype(o_ref.dtype)


def _mixer_a(proj, ln_g, ln_b, spatial_w, spatial_b, a_width, tm=256):
    t = proj.shape[0]
    tm = min(tm, t)
    groups = a_width // A_GROUP_DIM
    sb = jnp.broadcast_to(spatial_b[:, :, None], (groups, A_GROUP_DIM, A_GROUP_DIM))
    vec = pl.BlockSpec((1, a_width), lambda i: (0, 0))
    mat = pl.BlockSpec((groups, A_GROUP_DIM, A_GROUP_DIM), lambda i: (0, 0, 0))
    return pl.pallas_call(
        _mixer_a_body,
        out_shape=jax.ShapeDtypeStruct((t, a_width), BF16),
        grid=(t // tm,),
        in_specs=[pl.BlockSpec((tm, a_width), lambda i: (i, 0)),
                  pl.BlockSpec((tm, a_width), lambda i: (i, 1)),
                  vec, vec, mat, mat],
        out_specs=pl.BlockSpec((tm, a_width), lambda i: (i, 0)),
        compiler_params=_params(("arbitrary",)),
        name="mixer_a",
    )(proj, proj, ln_g.reshape(1, -1), ln_b.reshape(1, -1), spatial_w, sb)


def _split_bf16(x, parts):
    out = []
    for _ in range(parts - 1):
        hi = x.astype(BF16)
        out.append(hi)
        x = x - hi.astype(F32)
    out.append(x.astype(BF16))
    return out


def _head_sums(x, ones_bd):
    ones_b = ones_bd.astype(BF16)
    pieces = _split_bf16(x, 3)
    parts = []
    for s in range(x.shape[1] // LANES):
        sl = slice(s * LANES, (s + 1) * LANES)
        parts.append(_dot(pieces[0][:, sl], ones_b) + _dot(pieces[1][:, sl], ones_b) + _dot(pieces[2][:, sl], ones_b))
    return jnp.concatenate(parts, axis=1)


def _softplus(x):
    return jnp.maximum(x, 0.0) + jnp.log1p(jnp.exp(-jnp.abs(x)))


def _mixer_b_prep_body(r_ref, k_ref, v_ref, l_ref, pr_ref, pk_ref, pv_ref, pl_ref,
                       mur_ref, muk_ref, muv_ref, mul_ref, dup_ref, iup_ref, gup_ref,
                       dbase_ref, ibase_ref, kks_ref, kas_ref, bonus_ref, ones_ref,
                       ro_ref, wo_ref, ko_ref, vo_ref, ao_ref, bo_ref, go_ref, bvo_ref):
    first = pl.program_id(0) == 0

    def shifted(p_ref, prev_ref, mu_ref):
        p = p_ref[...]
        last = prev_ref.shape[0] - 1
        prev_row = jnp.where(first, 0.0, prev_ref[last:last + 1, :])
        rowid = lax.broadcasted_iota(jnp.int32, p.shape, 0)
        prev = jnp.where(rowid == 0, prev_row, pltpu.roll(p, 1, 0))
        return p + mu_ref[...] * (prev - p)

    r = shifted(r_ref, pr_ref, mur_ref)
    k = shifted(k_ref, pk_ref, muk_ref)
    v = shifted(v_ref, pv_ref, muv_ref)
    lora = shifted(l_ref, pl_ref, mul_ref)
    ones_bd = ones_ref[...]

    dec_in = dbase_ref[...] + _dot(jnp.tanh(lora).astype(BF16), dup_ref[...])
    w_log = -_softplus(-dec_in) - 0.5
    log_decay = -jnp.exp(w_log)
    a = jax.nn.sigmoid(ibase_ref[...] + _dot(lora.astype(BF16), iup_ref[...]))
    g = _dot(jax.nn.sigmoid(lora).astype(BF16), gup_ref[...])

    kk = k * kks_ref[...]
    kk = kk / jnp.maximum(jnp.sqrt(_head_sums(kk * kk, ones_bd)), KK_EPS)
    k = k * (1.0 + (a - 1.0) * kas_ref[...])

    ro_ref[...] = r
    wo_ref[...] = log_decay
    ko_ref[...] = k
    vo_ref[...] = v
    ao_ref[...] = -kk
    bo_ref[...] = kk * a
    go_ref[...] = g
    bvo_ref[...] = _head_sums(r * k * bonus_ref[...], ones_bd) * v


def _ones_block_diag():
    i = lax.broadcasted_iota(jnp.int32, (LANES, LANES), 0) // B_HEAD_DIM
    j = lax.broadcasted_iota(jnp.int32, (LANES, LANES), 1) // B_HEAD_DIM
    return (i == j).astype(F32)


def _mixer_b_prep(proj, a_cols, bw, lora_col, lora_w, shift_mu, decay_up, decay_base, iclr_up,
                  iclr_base, gate_up, kk_scale, ka_scale, bonus, tm=256):
    t = proj.shape[0]
    tm = min(tm, t)
    cb = a_cols // bw
    lb = lora_col // lora_w
    n_lora = DECAY_LORA + ICLR_LORA + GATE_LORA

    def pad_rows(w, start):
        return jnp.zeros((lora_w, bw), F32).at[start:start + w.shape[0]].set(w).astype(BF16)

    dup = pad_rows(decay_up, 0)
    iup = pad_rows(iclr_up, DECAY_LORA)
    gup = pad_rows(gate_up, DECAY_LORA + ICLR_LORA)
    mu = shift_mu.reshape(1, -1)
    mu_l = jnp.zeros((1, lora_w), F32).at[:, :n_lora].set(mu[:, 3 * bw:])

    def cur(width, blk):
        return pl.BlockSpec((tm, width), lambda i: (i, blk))

    def prev(width, blk):
        return pl.BlockSpec((8, width), lambda i: (jnp.maximum(i * (tm // 8) - 1, 0), blk))

    vec = pl.BlockSpec((1, bw), lambda i: (0, 0))
    vecl = pl.BlockSpec((1, lora_w), lambda i: (0, 0))
    up = pl.BlockSpec((lora_w, bw), lambda i: (0, 0))
    out = pl.BlockSpec((tm, bw), lambda i: (i, 0))
    row = lambda x: x.reshape(1, -1)
    return pl.pallas_call(
        _mixer_b_prep_body,
        out_shape=[jax.ShapeDtypeStruct((t, bw), F32)] * 8,
        grid=(t // tm,),
        in_specs=[cur(bw, cb), cur(bw, cb + 1), cur(bw, cb + 2), cur(lora_w, lb),
                  prev(bw, cb), prev(bw, cb + 1), prev(bw, cb + 2), prev(lora_w, lb),
                  vec, vec, vec, vecl, up, up, up, vec, vec, vec, vec, vec,
                  pl.BlockSpec((LANES, LANES), lambda i: (0, 0))],
        out_specs=[out] * 8,
        compiler_params=_params(("arbitrary",)),
        name="mixer_b_prep",
    )(proj, proj, proj, proj, proj, proj, proj, proj,
      mu[:, :bw], mu[:, bw:2 * bw], mu[:, 2 * bw:3 * bw], mu_l, dup, iup, gup,
      row(decay_base), row(iclr_base), row(kk_scale), row(ka_scale), row(bonus), _ones_block_diag())


def _scan_body(r_ref, w_ref, k_ref, v_ref, a_ref, b_ref, y_ref, s_ref):
    @pl.when(pl.program_id(1) == 0)
    def _():
        s_ref[...] = jnp.zeros(s_ref.shape, F32)

    c = SCAN_CHUNK
    ti = lax.broadcasted_iota(jnp.int32, (c, c), 0)
    tj = lax.broadcasted_iota(jnp.int32, (c, c), 1)
    tri_incl = (tj <= ti).astype(BF16)
    head0 = lax.broadcasted_iota(jnp.int32, (c, LANES), 1) < B_HEAD_DIM
    i2 = lax.broadcasted_iota(jnp.int32, (2 * c, 2 * c), 0)
    j2 = lax.broadcasted_iota(jnp.int32, (2 * c, 2 * c), 1)
    eye = (i2 == j2).astype(F32)
    strict_bd = (i2 // c == j2 // c) & (j2 < i2)
    ic = lax.broadcasted_iota(jnp.int32, (c, 2 * c), 0)
    jc = lax.broadcasted_iota(jnp.int32, (c, 2 * c), 1)
    incl_cat = (jc % c) <= ic
    contract0 = (((0,), (0,)), ((), ()))
    contract1 = (((1,), (1,)), ((), ()))

    def stack(x):
        return jnp.concatenate([jnp.where(head0, x, 0.0), jnp.where(head0, 0.0, x)], axis=0)

    def cat(xs, axis):
        return jnp.concatenate(xs, axis=axis)

    pairs = range(r_ref.shape[1] // LANES)
    sls = [slice(p * LANES, (p + 1) * LANES) for p in pairs]
    w = [w_ref[:, sl] for sl in sls]
    cum3 = [_dot(tri_incl, cat(_split_bf16(w[p], 3), 1)) for p in pairs]
    cum = [x[:, :LANES] + x[:, LANES:2 * LANES] + x[:, 2 * LANES:] for x in cum3]
    cum_last = [x[c - 1:c, :] for x in cum]
    e_neg = [jnp.exp(-cum[p]) for p in pairs]
    e_tail = [jnp.exp(cum_last[p] - cum[p]) for p in pairs]
    r_t = [r_ref[:, sls[p]] * jnp.exp(cum[p]) for p in pairs]
    a_st = [stack(a_ref[:, sls[p]] * jnp.exp(cum[p] - w[p])) for p in pairs]
    g_c = [jnp.exp(x) for x in cum_last]
    b = [b_ref[:, sl] for sl in sls]
    k = [k_ref[:, sl] for sl in sls]
    v_bd = [stack(v_ref[:, sl]).astype(BF16) for sl in sls]
    lhs = [cat([a_st[p], r_t[p]], 0).astype(BF16) for p in pairs]
    rhs = [cat([stack(b[p] * e_neg[p]), stack(k[p] * e_neg[p])], 0).astype(BF16) for p in pairs]
    big = [lax.dot_general(lhs[p], rhs[p], contract1, preferred_element_type=F32) for p in pairs]
    a_ak = [jnp.where(strict_bd, x[:2 * c, 2 * c:], 0.0).astype(BF16) for x in big]
    a_rb = [jnp.where(incl_cat, x[2 * c:, :2 * c], 0.0).astype(BF16) for x in big]
    a_rk = [jnp.where(incl_cat, x[2 * c:, 2 * c:], 0.0).astype(BF16) for x in big]
    akv = [_dot(a_ak[p], v_bd[p]) for p in pairs]

    x = [jnp.where(strict_bd, y[:2 * c, :2 * c], 0.0) for y in big]
    t_inv = [eye + y for y in x]
    span = 2
    while span < c:
        xb = [y.astype(BF16) for y in x]
        x = [_dot(y, y) for y in xb]
        t_inv = [t_inv[p] + _dot(t_inv[p].astype(BF16), x[p].astype(BF16)) for p in pairs]
        span *= 2

    tub = [_dot(t_inv[p].astype(BF16), cat([a_st[p], akv[p]], 1).astype(BF16)).astype(BF16)
           for p in pairs]
    ry = [_dot(a_rb[p], tub[p]) for p in pairs]
    y0 = [ry[p][:, LANES:] + _dot(a_rk[p], v_bd[p]) for p in pairs]
    r_hat = [r_t[p] + ry[p][:, :LANES] for p in pairs]
    bg = [stack(b[p] * e_tail[p]).astype(BF16) for p in pairs]
    kg = [stack(k[p] * e_tail[p]).astype(BF16) for p in pairs]
    mn = [lax.dot_general(bg[p], tub[p], contract0, preferred_element_type=F32) for p in pairs]
    n = [mn[p][:, LANES:] + lax.dot_general(kg[p], v_bd[p], contract0, preferred_element_type=F32)
         for p in pairs]

    s0 = [s_ref[p] for p in pairs]
    fin = [_dot(cat([r_hat[p], mn[p][:, :LANES]], 0).astype(BF16), cat(_split_bf16(s0[p], 2), 1))
           for p in pairs]
    g_col = [jnp.sum(eye * g_c[p], axis=1, keepdims=True) for p in pairs]
    for p in pairs:
        y_ref[:, sls[p]] = fin[p][:c, :LANES] + fin[p][:c, LANES:] + y0[p]
        s_ref[p] = g_col[p] * s0[p] + (fin[p][c:, :LANES] + fin[p][c:, LANES:]) + n[p]


def _rwkv_scan(r, w, k, v, a, b, pairs_per_step=16):
    t, bw = r.shape
    pw = pairs_per_step * LANES
    blk = pl.BlockSpec((SCAN_CHUNK, pw), lambda g, c: (c, g))
    return pl.pallas_call(
        _scan_body,
        out_shape=jax.ShapeDtypeStruct((t, bw), F32),
        grid=(bw // pw, t // SCAN_CHUNK),
        in_specs=[blk] * 6,
        out_specs=blk,
        scratch_shapes=[pltpu.VMEM((pairs_per_step, LANES, LANES), F32)],
        compiler_params=_params(("arbitrary", "arbitrary")),
        name="rwkv_scan",
    )(r, w, k, v, a, b)


def _mixer_b_post_body(y_ref, bv_ref, g_ref, gng_ref, gnb_ref, ones_ref, o_ref):
    y = y_ref[...]
    ones_bd = ones_ref[...]
    inv_n = 1.0 / B_HEAD_DIM
    mu = _head_sums(y, ones_bd) * inv_n
    d = y - mu
    var = _head_sums(d * d, ones_bd) * inv_n
    yn = (d * lax.rsqrt(var + GN_EPS)) * gng_ref[...] + gnb_ref[...]
    o_ref[...] = ((yn + bv_ref[...]) * g_ref[...]).astype(o_ref.dtype)


def _mixer_b_post(y, bv, g, gn_g, gn_b, tm=256):
    t, bw = y.shape
    tm = min(tm, t)
    row = pl.BlockSpec((tm, bw), lambda i: (i, 0))
    vec = pl.BlockSpec((1, bw), lambda i: (0, 0))
    return pl.pallas_call(
        _mixer_b_post_body,
        out_shape=jax.ShapeDtypeStruct((t, bw), BF16),
        grid=(t // tm,),
        in_specs=[row, row, row, vec, vec, pl.BlockSpec((LANES, LANES), lambda i: (0, 0))],
        out_specs=row,
        compiler_params=_params(("arbitrary",)),
        name="mixer_b_post",
    )(y, bv, g, gn_g.reshape(1, -1), gn_b.reshape(1, -1), _ones_block_diag())


def _out_proj_body(ya_ref, yb_ref, w_ref, x_ref, g1_ref, o_ref, wb_ref):
    @pl.when(pl.program_id(1) == 0)
    def _():
        wb_ref[...] = w_ref[...].astype(wb_ref.dtype)

    half = ya_ref.shape[1]
    mix = _dot(ya_ref[...], wb_ref[:half, :]) + _dot(yb_ref[...], wb_ref[half:, :])
    o_ref[...] = x_ref[...] + g1_ref[...] * mix


def _out_projection(ya, yb, w_out, x, g1, tm=1024, tn=512):
    t, d = x.shape
    tm = min(tm, t)
    half = ya.shape[1]
    act = pl.BlockSpec((tm, half), lambda j, i: (i, 0))
    tile = pl.BlockSpec((tm, tn), lambda j, i: (i, j))
    return pl.pallas_call(
        _out_proj_body,
        out_shape=jax.ShapeDtypeStruct((t, d), F32),
        grid=(d // tn, t // tm),
        in_specs=[act, act, pl.BlockSpec((2 * half, tn), lambda j, i: (0, j)), tile,
                  pl.BlockSpec((1, tn), lambda j, i: (0, j))],
        out_specs=tile,
        scratch_shapes=[pltpu.VMEM((2 * half, tn), BF16)],
        compiler_params=_params(("arbitrary", "arbitrary")),
        name="out_projection",
    )(ya, yb, w_out, x, g1)


def _router_body(x_ref, g_ref, sh_ref, sc_ref, rw_ref, rb_ref,
                 te_ref, gate_ref, pos_ref, cnt_ref, carry_ref):
    @pl.when(pl.program_id(0) == 0)
    def _():
        carry_ref[...] = jnp.zeros(carry_ref.shape, F32)

    x = x_ref[...]
    h = _norm_modulate(x, g_ref[...], sh_ref[...], sc_ref[...])
    tm = x.shape[0]
    n_exp = rw_ref.shape[1]
    per_group = n_exp // N_GROUPS
    neg_inf = -jnp.inf

    h_hi, h_lo = _split_bf16(h, 2)
    w_hi, w_lo = _split_bf16(rw_ref[...], 2)
    scores = jax.nn.sigmoid(_dot(h_hi, w_hi) + (_dot(h_hi, w_lo) + _dot(h_lo, w_hi)))
    sel = scores + rb_ref[...]
    lane = lax.broadcasted_iota(jnp.int32, (tm, n_exp), 1)
    gid = lane // per_group

    gscore = []
    for gi in range(N_GROUPS):
        in_g = gid == gi
        m1 = jnp.max(jnp.where(in_g, sel, neg_inf), axis=-1, keepdims=True)
        i1 = jnp.min(jnp.where(in_g & (sel == m1), lane, n_exp), axis=-1, keepdims=True)
        m2 = jnp.max(jnp.where(in_g & (lane != i1), sel, neg_inf), axis=-1, keepdims=True)
        gscore.append(m1 + m2)
    allowed = jnp.zeros((tm, n_exp), jnp.bool_)
    for gi in range(N_GROUPS):
        beaten = jnp.zeros((tm, 1), jnp.int32)
        for gj in range(N_GROUPS):
            if gj == gi:
                continue
            wins = (gscore[gj] > gscore[gi]) | ((gscore[gj] == gscore[gi]) & (gj < gi))
            beaten = beaten + wins.astype(jnp.int32)
        allowed = allowed | ((gid == gi) & (beaten < TOPK_GROUPS))

    masked = jnp.where(allowed, sel, neg_inf)
    chosen = jnp.zeros((tm, n_exp), F32)
    idxs, gates = [], []
    for _ in range(TOP_K):
        m = jnp.max(masked, axis=-1, keepdims=True)
        idx = jnp.min(jnp.where(masked == m, lane, n_exp), axis=-1, keepdims=True)
        hit = lane == idx
        gates.append(jnp.sum(jnp.where(hit, scores, 0.0), axis=-1, keepdims=True))
        idxs.append(idx)
        chosen = jnp.where(hit, 1.0, chosen)
        masked = jnp.where(hit, neg_inf, masked)
    gsum = gates[0]
    for gk in gates[1:]:
        gsum = gsum + gk

    ri = lax.broadcasted_iota(jnp.int32, (tm, tm), 0)
    ci = lax.broadcasted_iota(jnp.int32, (tm, tm), 1)
    before = (ci < ri).astype(BF16)
    pos = carry_ref[...] + _dot(before, chosen.astype(BF16))
    carry_ref[...] = carry_ref[...] + jnp.sum(chosen, axis=0, keepdims=True)
    cnt_ref[...] = jnp.broadcast_to(carry_ref[...], cnt_ref.shape).astype(jnp.int32)

    te = jnp.zeros((tm, n_exp), jnp.int32)
    gt = jnp.zeros((tm, n_exp), F32)
    ps = jnp.zeros((tm, n_exp), jnp.int32)
    for kk in range(TOP_K):
        slot = lane == kk
        hit = lane == idxs[kk]
        pk = jnp.sum(jnp.where(hit, pos, 0.0), axis=-1, keepdims=True)
        te = jnp.where(slot, idxs[kk], te)
        gt = jnp.where(slot, gates[kk] / gsum * ROUTED_SCALE, gt)
        ps = jnp.where(slot, pk.astype(jnp.int32), ps)
    te_ref[...] = te
    gate_ref[...] = gt
    pos_ref[...] = ps


def _router(x1, norm_g, shift, scale, router_w, router_bias, tm=256):
    t, d = x1.shape
    tm = min(tm, t)
    n_exp = router_w.shape[1]
    row = pl.BlockSpec((tm, d), lambda i: (i, 0))
    vec = pl.BlockSpec((1, d), lambda i: (0, 0))
    small = pl.BlockSpec((tm, n_exp), lambda i: (i, 0))
    return pl.pallas_call(
        _router_body,
        out_shape=[jax.ShapeDtypeStruct((t, n_exp), jnp.int32), jax.ShapeDtypeStruct((t, n_exp), F32),
                   jax.ShapeDtypeStruct((t, n_exp), jnp.int32), jax.ShapeDtypeStruct((8, n_exp), jnp.int32)],
        grid=(t // tm,),
        in_specs=[row, vec, vec, vec,
                  pl.BlockSpec((d, n_exp), lambda i: (0, 0)), pl.BlockSpec((1, n_exp), lambda i: (0, 0))],
        out_specs=[small, small, small, pl.BlockSpec((8, n_exp), lambda i: (0, 0))],
        scratch_shapes=[pltpu.VMEM((1, n_exp), F32)],
        compiler_params=_params(("arbitrary",)),
        name="moe_router",
    )(x1, norm_g, shift, scale, router_w, router_bias.reshape(1, -1))


def _dest_body(starts_ref, te_ref, pos_ref, o_ref):
    te = te_ref[...]
    acc = jnp.zeros(te.shape, jnp.int32)
    for e in range(starts_ref.shape[0]):
        acc = jnp.where(te == e, starts_ref[e], acc)
    o_ref[...] = acc + pos_ref[...]


def _dest_rows(starts, top_e, pos_sel, tm=512):
    t, n_exp = top_e.shape
    tm = min(tm, t)
    blk = pl.BlockSpec((tm, n_exp), lambda i: (i, 0))
    return pl.pallas_call(
        _dest_body,
        out_shape=jax.ShapeDtypeStruct((t, n_exp), jnp.int32),
        grid=(t // tm,),
        in_specs=[pl.BlockSpec(memory_space=pltpu.SMEM), blk, blk],
        out_specs=blk,
        compiler_params=_params(("arbitrary",)),
        name="moe_dest",
    )(starts, top_e, pos_sel)


def _pack_halves(x):
    half = x.shape[1] // 2
    lo = pltpu.bitcast(x[:, :half].astype(BF16).astype(F32), jnp.uint32)
    hi = pltpu.bitcast(x[:, half:].astype(BF16).astype(F32), jnp.uint32)
    return (lo >> 16) | (hi & jnp.uint32(0xFFFF0000))


def _unpack_halves(p):
    lo = pltpu.bitcast(p << 16, F32)
    hi = pltpu.bitcast(p & jnp.uint32(0xFFFF0000), F32)
    return lo, hi


def _dispatch_body(end_ref, nb_ref, dst_ref, x_ref, g_ref, sh_ref, sc_ref, sgu_ref, sd_ref, g2_ref,
                   base_ref, xs_hbm, hp_buf, zero_buf, sem, zsem):
    i = pl.program_id(0)
    n = pl.num_programs(0)
    tm = x_ref.shape[0]
    rows = zero_buf.shape[0]
    n_exp = end_ref.shape[0]
    n_blocks = xs_hbm.shape[0] // rows

    def drain(slot):
        for _ in range(TOP_K):
            pltpu.make_async_copy(hp_buf.at[slot], xs_hbm.at[pl.ds(0, tm), :], sem.at[slot]).wait()

    @pl.when(i == 0)
    def _():
        zero_buf[...] = jnp.zeros(zero_buf.shape, zero_buf.dtype)
        nb = nb_ref[0]

        def zero_rows(start, count):
            first = start if count == 1 else pl.multiple_of(start, 8)
            return pltpu.make_async_copy(zero_buf.at[pl.ds(0, count), :], xs_hbm.at[pl.ds(first, count), :], zsem)

        def issue(start, stop, count):
            def body(j, carry):
                zero_rows(start + j * count, count).start()
                return carry
            n_copies = (stop - start) // count
            lax.fori_loop(0, n_copies, body, 0)
            return n_copies

        def fill(e, carry):
            n1, n8 = carry
            end = end_ref[e]
            aligned = (end + 7) // 8 * 8
            stop = (end + rows - 1) // rows * rows
            return n1 + issue(end, aligned, 1), n8 + issue(aligned, stop, 8)
        n1, n8 = lax.fori_loop(0, n_exp, fill, (0, 0))
        n_tail = issue(nb * rows, n_blocks * rows, rows)

        for count, n_copies in ((1, n1), (8, n8), (rows, n_tail)):
            def wait(j, carry, count=count):
                zero_rows(0, count).wait()
                return carry
            lax.fori_loop(0, n_copies, wait, 0)

    slot = i % 2

    @pl.when(i >= 2)
    def _():
        drain(slot)

    x = x_ref[...]
    h = _norm_modulate(x, g_ref[...], sh_ref[...], sc_ref[...])
    hp_buf[slot] = _pack_halves(h)

    def body(tk, carry):
        for kk in range(TOP_K):
            pltpu.make_async_copy(hp_buf.at[slot, pl.ds(tk, 1), :],
                                  xs_hbm.at[pl.ds(dst_ref[0, 0, tk * TOP_K + kk], 1), :], sem.at[slot]).start()
        return carry
    lax.fori_loop(0, tm, body, 0)

    gu = _dot(h.astype(BF16), sgu_ref[...])
    ds = sd_ref.shape[0]
    act = (jax.nn.silu(gu[:, :ds]) * gu[:, ds:]).astype(BF16)
    base_ref[...] = x + g2_ref[...] * _dot(act, sd_ref[...])

    @pl.when(i == n - 1)
    def _():
        drain(slot)

        @pl.when(n > 1)
        def _():
            drain(1 - slot)


def _dispatch(x1, norm_g, shift, scale, sh_w_gate, sh_w_up, sh_w_down, g2, seg_end, n_used, dest, n_rows, tm=256):
    t, d = x1.shape
    tm = min(tm, t)
    nt = t // tm
    ds = sh_w_gate.shape[1]
    sgu = jnp.concatenate([sh_w_gate, sh_w_up], axis=1).astype(BF16)
    dest3 = dest.reshape(nt, 1, tm * TOP_K)
    row = pl.BlockSpec((tm, d), lambda i, e, nb: (i, 0))
    vec = pl.BlockSpec((1, d), lambda i, e, nb: (0, 0))
    grid_spec = pltpu.PrefetchScalarGridSpec(
        num_scalar_prefetch=2,
        grid=(nt,),
        in_specs=[pl.BlockSpec((1, 1, tm * TOP_K), lambda i, e, nb: (i, 0, 0), memory_space=pltpu.SMEM),
                  row, vec, vec, vec,
                  pl.BlockSpec((d, 2 * ds), lambda i, e, nb: (0, 0)),
                  pl.BlockSpec((ds, d), lambda i, e, nb: (0, 0)), vec],
        out_specs=[row, pl.BlockSpec(memory_space=pl.ANY)],
        scratch_shapes=[pltpu.VMEM((2, tm, d // 2), jnp.uint32), pltpu.VMEM((MOE_ROWS, d // 2), jnp.uint32),
                        pltpu.SemaphoreType.DMA((2,)), pltpu.SemaphoreType.DMA(())],
    )
    return pl.pallas_call(
        _dispatch_body,
        out_shape=[jax.ShapeDtypeStruct((t, d), F32), jax.ShapeDtypeStruct((n_rows, d // 2), jnp.uint32)],
        grid_spec=grid_spec,
        compiler_params=_params(("arbitrary",)),
        name="moe_dispatch",
    )(seg_end, n_used, dest3, x1, norm_g, shift, scale, sgu, sh_w_down.astype(BF16), g2)


def _expert_body(be_ref, nb_ref, first_ref, next_ref, slot_ref, x_ref, wg_hbm, wu_hbm, wd_hbm, o_ref,
                 wg_buf, wu_buf, wd_buf, sem):
    b = pl.program_id(0)
    nb = nb_ref[0]

    def weight_copies(e, slot):
        return [pltpu.make_async_copy(src.at[e], dst.at[slot], sem.at[slot])
                for src, dst in ((wg_hbm, wg_buf), (wu_hbm, wu_buf), (wd_hbm, wd_buf))]

    @pl.when((b == 0) & (nb > 0))
    def _():
        for cp in weight_copies(be_ref[0], 0):
            cp.start()

    @pl.when(b < nb)
    def _():
        slot = slot_ref[b]

        @pl.when(first_ref[b] == 1)
        def _():
            for cp in weight_copies(0, slot):
                cp.wait()

            @pl.when(next_ref[b] >= 0)
            def _():
                for cp in weight_copies(next_ref[b], 1 - slot):
                    cp.start()

        lo, hi = _unpack_halves(x_ref[...])
        lo = lo.astype(BF16)
        hi = hi.astype(BF16)
        half = lo.shape[1]
        gate = _dot(lo, wg_buf[slot, :half, :].astype(BF16)) + _dot(hi, wg_buf[slot, half:, :].astype(BF16))
        up = _dot(lo, wu_buf[slot, :half, :].astype(BF16)) + _dot(hi, wu_buf[slot, half:, :].astype(BF16))
        act = (jax.nn.silu(gate) * up).astype(BF16)
        o_ref[...] = _pack_halves(_dot(act, wd_buf[slot].astype(BF16)))

    @pl.when(b >= nb)
    def _():
        o_ref[...] = jnp.zeros(o_ref.shape, o_ref.dtype)


def _routed_experts(x_sorted, blk_expert, n_used, blk_first, blk_next, blk_slot, w_gate, w_up, w_down):
    n_exp, d, de = w_gate.shape
    rows = MOE_ROWS
    nb = x_sorted.shape[0] // rows

    def last_used(b, be, nbr, *_):
        return (jnp.maximum(jnp.minimum(b, nbr[0] - 1), 0), 0)

    any_space = pl.BlockSpec(memory_space=pl.ANY)
    grid_spec = pltpu.PrefetchScalarGridSpec(
        num_scalar_prefetch=5,
        grid=(nb,),
        in_specs=[pl.BlockSpec((rows, d // 2), last_used), any_space, any_space, any_space],
        out_specs=pl.BlockSpec((rows, d // 2), lambda b, *_: (b, 0)),
        scratch_shapes=[pltpu.VMEM((2, d, de), F32), pltpu.VMEM((2, d, de), F32), pltpu.VMEM((2, de, d), F32),
                        pltpu.SemaphoreType.DMA((2,))],
    )
    return pl.pallas_call(
        _expert_body,
        out_shape=jax.ShapeDtypeStruct((nb * rows, d // 2), jnp.uint32),
        grid_spec=grid_spec,
        compiler_params=_params(("arbitrary",)),
        name="routed_experts",
    )(blk_expert, n_used, blk_first, blk_next, blk_slot, x_sorted, w_gate, w_up, w_down)


def _combine_body(dst_ref, dstn_ref, y_hbm, gate_ref, base_ref, g2_ref, fg_ref, o_ref, ybuf, sem):
    i = pl.program_id(0)
    n = pl.num_programs(0)
    tm = base_ref.shape[0]

    def issue(dref, slot):
        def body(tk, carry):
            for kk in range(TOP_K):
                pltpu.make_async_copy(y_hbm.at[pl.ds(dref[0, 0, tk * TOP_K + kk], 1), :],
                                      ybuf.at[slot, kk, pl.ds(tk, 1), :], sem.at[slot]).start()
            return carry
        lax.fori_loop(0, tm, body, 0)

    @pl.when(i == 0)
    def _():
        issue(dst_ref, 0)

    @pl.when(i + 1 < n)
    def _():
        issue(dstn_ref, (i + 1) % 2)

    slot = i % 2
    for kk in range(TOP_K):
        pltpu.make_async_copy(y_hbm.at[pl.ds(0, tm), :], ybuf.at[slot, kk], sem.at[slot]).wait()
    half = ybuf.shape[-1]
    r_lo = r_hi = None
    for kk in range(TOP_K):
        lo, hi = _unpack_halves(ybuf[slot, kk])
        gk = gate_ref[:, kk:kk + 1]
        r_lo = gk * lo if r_lo is None else r_lo + gk * lo
        r_hi = gk * hi if r_hi is None else r_hi + gk * hi
    x_lo = base_ref[:, :half] + g2_ref[:, :half] * r_lo
    x_hi = base_ref[:, half:] + g2_ref[:, half:] * r_hi
    ssq = jnp.sum(x_lo * x_lo, axis=-1, keepdims=True) + jnp.sum(x_hi * x_hi, axis=-1, keepdims=True)
    inv = lax.rsqrt(ssq / (2 * half) + NORM_EPS)
    o_ref[:, :half] = x_lo * inv * fg_ref[:, :half]
    o_ref[:, half:] = x_hi * inv * fg_ref[:, half:]


def _combine(y_sorted, dest, gate, base, g2, final_g):
    t, d = base.shape
    tm = min(COMBINE_TOKENS, t)
    nt = t // tm
    dest3 = dest.reshape(nt, 1, tm * TOP_K)
    n_exp = gate.shape[1]
    vec = pl.BlockSpec((1, d), lambda i: (0, 0))
    return pl.pallas_call(
        _combine_body,
        out_shape=jax.ShapeDtypeStruct((t, d), F32),
        grid=(nt,),
        in_specs=[pl.BlockSpec((1, 1, tm * TOP_K), lambda i: (i, 0, 0), memory_space=pltpu.SMEM),
                  pl.BlockSpec((1, 1, tm * TOP_K), lambda i: (jnp.minimum(i + 1, nt - 1), 0, 0),
                               memory_space=pltpu.SMEM),
                  pl.BlockSpec(memory_space=pl.ANY),
                  pl.BlockSpec((tm, n_exp), lambda i: (i, 0)),
                  pl.BlockSpec((tm, d), lambda i: (i, 0)), vec, vec],
        out_specs=pl.BlockSpec((tm, d), lambda i: (i, 0)),
        scratch_shapes=[pltpu.VMEM((2, TOP_K, tm, d // 2), jnp.uint32), pltpu.SemaphoreType.DMA((2,))],
        compiler_params=_params(("arbitrary",)),
        name="moe_combine",
    )(dest3, dest3, y_sorted, gate, base, g2, final_g.reshape(1, -1))


def _segment_tables(counts, n_tok):
    n_exp = counts.shape[0]
    rows = MOE_ROWS
    padded = (counts + rows - 1) // rows * rows
    ends = jnp.cumsum(padded)
    starts = ends - padded
    nb = n_tok * TOP_K // rows + n_exp + 1
    first_row = jnp.arange(nb, dtype=jnp.int32) * rows
    blk_expert = jnp.sum((ends[None, :] <= first_row[:, None]).astype(jnp.int32), axis=1)
    blk_expert = jnp.minimum(blk_expert, n_exp - 1)
    n_used = ends[-1] // rows
    blk = jnp.arange(nb, dtype=jnp.int32)
    prev_expert = jnp.concatenate([jnp.full((1,), -1, jnp.int32), blk_expert[:-1]])
    blk_first = ((blk_expert != prev_expert) & (blk < n_used)).astype(jnp.int32)
    blk_slot = (jnp.cumsum(blk_first) - 1) % 2
    after = ends[blk_expert] // rows
    blk_next = jnp.where(after < n_used, blk_expert[jnp.minimum(after, nb - 1)], -1)
    tables = [t.astype(jnp.int32) for t in (blk_expert, n_used.reshape(1), blk_first, blk_next, blk_slot)]
    return starts.astype(jnp.int32), (starts + counts).astype(jnp.int32), tables, nb * rows


def _layer(x, c, mod_w, mod_b, norm1_g, norm2_g, w_in, w_out, a_ln_g, a_ln_b, a_spatial_w, a_spatial_b,
           b_shift_mu, b_decay_up, b_decay_base, b_iclr_up, b_iclr_base, b_gate_up, b_kk_scale,
           b_ka_scale, b_bonus, b_gn_g, b_gn_b, router_w, router_bias, exp_w_gate, exp_w_up,
           exp_w_down, sh_w_gate, sh_w_up, sh_w_down):
    t, d = x.shape
    a_width = a_ln_g.shape[0]
    bw = b_decay_base.shape[0]
    lora_w = 512
    lora_col = 2 * a_width + 3 * bw
    assert lora_col % lora_w == 0 and w_in.shape[1] - lora_col <= lora_w

    mod = _mod_vector(c, mod_w, mod_b)
    sh1, sc1, g1, sh2, sc2, g2 = [mod[:, i * d:(i + 1) * d] for i in range(6)]

    h1 = _normmod(x, norm1_g.reshape(1, d), sh1, sc1)
    proj = _in_projection(h1, w_in.T, tn=lora_w)
    ya = _mixer_a(proj, a_ln_g, a_ln_b, a_spatial_w, a_spatial_b, a_width)
    r, w, k, v, av, bv_, g, bonus_v = _mixer_b_prep(
        proj, 2 * a_width, bw, lora_col, lora_w, b_shift_mu, b_decay_up, b_decay_base, b_iclr_up,
        b_iclr_base, b_gate_up, b_kk_scale, b_ka_scale, b_bonus.reshape(-1))
    y = _rwkv_scan(r, w, k, v, av, bv_)
    yb = _mixer_b_post(y, bonus_v, g, b_gn_g, b_gn_b)
    x1 = _out_projection(ya, yb, w_out, x, g1)

    n2 = norm2_g.reshape(1, d)
    top_e, gate, pos_sel, counts = _router(x1, n2, sh2, sc2, router_w, router_bias)
    starts, seg_end, blk_tables, n_rows = _segment_tables(counts[0], t)
    dest = _dest_rows(starts, top_e, pos_sel)[:, :TOP_K]
    base, x_sorted = _dispatch(x1, n2, sh2, sc2, sh_w_gate, sh_w_up, sh_w_down, g2, seg_end, blk_tables[1],
                               dest, n_rows)
    y_sorted = _routed_experts(x_sorted, *blk_tables, exp_w_gate, exp_w_up, exp_w_down)
    return base, g2, y_sorted, dest, gate


def kernel(x, c, mod_w, mod_b, norm1_g, norm2_g, w_in, w_out, a_ln_g, a_ln_b, a_spatial_w, a_spatial_b, b_shift_mu, b_decay_up, b_decay_base, b_iclr_up, b_iclr_base, b_gate_up, b_kk_scale, b_ka_scale, b_bonus, b_gn_g, b_gn_b, router_w, router_bias, exp_w_gate, exp_w_up, exp_w_down, sh_w_gate, sh_w_up, sh_w_down, final_g):
    batch, seq, d = x.shape
    assert batch == 1 and mod_w.shape[0] == 1, "single sequence, single layer"
    layer = [p[0] for p in (mod_w, mod_b, norm1_g, norm2_g, w_in, w_out, a_ln_g, a_ln_b, a_spatial_w,
                            a_spatial_b, b_shift_mu, b_decay_up, b_decay_base, b_iclr_up, b_iclr_base,
                            b_gate_up, b_kk_scale, b_ka_scale, b_bonus, b_gn_g, b_gn_b, router_w,
                            router_bias, exp_w_gate, exp_w_up, exp_w_down, sh_w_gate, sh_w_up, sh_w_down)]
    base, g2, y_sorted, dest, gate = _layer(x[0], c, *layer)
    out = _combine(y_sorted, dest, gate, base, g2, final_g)
    return out.reshape(batch, seq, d)
```

```python
import functools

import jax
import jax.numpy as jnp
from jax import lax
from jax.experimental import pallas as pl
from jax.experimental.pallas import tpu as pltpu

A_GROUP_DIM = 128
B_HEAD_DIM = 64
DECAY_LORA = 96
ICLR_LORA = 96
GATE_LORA = 256
TOP_K = 8
N_GROUPS = 8
TOPK_GROUPS = 4
ROUTED_SCALE = 2.5
NORM_EPS = 1e-6
LN_EPS = 1e-5
GN_EPS = 64e-5
KK_EPS = 1e-12

LANES = 128
SUBLANES = 8
VMEM_LIMIT_BYTES = 58 * 1024 * 1024
SCAN_CHUNK = 64
MOE_ROWS = 256
COMBINE_TOKENS = 64

F32 = jnp.float32
BF16 = jnp.bfloat16


def _params(sem, vmem=VMEM_LIMIT_BYTES):
    return pltpu.CompilerParams(dimension_semantics=sem, vmem_limit_bytes=vmem)


def _dot(a, b):
    return jnp.dot(a, b, preferred_element_type=F32)


MXU_WIDTH = 256


def _even_matmul(x, w_ref):
    hm = x.shape[0] // 2
    tiles = w_ref.shape[1] // MXU_WIDTH
    rows = [jnp.concatenate([_dot(x[r * hm:(r + 1) * hm],
                                  w_ref[:, c * MXU_WIDTH:(c + 1) * MXU_WIDTH].astype(BF16))
                             for c in range(tiles)], axis=1) for r in range(2)]
    return jnp.concatenate(rows, axis=0)


def _mod_body(c_ref, w_ref, b_ref, o_ref, s_ref):
    @pl.when(pl.program_id(0) == 0)
    def _():
        c = c_ref[...]
        s_ref[...] = jnp.broadcast_to(c * jax.nn.sigmoid(c), s_ref.shape)

    s = s_ref[...]
    cols = [jnp.sum(w_ref[:, j * LANES:(j + 1) * LANES] * s, axis=0, keepdims=True)
            for j in range(w_ref.shape[1] // LANES)]
    o_ref[...] = jnp.concatenate(cols, axis=1) + b_ref[...]


def _mod_vector(c, mod_w, mod_b, tn=512):
    d, n = mod_w.shape
    return pl.pallas_call(
        _mod_body,
        out_shape=jax.ShapeDtypeStruct((1, n), F32),
        grid=(n // tn,),
        in_specs=[pl.BlockSpec((d, 1), lambda j: (0, 0)),
                  pl.BlockSpec((d, tn), lambda j: (0, j)),
                  pl.BlockSpec((1, tn), lambda j: (0, j))],
        out_specs=pl.BlockSpec((1, tn), lambda j: (0, j)),
        scratch_shapes=[pltpu.VMEM((d, LANES), F32)],
        compiler_params=_params(("arbitrary",)),
        name="mod_vector",
    )(c.reshape(d, 1), mod_w, mod_b.reshape(1, n))


def _norm_modulate(x, g, shift, scale):
    ms = jnp.mean(x * x, axis=-1, keepdims=True)
    y = x * lax.rsqrt(ms + NORM_EPS) * g
    return y * (1.0 + scale) + shift


def _normmod_body(x_ref, g_ref, sh_ref, sc_ref, o_ref):
    o_ref[...] = _norm_modulate(x_ref[...], g_ref[...], sh_ref[...], sc_ref[...]).astype(o_ref.dtype)


def _normmod(x, g, shift, scale, tm=256):
    t, d = x.shape
    tm = min(tm, t)
    row = pl.BlockSpec((tm, d), lambda i: (i, 0))
    vec = pl.BlockSpec((1, d), lambda i: (0, 0))
    return pl.pallas_call(
        _normmod_body,
        out_shape=jax.ShapeDtypeStruct((t, d), BF16),
        grid=(t // tm,),
        in_specs=[row, vec, vec, vec],
        out_specs=row,
        compiler_params=_params(("arbitrary",)),
        name="norm_modulate",
    )(x, g, shift, scale)


def _proj_body(h_ref, wt_ref, o_ref, wb_ref, *, n_valid):
    @pl.when(pl.program_id(1) == 0)
    def _():
        wt = wt_ref[...]
        col = pl.program_id(0) * wt.shape[0] + lax.broadcasted_iota(jnp.int32, wt.shape, 0)
        wb_ref[...] = jnp.where(col < n_valid, wt, 0.0).T.astype(wb_ref.dtype)

    o_ref[...] = _dot(h_ref[...], wb_ref[...])


def _in_projection(h, w_in_t, tm=1024, tn=512):
    t, d = h.shape
    n = w_in_t.shape[0]
    tm = min(tm, t)
    n_pad = pl.cdiv(n, tn) * tn
    return pl.pallas_call(
        functools.partial(_proj_body, n_valid=n),
        out_shape=jax.ShapeDtypeStruct((t, n_pad), F32),
        grid=(n_pad // tn, t // tm),
        in_specs=[pl.BlockSpec((tm, d), lambda j, i: (i, 0)),
                  pl.BlockSpec((tn, d), lambda j, i: (j, 0))],
        out_specs=pl.BlockSpec((tm, tn), lambda j, i: (i, j)),
        scratch_shapes=[pltpu.VMEM((d, tn), BF16)],
        compiler_params=_params(("arbitrary", "arbitrary")),
        name="in_projection",
    )(h, w_in_t)


def _gelu(x):
    return 0.5 * x * (1.0 + lax.erf(x * (2.0 ** -0.5)))


def _mixer_a_body(u_ref, v_ref, lng_ref, lnb_ref, ws_ref, sb_ref, o_ref):
    tm = u_ref.shape[0]
    gd = A_GROUP_DIM
    row = lax.broadcasted_iota(jnp.int32, (gd, gd), 0)
    col = lax.broadcasted_iota(jnp.int32, (gd, gd), 1)
    causal = col <= row
    for g in range(u_ref.shape[1] // gd):
        sl = slice(g * gd, (g + 1) * gd)
        u = _gelu(u_ref[:, sl])
        v = _gelu(v_ref[:, sl])
        mu = jnp.mean(v, axis=-1, keepdims=True)
        dv = v - mu
        var = jnp.mean(dv * dv, axis=-1, keepdims=True)
        vn = ((dv * lax.rsqrt(var + LN_EPS)) * lng_ref[:, sl] + lnb_ref[:, sl]).astype(BF16)
        w = jnp.where(causal, ws_ref[g], 0.0).astype(BF16)
        for n in range(tm // gd):
            rs = slice(n * gd, (n + 1) * gd)
            mixed = _dot(w, vn[rs]) + sb_ref[g]
            o_ref[rs, sl] = (u[rs] * mixed).astype(o_ref.dtype)


def _mixer_a(proj, ln_g, ln_b, spatial_w, spatial_b, a_width, tm=256):
    t = proj.shape[0]
    tm = min(tm, t)
    groups = a_width // A_GROUP_DIM
    sb = jnp.broadcast_to(spatial_b[:, :, None], (groups, A_GROUP_DIM, A_GROUP_DIM))
    vec = pl.BlockSpec((1, a_width), lambda i: (0, 0))
    mat = pl.BlockSpec((groups, A_GROUP_DIM, A_GROUP_DIM), lambda i: (0, 0, 0))
    return pl.pallas_call(
        _mixer_a_body,
        out_shape=jax.ShapeDtypeStruct((t, a_width), BF16),
        grid=(t // tm,),
        in_specs=[pl.BlockSpec((tm, a_width), lambda i: (i, 0)),
                  pl.BlockSpec((tm, a_width), lambda i: (i, 1)),
                  vec, vec, mat, mat],
        out_specs=pl.BlockSpec((tm, a_width), lambda i: (i, 0)),
        compiler_params=_params(("arbitrary",)),
        name="mixer_a",
    )(proj, proj, ln_g.reshape(1, -1), ln_b.reshape(1, -1), spatial_w, sb)


def _split_bf16(x, parts):
    out = []
    for _ in range(parts - 1):
        hi = x.astype(BF16)
        out.append(hi)
        x = x - hi.astype(F32)
    out.append(x.astype(BF16))
    return out


def _head_sums(x, ones_bd):
    ones_b = ones_bd.astype(BF16)
    pieces = _split_bf16(x, 3)
    parts = []
    for s in range(x.shape[1] // LANES):
        sl = slice(s * LANES, (s + 1) * LANES)
        parts.append(_dot(pieces[0][:, sl], ones_b) + _dot(pieces[1][:, sl], ones_b) + _dot(pieces[2][:, sl], ones_b))
    return jnp.concatenate(parts, axis=1)


def _softplus(x):
    return jnp.maximum(x, 0.0) + jnp.log1p(jnp.exp(-jnp.abs(x)))


def _mixer_b_prep_body(r_ref, k_ref, v_ref, l_ref, pr_ref, pk_ref, pv_ref, pl_ref,
                       mur_ref, muk_ref, muv_ref, mul_ref, dup_ref, iup_ref, gup_ref,
                       dbase_ref, ibase_ref, kks_ref, kas_ref, bonus_ref, ones_ref,
                       ro_ref, wo_ref, ko_ref, vo_ref, ao_ref, bo_ref, go_ref, bvo_ref):
    first = pl.program_id(0) == 0

    def shifted(p_ref, prev_ref, mu_ref):
        p = p_ref[...]
        last = prev_ref.shape[0] - 1
        prev_row = jnp.where(first, 0.0, prev_ref[last:last + 1, :])
        rowid = lax.broadcasted_iota(jnp.int32, p.shape, 0)
        prev = jnp.where(rowid == 0, prev_row, pltpu.roll(p, 1, 0))
        return p + mu_ref[...] * (prev - p)

    r = shifted(r_ref, pr_ref, mur_ref)
    k = shifted(k_ref, pk_ref, muk_ref)
    v = shifted(v_ref, pv_ref, muv_ref)
    lora = shifted(l_ref, pl_ref, mul_ref)
    ones_bd = ones_ref[...]

    dec_in = dbase_ref[...] + _dot(jnp.tanh(lora).astype(BF16), dup_ref[...])
    w_log = -_softplus(-dec_in) - 0.5
    log_decay = -jnp.exp(w_log)
    a = jax.nn.sigmoid(ibase_ref[...] + _dot(lora.astype(BF16), iup_ref[...]))
    g = _dot(jax.nn.sigmoid(lora).astype(BF16), gup_ref[...])

    kk = k * kks_ref[...]
    kk = kk / jnp.maximum(jnp.sqrt(_head_sums(kk * kk, ones_bd)), KK_EPS)
    k = k * (1.0 + (a - 1.0) * kas_ref[...])

    ro_ref[...] = r
    wo_ref[...] = log_decay
    ko_ref[...] = k
    vo_ref[...] = v
    ao_ref[...] = -kk
    bo_ref[...] = kk * a
    go_ref[...] = g
    bvo_ref[...] = _head_sums(r * k * bonus_ref[...], ones_bd) * v


def _ones_block_diag():
    i = lax.broadcasted_iota(jnp.int32, (LANES, LANES), 0) // B_HEAD_DIM
    j = lax.broadcasted_iota(jnp.int32, (LANES, LANES), 1) // B_HEAD_DIM
    return (i == j).astype(F32)


def _mixer_b_prep(proj, a_cols, bw, lora_col, lora_w, shift_mu, decay_up, decay_base, iclr_up,
                  iclr_base, gate_up, kk_scale, ka_scale, bonus, tm=256):
    t = proj.shape[0]
    tm = min(tm, t)
    cb = a_cols // bw
    lb = lora_col // lora_w
    n_lora = DECAY_LORA + ICLR_LORA + GATE_LORA

    def pad_rows(w, start):
        return jnp.zeros((lora_w, bw), F32).at[start:start + w.shape[0]].set(w).astype(BF16)

    dup = pad_rows(decay_up, 0)
    iup = pad_rows(iclr_up, DECAY_LORA)
    gup = pad_rows(gate_up, DECAY_LORA + ICLR_LORA)
    mu = shift_mu.reshape(1, -1)
    mu_l = jnp.zeros((1, lora_w), F32).at[:, :n_lora].set(mu[:, 3 * bw:])

    def cur(width, blk):
        return pl.BlockSpec((tm, width), lambda i: (i, blk))

    def prev(width, blk):
        return pl.BlockSpec((8, width), lambda i: (jnp.maximum(i * (tm // 8) - 1, 0), blk))

    vec = pl.BlockSpec((1, bw), lambda i: (0, 0))
    vecl = pl.BlockSpec((1, lora_w), lambda i: (0, 0))
    up = pl.BlockSpec((lora_w, bw), lambda i: (0, 0))
    out = pl.BlockSpec((tm, bw), lambda i: (i, 0))
    row = lambda x: x.reshape(1, -1)
    return pl.pallas_call(
        _mixer_b_prep_body,
        out_shape=[jax.ShapeDtypeStruct((t, bw), F32)] * 8,
        grid=(t // tm,),
        in_specs=[cur(bw, cb), cur(bw, cb + 1), cur(bw, cb + 2), cur(lora_w, lb),
                  prev(bw, cb), prev(bw, cb + 1), prev(bw, cb + 2), prev(lora_w, lb),
                  vec, vec, vec, vecl, up, up, up, vec, vec, vec, vec, vec,
                  pl.BlockSpec((LANES, LANES), lambda i: (0, 0))],
        out_specs=[out] * 8,
        compiler_params=_params(("arbitrary",)),
        name="mixer_b_prep",
    )(proj, proj, proj, proj, proj, proj, proj, proj,
      mu[:, :bw], mu[:, bw:2 * bw], mu[:, 2 * bw:3 * bw], mu_l, dup, iup, gup,
      row(decay_base), row(iclr_base), row(kk_scale), row(ka_scale), row(bonus), _ones_block_diag())


def _scan_body(r_ref, w_ref, k_ref, v_ref, a_ref, b_ref, y_ref, s_ref):
    @pl.when(pl.program_id(1) == 0)
    def _():
        s_ref[...] = jnp.zeros(s_ref.shape, F32)

    c = SCAN_CHUNK
    ti = lax.broadcasted_iota(jnp.int32, (c, c), 0)
    tj = lax.broadcasted_iota(jnp.int32, (c, c), 1)
    tri_incl = (tj <= ti).astype(BF16)
    head0 = lax.broadcasted_iota(jnp.int32, (c, LANES), 1) < B_HEAD_DIM
    i2 = lax.broadcasted_iota(jnp.int32, (2 * c, 2 * c), 0)
    j2 = lax.broadcasted_iota(jnp.int32, (2 * c, 2 * c), 1)
    eye = (i2 == j2).astype(F32)
    strict_bd = (i2 // c == j2 // c) & (j2 < i2)
    ic = lax.broadcasted_iota(jnp.int32, (c, 2 * c), 0)
    jc = lax.broadcasted_iota(jnp.int32, (c, 2 * c), 1)
    incl_cat = (jc % c) <= ic
    contract0 = (((0,), (0,)), ((), ()))
    contract1 = (((1,), (1,)), ((), ()))

    def stack(x):
        return jnp.concatenate([jnp.where(head0, x, 0.0), jnp.where(head0, 0.0, x)], axis=0)

    def cat(xs, axis):
        return jnp.concatenate(xs, axis=axis)

    pairs = range(r_ref.shape[1] // LANES)
    sls = [slice(p * LANES, (p + 1) * LANES) for p in pairs]
    w = [w_ref[:, sl] for sl in sls]
    cum3 = [_dot(tri_incl, cat(_split_bf16(w[p], 3), 1)) for p in pairs]
    cum = [x[:, :LANES] + x[:, LANES:2 * LANES] + x[:, 2 * LANES:] for x in cum3]
    cum_last = [x[c - 1:c, :] for x in cum]
    e_neg = [jnp.exp(-cum[p]) for p in pairs]
    e_tail = [jnp.exp(cum_last[p] - cum[p]) for p in pairs]
    r_t = [r_ref[:, sls[p]] * jnp.exp(cum[p]) for p in pairs]
    a_st = [stack(a_ref[:, sls[p]] * jnp.exp(cum[p] - w[p])) for p in pairs]
    g_c = [jnp.exp(x) for x in cum_last]
    b = [b_ref[:, sl] for sl in sls]
    k = [k_ref[:, sl] for sl in sls]
    v_bd = [stack(v_ref[:, sl]).astype(BF16) for sl in sls]
    lhs = [cat([a_st[p], r_t[p]], 0).astype(BF16) for p in pairs]
    rhs = [cat([stack(b[p] * e_neg[p]), stack(k[p] * e_neg[p])], 0).astype(BF16) for p in pairs]
    big = [lax.dot_general(lhs[p], rhs[p], contract1, preferred_element_type=F32) for p in pairs]
    a_ak = [jnp.where(strict_bd, x[:2 * c, 2 * c:], 0.0).astype(BF16) for x in big]
    a_rb = [jnp.where(incl_cat, x[2 * c:, :2 * c], 0.0).astype(BF16) for x in big]
    a_rk = [jnp.where(incl_cat, x[2 * c:, 2 * c:], 0.0).astype(BF16) for x in big]
    akv = [_dot(a_ak[p], v_bd[p]) for p in pairs]

    x = [jnp.where(strict_bd, y[:2 * c, :2 * c], 0.0) for y in big]
    t_inv = [eye + y for y in x]
    span = 2
    while span < c:
        xb = [y.astype(BF16) for y in x]
        x = [_dot(y, y) for y in xb]
        t_inv = [t_inv[p] + _dot(t_inv[p].astype(BF16), x[p].astype(BF16)) for p in pairs]
        span *= 2

    tub = [_dot(t_inv[p].astype(BF16), cat([a_st[p], akv[p]], 1).astype(BF16)).astype(BF16)
           for p in pairs]
    ry = [_dot(a_rb[p], tub[p]) for p in pairs]
    y0 = [ry[p][:, LANES:] + _dot(a_rk[p], v_bd[p]) for p in pairs]
    r_hat = [r_t[p] + ry[p][:, :LANES] for p in pairs]
    bg = [stack(b[p] * e_tail[p]).astype(BF16) for p in pairs]
    kg = [stack(k[p] * e_tail[p]).astype(BF16) for p in pairs]
    mn = [lax.dot_general(bg[p], tub[p], contract0, preferred_element_type=F32) for p in pairs]
    n = [mn[p][:, LANES:] + lax.dot_general(kg[p], v_bd[p], contract0, preferred_element_type=F32)
         for p in pairs]

    s0 = [s_ref[p] for p in pairs]
    fin = [_dot(cat([r_hat[p], mn[p][:, :LANES]], 0).astype(BF16), cat(_split_bf16(s0[p], 2), 1))
           for p in pairs]
    g_col = [jnp.sum(eye * g_c[p], axis=1, keepdims=True) for p in pairs]
    for p in pairs:
        y_ref[:, sls[p]] = fin[p][:c, :LANES] + fin[p][:c, LANES:] + y0[p]
        s_ref[p] = g_col[p] * s0[p] + (fin[p][c:, :LANES] + fin[p][c:, LANES:]) + n[p]


def _rwkv_scan(r, w, k, v, a, b, pairs_per_step=16):
    t, bw = r.shape
    pw = pairs_per_step * LANES
    blk = pl.BlockSpec((SCAN_CHUNK, pw), lambda g, c: (c, g))
    return pl.pallas_call(
        _scan_body,
        out_shape=jax.ShapeDtypeStruct((t, bw), F32),
        grid=(bw // pw, t // SCAN_CHUNK),
        in_specs=[blk] * 6,
        out_specs=blk,
        scratch_shapes=[pltpu.VMEM((pairs_per_step, LANES, LANES), F32)],
        compiler_params=_params(("arbitrary", "arbitrary")),
        name="rwkv_scan",
    )(r, w, k, v, a, b)


def _mixer_b_post_body(y_ref, bv_ref, g_ref, gng_ref, gnb_ref, ones_ref, o_ref):
    y = y_ref[...]
    ones_bd = ones_ref[...]
    inv_n = 1.0 / B_HEAD_DIM
    mu = _head_sums(y, ones_bd) * inv_n
    d = y - mu
    var = _head_sums(d * d, ones_bd) * inv_n
    yn = (d * lax.rsqrt(var + GN_EPS)) * gng_ref[...] + gnb_ref[...]
    o_ref[...] = ((yn + bv_ref[...]) * g_ref[...]).astype(o_ref.dtype)


def _mixer_b_post(y, bv, g, gn_g, gn_b, tm=256):
    t, bw = y.shape
    tm = min(tm, t)
    row = pl.BlockSpec((tm, bw), lambda i: (i, 0))
    vec = pl.BlockSpec((1, bw), lambda i: (0, 0))
    return pl.pallas_call(
        _mixer_b_post_body,
        out_shape=jax.ShapeDtypeStruct((t, bw), BF16),
        grid=(t // tm,),
        in_specs=[row, row, row, vec, vec, pl.BlockSpec((LANES, LANES), lambda i: (0, 0))],
        out_specs=row,
        compiler_params=_params(("arbitrary",)),
        name="mixer_b_post",
    )(y, bv, g, gn_g.reshape(1, -1), gn_b.reshape(1, -1), _ones_block_diag())


def _out_proj_body(ya_ref, yb_ref, w_ref, x_ref, g1_ref, o_ref, wb_ref):
    @pl.when(pl.program_id(1) == 0)
    def _():
        wb_ref[...] = w_ref[...].astype(wb_ref.dtype)

    half = ya_ref.shape[1]
    mix = _dot(ya_ref[...], wb_ref[:half, :]) + _dot(yb_ref[...], wb_ref[half:, :])
    o_ref[...] = x_ref[...] + g1_ref[...] * mix


def _out_projection(ya, yb, w_out, x, g1, tm=1024, tn=512):
    t, d = x.shape
    tm = min(tm, t)
    half = ya.shape[1]
    act = pl.BlockSpec((tm, half), lambda j, i: (i, 0))
    tile = pl.BlockSpec((tm, tn), lambda j, i: (i, j))
    return pl.pallas_call(
        _out_proj_body,
        out_shape=jax.ShapeDtypeStruct((t, d), F32),
        grid=(d // tn, t // tm),
        in_specs=[act, act, pl.BlockSpec((2 * half, tn), lambda j, i: (0, j)), tile,
                  pl.BlockSpec((1, tn), lambda j, i: (0, j))],
        out_specs=tile,
        scratch_shapes=[pltpu.VMEM((2 * half, tn), BF16)],
        compiler_params=_params(("arbitrary", "arbitrary")),
        name="out_projection",
    )(ya, yb, w_out, x, g1)


def _router_body(x_ref, g_ref, sh_ref, sc_ref, rw_ref, rb_ref,
                 te_ref, gate_ref, pos_ref, cnt_ref, carry_ref):
    @pl.when(pl.program_id(0) == 0)
    def _():
        carry_ref[...] = jnp.zeros(carry_ref.shape, F32)

    x = x_ref[...]
    h = _norm_modulate(x, g_ref[...], sh_ref[...], sc_ref[...])
    tm = x.shape[0]
    n_exp = rw_ref.shape[1]
    per_group = n_exp // N_GROUPS
    neg_inf = -jnp.inf

    h_hi, h_lo = _split_bf16(h, 2)
    w_hi, w_lo = _split_bf16(rw_ref[...], 2)
    scores = jax.nn.sigmoid(_dot(h_hi, w_hi) + (_dot(h_hi, w_lo) + _dot(h_lo, w_hi)))
    sel = scores + rb_ref[...]
    lane = lax.broadcasted_iota(jnp.int32, (tm, n_exp), 1)
    gid = lane // per_group

    gscore = []
    for gi in range(N_GROUPS):
        in_g = gid == gi
        m1 = jnp.max(jnp.where(in_g, sel, neg_inf), axis=-1, keepdims=True)
        i1 = jnp.min(jnp.where(in_g & (sel == m1), lane, n_exp), axis=-1, keepdims=True)
        m2 = jnp.max(jnp.where(in_g & (lane != i1), sel, neg_inf), axis=-1, keepdims=True)
        gscore.append(m1 + m2)
    allowed = jnp.zeros((tm, n_exp), jnp.bool_)
    for gi in range(N_GROUPS):
        beaten = jnp.zeros((tm, 1), jnp.int32)
        for gj in range(N_GROUPS):
            if gj == gi:
                continue
            wins = (gscore[gj] > gscore[gi]) | ((gscore[gj] == gscore[gi]) & (gj < gi))
            beaten = beaten + wins.astype(jnp.int32)
        allowed = allowed | ((gid == gi) & (beaten < TOPK_GROUPS))

    masked = jnp.where(allowed, sel, neg_inf)
    chosen = jnp.zeros((tm, n_exp), F32)
    idxs, gates = [], []
    for _ in range(TOP_K):
        m = jnp.max(masked, axis=-1, keepdims=True)
        idx = jnp.min(jnp.where(masked == m, lane, n_exp), axis=-1, keepdims=True)
        hit = lane == idx
        gates.append(jnp.sum(jnp.where(hit, scores, 0.0), axis=-1, keepdims=True))
        idxs.append(idx)
        chosen = jnp.where(hit, 1.0, chosen)
        masked = jnp.where(hit, neg_inf, masked)
    gsum = gates[0]
    for gk in gates[1:]:
        gsum = gsum + gk

    ri = lax.broadcasted_iota(jnp.int32, (tm, tm), 0)
    ci = lax.broadcasted_iota(jnp.int32, (tm, tm), 1)
    before = (ci < ri).astype(BF16)
    pos = carry_ref[...] + _dot(before, chosen.astype(BF16))
    carry_ref[...] = carry_ref[...] + jnp.sum(chosen, axis=0, keepdims=True)
    cnt_ref[...] = jnp.broadcast_to(carry_ref[...], cnt_ref.shape).astype(jnp.int32)

    te = jnp.zeros((tm, n_exp), jnp.int32)
    gt = jnp.zeros((tm, n_exp), F32)
    ps = jnp.zeros((tm, n_exp), jnp.int32)
    for kk in range(TOP_K):
        slot = lane == kk
        hit = lane == idxs[kk]
        pk = jnp.sum(jnp.where(hit, pos, 0.0), axis=-1, keepdims=True)
        te = jnp.where(slot, idxs[kk], te)
        gt = jnp.where(slot, gates[kk] / gsum * ROUTED_SCALE, gt)
        ps = jnp.where(slot, pk.astype(jnp.int32), ps)
    te_ref[...] = te
    gate_ref[...] = gt
    pos_ref[...] = ps


def _router(x1, norm_g, shift, scale, router_w, router_bias, tm=256):
    t, d = x1.shape
    tm = min(tm, t)
    n_exp = router_w.shape[1]
    row = pl.BlockSpec((tm, d), lambda i: (i, 0))
    vec = pl.BlockSpec((1, d), lambda i: (0, 0))
    small = pl.BlockSpec((tm, n_exp), lambda i: (i, 0))
    return pl.pallas_call(
        _router_body,
        out_shape=[jax.ShapeDtypeStruct((t, n_exp), jnp.int32), jax.ShapeDtypeStruct((t, n_exp), F32),
                   jax.ShapeDtypeStruct((t, n_exp), jnp.int32), jax.ShapeDtypeStruct((8, n_exp), jnp.int32)],
        grid=(t // tm,),
        in_specs=[row, vec, vec, vec,
                  pl.BlockSpec((d, n_exp), lambda i: (0, 0)), pl.BlockSpec((1, n_exp), lambda i: (0, 0))],
        out_specs=[small, small, small, pl.BlockSpec((8, n_exp), lambda i: (0, 0))],
        scratch_shapes=[pltpu.VMEM((1, n_exp), F32)],
        compiler_params=_params(("arbitrary",)),
        name="moe_router",
    )(x1, norm_g, shift, scale, router_w, router_bias.reshape(1, -1))


def _dest_body(starts_ref, te_ref, pos_ref, o_ref):
    te = te_ref[...]
    acc = jnp.zeros(te.shape, jnp.int32)
    for e in range(starts_ref.shape[0]):
        acc = jnp.where(te == e, starts_ref[e], acc)
    o_ref[...] = acc + pos_ref[...]


def _dest_rows(starts, top_e, pos_sel, tm=512):
    t, n_exp = top_e.shape
    tm = min(tm, t)
    blk = pl.BlockSpec((tm, n_exp), lambda i: (i, 0))
    return pl.pallas_call(
        _dest_body,
        out_shape=jax.ShapeDtypeStruct((t, n_exp), jnp.int32),
        grid=(t // tm,),
        in_specs=[pl.BlockSpec(memory_space=pltpu.SMEM), blk, blk],
        out_specs=blk,
        compiler_params=_params(("arbitrary",)),
        name="moe_dest",
    )(starts, top_e, pos_sel)


def _pack_halves(x):
    half = x.shape[1] // 2
    lo = pltpu.bitcast(x[:, :half].astype(BF16).astype(F32), jnp.uint32)
    hi = pltpu.bitcast(x[:, half:].astype(BF16).astype(F32), jnp.uint32)
    return (lo >> 16) | (hi & jnp.uint32(0xFFFF0000))


def _unpack_halves(p):
    lo = pltpu.bitcast(p << 16, F32)
    hi = pltpu.bitcast(p & jnp.uint32(0xFFFF0000), F32)
    return lo, hi


def _dispatch_body(end_ref, nb_ref, dst_ref, x_ref, g_ref, sh_ref, sc_ref, sgu_ref, sd_ref, g2_ref,
                   base_ref, xs_hbm, hp_buf, zero_buf, sem, zsem):
    i = pl.program_id(0)
    n = pl.num_programs(0)
    tm = x_ref.shape[0]
    rows = zero_buf.shape[0]
    n_exp = end_ref.shape[0]
    n_blocks = xs_hbm.shape[0] // rows

    def drain(slot):
        for _ in range(TOP_K):
            pltpu.make_async_copy(hp_buf.at[slot], xs_hbm.at[pl.ds(0, tm), :], sem.at[slot]).wait()

    @pl.when(i == 0)
    def _():
        zero_buf[...] = jnp.zeros(zero_buf.shape, zero_buf.dtype)
        nb = nb_ref[0]

        def zero_rows(start, count):
            first = start if count == 1 else pl.multiple_of(start, 8)
            return pltpu.make_async_copy(zero_buf.at[pl.ds(0, count), :], xs_hbm.at[pl.ds(first, count), :], zsem)

        def issue(start, stop, count):
            def body(j, carry):
                zero_rows(start + j * count, count).start()
                return carry
            n_copies = (stop - start) // count
            lax.fori_loop(0, n_copies, body, 0)
            return n_copies

        def fill(e, carry):
            n1, n8 = carry
            end = end_ref[e]
            aligned = (end + 7) // 8 * 8
            stop = (end + rows - 1) // rows * rows
            return n1 + issue(end, aligned, 1), n8 + issue(aligned, stop, 8)
        n1, n8 = lax.fori_loop(0, n_exp, fill, (0, 0))
        n_tail = issue(nb * rows, n_blocks * rows, rows)

        for count, n_copies in ((1, n1), (8, n8), (rows, n_tail)):
            def wait(j, carry, count=count):
                zero_rows(0, count).wait()
                return carry
            lax.fori_loop(0, n_copies, wait, 0)

    slot = i % 2

    @pl.when(i >= 2)
    def _():
        drain(slot)

    x = x_ref[...]
    h = _norm_modulate(x, g_ref[...], sh_ref[...], sc_ref[...])
    hp_buf[slot] = _pack_halves(h)

    def body(g, carry):
        t0 = pl.multiple_of(g * SUBLANES, SUBLANES)
        for j in range(SUBLANES):
            for kk in range(TOP_K):
                pltpu.make_async_copy(hp_buf.at[slot, pl.ds(t0 + j, 1), :],
                                      xs_hbm.at[pl.ds(dst_ref[0, 0, (t0 + j) * TOP_K + kk], 1), :],
                                      sem.at[slot]).start()
        return carry
    lax.fori_loop(0, tm // SUBLANES, body, 0)

    ds = sd_ref.shape[0]
    gu = _even_matmul(h.astype(BF16), sgu_ref)
    act = (jax.nn.silu(gu[:, :ds]) * gu[:, ds:]).astype(BF16)
    base_ref[...] = x + g2_ref[...] * _dot(act, sd_ref[...])

    @pl.when(i == n - 1)
    def _():
        drain(slot)

        @pl.when(n > 1)
        def _():
            drain(1 - slot)


def _dispatch(x1, norm_g, shift, scale, sh_w_gate, sh_w_up, sh_w_down, g2, seg_end, n_used, dest, n_rows, tm=256):
    t, d = x1.shape
    tm = min(tm, t)
    nt = t // tm
    ds = sh_w_gate.shape[1]
    sgu = jnp.concatenate([sh_w_gate, sh_w_up], axis=1).astype(BF16)
    dest3 = dest.reshape(nt, 1, tm * TOP_K)
    row = pl.BlockSpec((tm, d), lambda i, e, nb: (i, 0))
    vec = pl.BlockSpec((1, d), lambda i, e, nb: (0, 0))
    grid_spec = pltpu.PrefetchScalarGridSpec(
        num_scalar_prefetch=2,
        grid=(nt,),
        in_specs=[pl.BlockSpec((1, 1, tm * TOP_K), lambda i, e, nb: (i, 0, 0), memory_space=pltpu.SMEM),
                  row, vec, vec, vec,
                  pl.BlockSpec((d, 2 * ds), lambda i, e, nb: (0, 0)),
                  pl.BlockSpec((ds, d), lambda i, e, nb: (0, 0)), vec],
        out_specs=[row, pl.BlockSpec(memory_space=pl.ANY)],
        scratch_shapes=[pltpu.VMEM((2, tm, d // 2), jnp.uint32), pltpu.VMEM((MOE_ROWS, d // 2), jnp.uint32),
                        pltpu.SemaphoreType.DMA((2,)), pltpu.SemaphoreType.DMA(())],
    )
    return pl.pallas_call(
        _dispatch_body,
        out_shape=[jax.ShapeDtypeStruct((t, d), F32), jax.ShapeDtypeStruct((n_rows, d // 2), jnp.uint32)],
        grid_spec=grid_spec,
        compiler_params=_params(("arbitrary",)),
        name="moe_dispatch",
    )(seg_end, n_used, dest3, x1, norm_g, shift, scale, sgu, sh_w_down.astype(BF16), g2)


def _expert_body(be_ref, nb_ref, first_ref, next_ref, slot_ref, x_ref, wg_hbm, wu_hbm, wd_hbm, o_ref,
                 wgu_buf, wd_buf, sem):
    b = pl.program_id(0)
    nb = nb_ref[0]
    de = wd_buf.shape[1]

    def weight_copies(e, slot):
        return [pltpu.make_async_copy(wg_hbm.at[e], wgu_buf.at[slot, :, pl.ds(0, de)], sem.at[slot]),
                pltpu.make_async_copy(wu_hbm.at[e], wgu_buf.at[slot, :, pl.ds(de, de)], sem.at[slot]),
                pltpu.make_async_copy(wd_hbm.at[e], wd_buf.at[slot], sem.at[slot])]

    @pl.when((b == 0) & (nb > 0))
    def _():
        for cp in weight_copies(be_ref[0], 0):
            cp.start()

    @pl.when(b < nb)
    def _():
        slot = slot_ref[b]

        @pl.when(first_ref[b] == 1)
        def _():
            for cp in weight_copies(0, slot):
                cp.wait()

            @pl.when(next_ref[b] >= 0)
            def _():
                for cp in weight_copies(next_ref[b], 1 - slot):
                    cp.start()

        lo, hi = _unpack_halves(x_ref[...])
        x = jnp.concatenate([lo.astype(BF16), hi.astype(BF16)], axis=1)
        gu = _even_matmul(x, wgu_buf.at[slot])
        act = (jax.nn.silu(gu[:, :de]) * gu[:, de:]).astype(BF16)
        o_ref[...] = _pack_halves(_dot(act, wd_buf[slot].astype(BF16)))

    @pl.when(b >= nb)
    def _():
        o_ref[...] = jnp.zeros(o_ref.shape, o_ref.dtype)


def _routed_experts(x_sorted, blk_expert, n_used, blk_first, blk_next, blk_slot, w_gate, w_up, w_down):
    n_exp, d, de = w_gate.shape
    rows = MOE_ROWS
    nb = x_sorted.shape[0] // rows

    def last_used(b, be, nbr, *_):
        return (jnp.maximum(jnp.minimum(b, nbr[0] - 1), 0), 0)

    any_space = pl.BlockSpec(memory_space=pl.ANY)
    grid_spec = pltpu.PrefetchScalarGridSpec(
        num_scalar_prefetch=5,
        grid=(nb,),
        in_specs=[pl.BlockSpec((rows, d // 2), last_used), any_space, any_space, any_space],
        out_specs=pl.BlockSpec((rows, d // 2), lambda b, *_: (b, 0)),
        scratch_shapes=[pltpu.VMEM((2, d, 2 * de), F32), pltpu.VMEM((2, de, d), F32),
                        pltpu.SemaphoreType.DMA((2,))],
    )
    return pl.pallas_call(
        _expert_body,
        out_shape=jax.ShapeDtypeStruct((nb * rows, d // 2), jnp.uint32),
        grid_spec=grid_spec,
        compiler_params=_params(("arbitrary",)),
        name="routed_experts",
    )(blk_expert, n_used, blk_first, blk_next, blk_slot, x_sorted, w_gate, w_up, w_down)


def _combine_body(dst_ref, dstn_ref, y_hbm, gate_ref, base_ref, g2_ref, fg_ref, o_ref, ybuf, sem):
    i = pl.program_id(0)
    n = pl.num_programs(0)
    tm = base_ref.shape[0]

    def issue(dref, slot):
        def body(g, carry):
            t0 = pl.multiple_of(g * SUBLANES, SUBLANES)
            for j in range(SUBLANES):
                for kk in range(TOP_K):
                    pltpu.make_async_copy(y_hbm.at[pl.ds(dref[0, 0, (t0 + j) * TOP_K + kk], 1), :],
                                          ybuf.at[slot, kk, pl.ds(t0 + j, 1), :], sem.at[slot]).start()
            return carry
        lax.fori_loop(0, tm // SUBLANES, body, 0)

    @pl.when(i == 0)
    def _():
        issue(dst_ref, 0)

    @pl.when(i + 1 < n)
    def _():
        issue(dstn_ref, (i + 1) % 2)

    slot = i % 2
    for kk in range(TOP_K):
        pltpu.make_async_copy(y_hbm.at[pl.ds(0, tm), :], ybuf.at[slot, kk], sem.at[slot]).wait()
    half = ybuf.shape[-1]
    r_lo = r_hi = None
    for kk in range(TOP_K):
        lo, hi = _unpack_halves(ybuf[slot, kk])
        gk = gate_ref[:, kk:kk + 1]
        r_lo = gk * lo if r_lo is None else r_lo + gk * lo
        r_hi = gk * hi if r_hi is None else r_hi + gk * hi
    x_lo = base_ref[:, :half] + g2_ref[:, :half] * r_lo
    x_hi = base_ref[:, half:] + g2_ref[:, half:] * r_hi
    ssq = jnp.sum(x_lo * x_lo, axis=-1, keepdims=True) + jnp.sum(x_hi * x_hi, axis=-1, keepdims=True)
    inv = lax.rsqrt(ssq / (2 * half) + NORM_EPS)
    o_ref[:, :half] = x_lo * inv * fg_ref[:, :half]
    o_ref[:, half:] = x_hi * inv * fg_ref[:, half:]


def _combine(y_sorted, dest, gate, base, g2, final_g):
    t, d = base.shape
    tm = min(COMBINE_TOKENS, t)
    nt = t // tm
    dest3 = dest.reshape(nt, 1, tm * TOP_K)
    n_exp = gate.shape[1]
    vec = pl.BlockSpec((1, d), lambda i: (0, 0))
    return pl.pallas_call(
        _combine_body,
        out_shape=jax.ShapeDtypeStruct((t, d), F32),
        grid=(nt,),
        in_specs=[pl.BlockSpec((1, 1, tm * TOP_K), lambda i: (i, 0, 0), memory_space=pltpu.SMEM),
                  pl.BlockSpec((1, 1, tm * TOP_K), lambda i: (jnp.minimum(i + 1, nt - 1), 0, 0),
                               memory_space=pltpu.SMEM),
                  pl.BlockSpec(memory_space=pl.ANY),
                  pl.BlockSpec((tm, n_exp), lambda i: (i, 0)),
                  pl.BlockSpec((tm, d), lambda i: (i, 0)), vec, vec],
        out_specs=pl.BlockSpec((tm, d), lambda i: (i, 0)),
        scratch_shapes=[pltpu.VMEM((2, TOP_K, tm, d // 2), jnp.uint32), pltpu.SemaphoreType.DMA((2,))],
        compiler_params=_params(("arbitrary",)),
        name="moe_combine",
    )(dest3, dest3, y_sorted, gate, base, g2, final_g.reshape(1, -1))


def _segment_tables(counts, n_tok):
    n_exp = counts.shape[0]
    rows = MOE_ROWS
    padded = (counts + rows - 1) // rows * rows
    ends = jnp.cumsum(padded)
    starts = ends - padded
    nb = n_tok * TOP_K // rows + n_exp + 1
    first_row = jnp.arange(nb, dtype=jnp.int32) * rows
    blk_expert = jnp.sum((ends[None, :] <= first_row[:, None]).astype(jnp.int32), axis=1)
    blk_expert = jnp.minimum(blk_expert, n_exp - 1)
    n_used = ends[-1] // rows
    blk = jnp.arange(nb, dtype=jnp.int32)
    prev_expert = jnp.concatenate([jnp.full((1,), -1, jnp.int32), blk_expert[:-1]])
    blk_first = ((blk_expert != prev_expert) & (blk < n_used)).astype(jnp.int32)
    blk_slot = (jnp.cumsum(blk_first) - 1) % 2
    after = ends[blk_expert] // rows
    blk_next = jnp.where(after < n_used, blk_expert[jnp.minimum(after, nb - 1)], -1)
    tables = [t.astype(jnp.int32) for t in (blk_expert, n_used.reshape(1), blk_first, blk_next, blk_slot)]
    return starts.astype(jnp.int32), (starts + counts).astype(jnp.int32), tables, nb * rows


def _layer(x, c, mod_w, mod_b, norm1_g, norm2_g, w_in, w_out, a_ln_g, a_ln_b, a_spatial_w, a_spatial_b,
           b_shift_mu, b_decay_up, b_decay_base, b_iclr_up, b_iclr_base, b_gate_up, b_kk_scale,
           b_ka_scale, b_bonus, b_gn_g, b_gn_b, router_w, router_bias, exp_w_gate, exp_w_up,
           exp_w_down, sh_w_gate, sh_w_up, sh_w_down):
    t, d = x.shape
    a_width = a_ln_g.shape[0]
    bw = b_decay_base.shape[0]
    lora_w = 512
    lora_col = 2 * a_width + 3 * bw
    assert lora_col % lora_w == 0 and w_in.shape[1] - lora_col <= lora_w

    mod = _mod_vector(c, mod_w, mod_b)
    sh1, sc1, g1, sh2, sc2, g2 = [mod[:, i * d:(i + 1) * d] for i in range(6)]

    h1 = _normmod(x, norm1_g.reshape(1, d), sh1, sc1)
    proj = _in_projection(h1, w_in.T, tn=lora_w)
    ya = _mixer_a(proj, a_ln_g, a_ln_b, a_spatial_w, a_spatial_b, a_width)
    r, w, k, v, av, bv_, g, bonus_v = _mixer_b_prep(
        proj, 2 * a_width, bw, lora_col, lora_w, b_shift_mu, b_decay_up, b_decay_base, b_iclr_up,
        b_iclr_base, b_gate_up, b_kk_scale, b_ka_scale, b_bonus.reshape(-1))
    y = _rwkv_scan(r, w, k, v, av, bv_)
    yb = _mixer_b_post(y, bonus_v, g, b_gn_g, b_gn_b)
    x1 = _out_projection(ya, yb, w_out, x, g1)

    n2 = norm2_g.reshape(1, d)
    top_e, gate, pos_sel, counts = _router(x1, n2, sh2, sc2, router_w, router_bias)
    starts, seg_end, blk_tables, n_rows = _segment_tables(counts[0], t)
    dest = _dest_rows(starts, top_e, pos_sel)[:, :TOP_K]
    base, x_sorted = _dispatch(x1, n2, sh2, sc2, sh_w_gate, sh_w_up, sh_w_down, g2, seg_end, blk_tables[1],
                               dest, n_rows)
    y_sorted = _routed_experts(x_sorted, *blk_tables, exp_w_gate, exp_w_up, exp_w_down)
    return base, g2, y_sorted, dest, gate


def kernel(x, c, mod_w, mod_b, norm1_g, norm2_g, w_in, w_out, a_ln_g, a_ln_b, a_spatial_w, a_spatial_b, b_shift_mu, b_decay_up, b_decay_base, b_iclr_up, b_iclr_base, b_gate_up, b_kk_scale, b_ka_scale, b_bonus, b_gn_g, b_gn_b, router_w, router_bias, exp_w_gate, exp_w_up, exp_w_down, sh_w_gate, sh_w_up, sh_w_down, final_g):
    batch, seq, d = x.shape
    assert batch == 1 and mod_w.shape[0] == 1, "single sequence, single layer"
    layer = [p[0] for p in (mod_w, mod_b, norm1_g, norm2_g, w_in, w_out, a_ln_g, a_ln_b, a_spatial_w,
                            a_spatial_b, b_shift_mu, b_decay_up, b_decay_base, b_iclr_up, b_iclr_base,
                            b_gate_up, b_kk_scale, b_ka_scale, b_bonus, b_gn_g, b_gn_b, router_w,
                            router_bias, exp_w_gate, exp_w_up, exp_w_down, sh_w_gate, sh_w_up, sh_w_down)]
    base, g2, y_sorted, dest, gate = _layer(x[0], c, *layer)
    out = _combine(y_sorted, dest, gate, base, g2, final_g)
    return out.reshape(batch, seq, d)
```

```python
import functools

import jax
import jax.numpy as jnp
from jax import lax
from jax.experimental import pallas as pl
from jax.experimental.pallas import tpu as pltpu

A_GROUP_DIM = 128
B_HEAD_DIM = 64
DECAY_LORA = 96
ICLR_LORA = 96
GATE_LORA = 256
TOP_K = 8
N_GROUPS = 8
TOPK_GROUPS = 4
ROUTED_SCALE = 2.5
NORM_EPS = 1e-6
LN_EPS = 1e-5
GN_EPS = 64e-5
KK_EPS = 1e-12

LANES = 128
SUBLANES = 8
VMEM_LIMIT_BYTES = 58 * 1024 * 1024
SCAN_CHUNK = 64
MOE_ROWS = 256
COMBINE_TOKENS = 64

F32 = jnp.float32
BF16 = jnp.bfloat16


def _params(sem, vmem=VMEM_LIMIT_BYTES):
    return pltpu.CompilerParams(dimension_semantics=sem, vmem_limit_bytes=vmem)


def _dot(a, b):
    return jnp.dot(a, b, preferred_element_type=F32)


MXU_WIDTH = 256


def _even_matmul(x, w_ref):
    hm = x.shape[0] // 2
    tiles = w_ref.shape[1] // MXU_WIDTH
    rows = [jnp.concatenate([_dot(x[r * hm:(r + 1) * hm],
                                  w_ref[:, c * MXU_WIDTH:(c + 1) * MXU_WIDTH].astype(BF16))
                             for c in range(tiles)], axis=1) for r in range(2)]
    return jnp.concatenate(rows, axis=0)


def _mod_body(c_ref, w_ref, b_ref, o_ref, s_ref):
    @pl.when(pl.program_id(0) == 0)
    def _():
        c = c_ref[...]
        s_ref[...] = jnp.broadcast_to(c * jax.nn.sigmoid(c), s_ref.shape)

    s = s_ref[...]
    cols = [jnp.sum(w_ref[:, j * LANES:(j + 1) * LANES] * s, axis=0, keepdims=True)
            for j in range(w_ref.shape[1] // LANES)]
    o_ref[...] = jnp.concatenate(cols, axis=1) + b_ref[...]


def _mod_vector(c, mod_w, mod_b, tn=512):
    d, n = mod_w.shape
    return pl.pallas_call(
        _mod_body,
        out_shape=jax.ShapeDtypeStruct((1, n), F32),
        grid=(n // tn,),
        in_specs=[pl.BlockSpec((d, 1), lambda j: (0, 0)),
                  pl.BlockSpec((d, tn), lambda j: (0, j)),
                  pl.BlockSpec((1, tn), lambda j: (0, j))],
        out_specs=pl.BlockSpec((1, tn), lambda j: (0, j)),
        scratch_shapes=[pltpu.VMEM((d, LANES), F32)],
        compiler_params=_params(("arbitrary",)),
        name="mod_vector",
    )(c.reshape(d, 1), mod_w, mod_b.reshape(1, n))


def _norm_modulate(x, g, shift, scale):
    ms = jnp.mean(x * x, axis=-1, keepdims=True)
    y = x * lax.rsqrt(ms + NORM_EPS) * g
    return y * (1.0 + scale) + shift


def _normmod_body(x_ref, g_ref, sh_ref, sc_ref, o_ref):
    o_ref[...] = _norm_modulate(x_ref[...], g_ref[...], sh_ref[...], sc_ref[...]).astype(o_ref.dtype)


def _normmod(x, g, shift, scale, tm=256):
    t, d = x.shape
    tm = min(tm, t)
    row = pl.BlockSpec((tm, d), lambda i: (i, 0))
    vec = pl.BlockSpec((1, d), lambda i: (0, 0))
    return pl.pallas_call(
        _normmod_body,
        out_shape=jax.ShapeDtypeStruct((t, d), BF16),
        grid=(t // tm,),
        in_specs=[row, vec, vec, vec],
        out_specs=row,
        compiler_params=_params(("arbitrary",)),
        name="norm_modulate",
    )(x, g, shift, scale)


def _proj_body(h_ref, wt_ref, o_ref, wb_ref, *, n_valid):
    @pl.when(pl.program_id(1) == 0)
    def _():
        wt = wt_ref[...]
        col = pl.program_id(0) * wt.shape[0] + lax.broadcasted_iota(jnp.int32, wt.shape, 0)
        wb_ref[...] = jnp.where(col < n_valid, wt, 0.0).T.astype(wb_ref.dtype)

    o_ref[...] = _dot(h_ref[...], wb_ref[...])


def _in_projection(h, w_in_t, tm=1024, tn=512):
    t, d = h.shape
    n = w_in_t.shape[0]
    tm = min(tm, t)
    n_pad = pl.cdiv(n, tn) * tn
    return pl.pallas_call(
        functools.partial(_proj_body, n_valid=n),
        out_shape=jax.ShapeDtypeStruct((t, n_pad), F32),
        grid=(n_pad // tn, t // tm),
        in_specs=[pl.BlockSpec((tm, d), lambda j, i: (i, 0)),
                  pl.BlockSpec((tn, d), lambda j, i: (j, 0))],
        out_specs=pl.BlockSpec((tm, tn), lambda j, i: (i, j)),
        scratch_shapes=[pltpu.VMEM((d, tn), BF16)],
        compiler_params=_params(("arbitrary", "arbitrary")),
        name="in_projection",
    )(h, w_in_t)


def _gelu(x):
    return 0.5 * x * (1.0 + lax.erf(x * (2.0 ** -0.5)))


def _mixer_a_body(u_ref, v_ref, lng_ref, lnb_ref, ws_ref, sb_ref, o_ref):
    tm = u_ref.shape[0]
    gd = A_GROUP_DIM
    row = lax.broadcasted_iota(jnp.int32, (gd, gd), 0)
    col = lax.broadcasted_iota(jnp.int32, (gd, gd), 1)
    causal = col <= row
    for g in range(u_ref.shape[1] // gd):
        sl = slice(g * gd, (g + 1) * gd)
        u = _gelu(u_ref[:, sl])
        v = _gelu(v_ref[:, sl])
        mu = jnp.mean(v, axis=-1, keepdims=True)
        dv = v - mu
        var = jnp.mean(dv * dv, axis=-1, keepdims=True)
        vn = ((dv * lax.rsqrt(var + LN_EPS)) * lng_ref[:, sl] + lnb_ref[:, sl]).astype(BF16)
        w = jnp.where(causal, ws_ref[g], 0.0).astype(BF16)
        for n in range(tm // gd):
            rs = slice(n * gd, (n + 1) * gd)
            mixed = _dot(w, vn[rs]) + sb_ref[g]
            o_ref[rs, sl] = (u[rs] * mixed).astype(o_ref.dtype)


def _mixer_a(proj, ln_g, ln_b, spatial_w, spatial_b, a_width, tm=256):
    t = proj.shape[0]
    tm = min(tm, t)
    groups = a_width // A_GROUP_DIM
    sb = jnp.broadcast_to(spatial_b[:, :, None], (groups, A_GROUP_DIM, A_GROUP_DIM))
    vec = pl.BlockSpec((1, a_width), lambda i: (0, 0))
    mat = pl.BlockSpec((groups, A_GROUP_DIM, A_GROUP_DIM), lambda i: (0, 0, 0))
    return pl.pallas_call(
        _mixer_a_body,
        out_shape=jax.ShapeDtypeStruct((t, a_width), BF16),
        grid=(t // tm,),
        in_specs=[pl.BlockSpec((tm, a_width), lambda i: (i, 0)),
                  pl.BlockSpec((tm, a_width), lambda i: (i, 1)),
                  vec, vec, mat, mat],
        out_specs=pl.BlockSpec((tm, a_width), lambda i: (i, 0)),
        compiler_params=_params(("arbitrary",)),
        name="mixer_a",
    )(proj, proj, ln_g.reshape(1, -1), ln_b.reshape(1, -1), spatial_w, sb)


def _split_bf16(x, parts):
    out = []
    for _ in range(parts - 1):
        hi = x.astype(BF16)
        out.append(hi)
        x = x - hi.astype(F32)
    out.append(x.astype(BF16))
    return out


def _head_sums(x):
    head0 = lax.broadcasted_iota(jnp.int32, (x.shape[0], LANES), 1) < B_HEAD_DIM
    parts = []
    for s in range(x.shape[1] // LANES):
        xs = x[:, s * LANES:(s + 1) * LANES]
        s0 = jnp.sum(jnp.where(head0, xs, 0.0), axis=-1, keepdims=True)
        s1 = jnp.sum(jnp.where(head0, 0.0, xs), axis=-1, keepdims=True)
        parts.append(jnp.where(head0, s0, s1))
    return jnp.concatenate(parts, axis=1)


def _softplus(x):
    return jnp.maximum(x, 0.0) + jnp.log1p(jnp.exp(-jnp.abs(x)))


def _scan_chunk(r_in, w_in, k_in, v_in, a_in, b_in, s_ref):
    c = SCAN_CHUNK
    ti = lax.broadcasted_iota(jnp.int32, (c, c), 0)
    tj = lax.broadcasted_iota(jnp.int32, (c, c), 1)
    tri_incl = (tj <= ti).astype(BF16)
    head0 = lax.broadcasted_iota(jnp.int32, (c, LANES), 1) < B_HEAD_DIM
    i2 = lax.broadcasted_iota(jnp.int32, (2 * c, 2 * c), 0)
    j2 = lax.broadcasted_iota(jnp.int32, (2 * c, 2 * c), 1)
    eye = (i2 == j2).astype(F32)
    strict_bd = (i2 // c == j2 // c) & (j2 < i2)
    ic = lax.broadcasted_iota(jnp.int32, (c, 2 * c), 0)
    jc = lax.broadcasted_iota(jnp.int32, (c, 2 * c), 1)
    incl_cat = (jc % c) <= ic
    contract0 = (((0,), (0,)), ((), ()))
    contract1 = (((1,), (1,)), ((), ()))

    def stack(x):
        return jnp.concatenate([jnp.where(head0, x, 0.0), jnp.where(head0, 0.0, x)], axis=0)

    def cat(xs, axis):
        return jnp.concatenate(xs, axis=axis)

    pairs = range(r_in.shape[1] // LANES)
    sls = [slice(p * LANES, (p + 1) * LANES) for p in pairs]
    w = [w_in[:, sl] for sl in sls]
    cum3 = [_dot(tri_incl, cat(_split_bf16(w[p], 3), 1)) for p in pairs]
    cum = [x[:, :LANES] + x[:, LANES:2 * LANES] + x[:, 2 * LANES:] for x in cum3]
    cum_last = [x[c - 1:c, :] for x in cum]
    e_neg = [jnp.exp(-cum[p]) for p in pairs]
    e_tail = [jnp.exp(cum_last[p] - cum[p]) for p in pairs]
    r_t = [r_in[:, sls[p]] * jnp.exp(cum[p]) for p in pairs]
    a_st = [stack(a_in[:, sls[p]] * jnp.exp(cum[p] - w[p])) for p in pairs]
    g_c = [jnp.exp(x) for x in cum_last]
    b = [b_in[:, sl] for sl in sls]
    k = [k_in[:, sl] for sl in sls]
    v_bd = [stack(v_in[:, sl]).astype(BF16) for sl in sls]
    lhs = [cat([a_st[p], r_t[p]], 0).astype(BF16) for p in pairs]
    rhs = [cat([stack(b[p] * e_neg[p]), stack(k[p] * e_neg[p])], 0).astype(BF16) for p in pairs]
    big = [lax.dot_general(lhs[p], rhs[p], contract1, preferred_element_type=F32) for p in pairs]
    a_ak = [jnp.where(strict_bd, x[:2 * c, 2 * c:], 0.0).astype(BF16) for x in big]
    a_rb = [jnp.where(incl_cat, x[2 * c:, :2 * c], 0.0).astype(BF16) for x in big]
    a_rk = [jnp.where(incl_cat, x[2 * c:, 2 * c:], 0.0).astype(BF16) for x in big]
    akv = [_dot(a_ak[p], v_bd[p]) for p in pairs]

    x = [jnp.where(strict_bd, y[:2 * c, :2 * c], 0.0) for y in big]
    t_inv = [eye + y for y in x]
    span = 2
    while span < c:
        xb = [y.astype(BF16) for y in x]
        x = [_dot(y, y) for y in xb]
        t_inv = [t_inv[p] + _dot(t_inv[p].astype(BF16), x[p].astype(BF16)) for p in pairs]
        span *= 2

    tub = [_dot(t_inv[p].astype(BF16), cat([a_st[p], akv[p]], 1).astype(BF16)).astype(BF16)
           for p in pairs]
    ry = [_dot(a_rb[p], tub[p]) for p in pairs]
    y0 = [ry[p][:, LANES:] + _dot(a_rk[p], v_bd[p]) for p in pairs]
    r_hat = [r_t[p] + ry[p][:, :LANES] for p in pairs]
    bg = [stack(b[p] * e_tail[p]).astype(BF16) for p in pairs]
    kg = [stack(k[p] * e_tail[p]).astype(BF16) for p in pairs]
    mn = [lax.dot_general(bg[p], tub[p], contract0, preferred_element_type=F32) for p in pairs]
    n = [mn[p][:, LANES:] + lax.dot_general(kg[p], v_bd[p], contract0, preferred_element_type=F32)
         for p in pairs]

    s0 = [s_ref[p] for p in pairs]
    fin = [_dot(cat([r_hat[p], mn[p][:, :LANES]], 0).astype(BF16), cat(_split_bf16(s0[p], 2), 1))
           for p in pairs]
    g_col = [jnp.sum(eye * g_c[p], axis=1, keepdims=True) for p in pairs]
    for p in pairs:
        s_ref[p] = g_col[p] * s0[p] + (fin[p][c:, :LANES] + fin[p][c:, LANES:]) + n[p]
    return cat([fin[p][:c, :LANES] + fin[p][:c, LANES:] + y0[p] for p in pairs], 1)


def _mixer_b_body(r_ref, k_ref, v_ref, l_ref, mur_ref, muk_ref, muv_ref, mul_ref, dup_ref, iup_ref, gup_ref,
                  dbase_ref, ibase_ref, kks_ref, kas_ref, bonus_ref, gng_ref, gnb_ref,
                  o_ref, s_ref, pr_ref, pk_ref, pv_ref, pl_ref):
    @pl.when(pl.program_id(0) == 0)
    def _():
        for ref in (s_ref, pr_ref, pk_ref, pv_ref, pl_ref):
            ref[...] = jnp.zeros(ref.shape, F32)

    def shifted(p_ref, prev_ref, mu_ref):
        p = p_ref[...]
        rowid = lax.broadcasted_iota(jnp.int32, p.shape, 0)
        prev = jnp.where(rowid == 0, prev_ref[...], pltpu.roll(p, 1, 0))
        prev_ref[...] = p[p.shape[0] - 1:, :]
        return p + mu_ref[...] * (prev - p)

    r = shifted(r_ref, pr_ref, mur_ref)
    k = shifted(k_ref, pk_ref, muk_ref)
    v = shifted(v_ref, pv_ref, muv_ref)
    lora = shifted(l_ref, pl_ref, mul_ref)

    dec_in = dbase_ref[...] + _dot(jnp.tanh(lora).astype(BF16), dup_ref[...])
    w_log = -_softplus(-dec_in) - 0.5
    log_decay = -jnp.exp(w_log)
    a = jax.nn.sigmoid(ibase_ref[...] + _dot(lora.astype(BF16), iup_ref[...]))
    g = _dot(jax.nn.sigmoid(lora).astype(BF16), gup_ref[...])

    kk = k * kks_ref[...]
    kk = kk / jnp.maximum(jnp.sqrt(_head_sums(kk * kk)), KK_EPS)
    k = k * (1.0 + (a - 1.0) * kas_ref[...])
    bonus_v = _head_sums(r * k * bonus_ref[...]) * v

    y = _scan_chunk(r, log_decay, k, v, -kk, kk * a, s_ref)

    inv_n = 1.0 / B_HEAD_DIM
    mu = _head_sums(y) * inv_n
    d = y - mu
    var = _head_sums(d * d) * inv_n
    yn = (d * lax.rsqrt(var + GN_EPS)) * gng_ref[...] + gnb_ref[...]
    o_ref[...] = ((yn + bonus_v) * g).astype(o_ref.dtype)


def _mixer_b(proj, a_cols, bw, lora_col, lora_w, shift_mu, decay_up, decay_base, iclr_up, iclr_base,
             gate_up, kk_scale, ka_scale, bonus, gn_g, gn_b):
    t = proj.shape[0]
    c = SCAN_CHUNK
    cb = a_cols // bw
    lb = lora_col // lora_w
    n_lora = DECAY_LORA + ICLR_LORA + GATE_LORA

    def pad_rows(w, start):
        return jnp.zeros((lora_w, bw), F32).at[start:start + w.shape[0]].set(w).astype(BF16)

    dup = pad_rows(decay_up, 0)
    iup = pad_rows(iclr_up, DECAY_LORA)
    gup = pad_rows(gate_up, DECAY_LORA + ICLR_LORA)
    mu = shift_mu.reshape(1, -1)
    mu_l = jnp.zeros((1, lora_w), F32).at[:, :n_lora].set(mu[:, 3 * bw:])

    def cur(width, blk):
        return pl.BlockSpec((c, width), lambda i: (i, blk))

    vec = pl.BlockSpec((1, bw), lambda i: (0, 0))
    vecl = pl.BlockSpec((1, lora_w), lambda i: (0, 0))
    up = pl.BlockSpec((lora_w, bw), lambda i: (0, 0))
    row = lambda x: x.reshape(1, -1)
    return pl.pallas_call(
        _mixer_b_body,
        out_shape=jax.ShapeDtypeStruct((t, bw), BF16),
        grid=(t // c,),
        in_specs=[cur(bw, cb), cur(bw, cb + 1), cur(bw, cb + 2), cur(lora_w, lb),
                  vec, vec, vec, vecl, up, up, up, vec, vec, vec, vec, vec, vec, vec],
        out_specs=pl.BlockSpec((c, bw), lambda i: (i, 0)),
        scratch_shapes=[pltpu.VMEM((bw // LANES, LANES, LANES), F32),
                        pltpu.VMEM((1, bw), F32), pltpu.VMEM((1, bw), F32), pltpu.VMEM((1, bw), F32),
                        pltpu.VMEM((1, lora_w), F32)],
        compiler_params=_params(("arbitrary",)),
        name="mixer_b",
    )(proj, proj, proj, proj, mu[:, :bw], mu[:, bw:2 * bw], mu[:, 2 * bw:3 * bw], mu_l, dup, iup, gup,
      row(decay_base), row(iclr_base), row(kk_scale), row(ka_scale), row(bonus), row(gn_g), row(gn_b))


def _out_proj_body(ya_ref, yb_ref, w_ref, x_ref, g1_ref, o_ref, wb_ref):
    @pl.when(pl.program_id(1) == 0)
    def _():
        wb_ref[...] = w_ref[...].astype(wb_ref.dtype)

    half = ya_ref.shape[1]
    mix = _dot(ya_ref[...], wb_ref[:half, :]) + _dot(yb_ref[...], wb_ref[half:, :])
    o_ref[...] = x_ref[...] + g1_ref[...] * mix


def _out_projection(ya, yb, w_out, x, g1, tm=1024, tn=512):
    t, d = x.shape
    tm = min(tm, t)
    half = ya.shape[1]
    act = pl.BlockSpec((tm, half), lambda j, i: (i, 0))
    tile = pl.BlockSpec((tm, tn), lambda j, i: (i, j))
    return pl.pallas_call(
        _out_proj_body,
        out_shape=jax.ShapeDtypeStruct((t, d), F32),
        grid=(d // tn, t // tm),
        in_specs=[act, act, pl.BlockSpec((2 * half, tn), lambda j, i: (0, j)), tile,
                  pl.BlockSpec((1, tn), lambda j, i: (0, j))],
        out_specs=tile,
        scratch_shapes=[pltpu.VMEM((2 * half, tn), BF16)],
        compiler_params=_params(("arbitrary", "arbitrary")),
        name="out_projection",
    )(ya, yb, w_out, x, g1)


def _router_body(x_ref, g_ref, sh_ref, sc_ref, rw_ref, rb_ref,
                 te_ref, gate_ref, pos_ref, cnt_ref, carry_ref):
    @pl.when(pl.program_id(0) == 0)
    def _():
        carry_ref[...] = jnp.zeros(carry_ref.shape, F32)

    x = x_ref[...]
    h = _norm_modulate(x, g_ref[...], sh_ref[...], sc_ref[...])
    tm = x.shape[0]
    n_exp = rw_ref.shape[1]
    per_group = n_exp // N_GROUPS
    neg_inf = -jnp.inf

    h_hi, h_lo = _split_bf16(h, 2)
    w_hi, w_lo = _split_bf16(rw_ref[...], 2)
    scores = jax.nn.sigmoid(_dot(h_hi, w_hi) + (_dot(h_hi, w_lo) + _dot(h_lo, w_hi)))
    sel = scores + rb_ref[...]
    lane = lax.broadcasted_iota(jnp.int32, (tm, n_exp), 1)
    gid = lane // per_group

    gscore = []
    for gi in range(N_GROUPS):
        in_g = gid == gi
        m1 = jnp.max(jnp.where(in_g, sel, neg_inf), axis=-1, keepdims=True)
        i1 = jnp.min(jnp.where(in_g & (sel == m1), lane, n_exp), axis=-1, keepdims=True)
        m2 = jnp.max(jnp.where(in_g & (lane != i1), sel, neg_inf), axis=-1, keepdims=True)
        gscore.append(m1 + m2)
    allowed = jnp.zeros((tm, n_exp), jnp.bool_)
    for gi in range(N_GROUPS):
        beaten = jnp.zeros((tm, 1), jnp.int32)
        for gj in range(N_GROUPS):
            if gj == gi:
                continue
            wins = (gscore[gj] > gscore[gi]) | ((gscore[gj] == gscore[gi]) & (gj < gi))
            beaten = beaten + wins.astype(jnp.int32)
        allowed = allowed | ((gid == gi) & (beaten < TOPK_GROUPS))

    masked = jnp.where(allowed, sel, neg_inf)
    chosen = jnp.zeros((tm, n_exp), F32)
    idxs, gates = [], []
    for _ in range(TOP_K):
        m = jnp.max(masked, axis=-1, keepdims=True)
        idx = jnp.min(jnp.where(masked == m, lane, n_exp), axis=-1, keepdims=True)
        hit = lane == idx
        gates.append(jnp.sum(jnp.where(hit, scores, 0.0), axis=-1, keepdims=True))
        idxs.append(idx)
        chosen = jnp.where(hit, 1.0, chosen)
        masked = jnp.where(hit, neg_inf, masked)
    gsum = gates[0]
    for gk in gates[1:]:
        gsum = gsum + gk

    ri = lax.broadcasted_iota(jnp.int32, (tm, tm), 0)
    ci = lax.broadcasted_iota(jnp.int32, (tm, tm), 1)
    before = (ci < ri).astype(BF16)
    pos = carry_ref[...] + _dot(before, chosen.astype(BF16))
    carry_ref[...] = carry_ref[...] + jnp.sum(chosen, axis=0, keepdims=True)
    cnt_ref[...] = jnp.broadcast_to(carry_ref[...], cnt_ref.shape).astype(jnp.int32)

    te = jnp.zeros((tm, n_exp), jnp.int32)
    gt = jnp.zeros((tm, n_exp), F32)
    ps = jnp.zeros((tm, n_exp), jnp.int32)
    for kk in range(TOP_K):
        slot = lane == kk
        hit = lane == idxs[kk]
        pk = jnp.sum(jnp.where(hit, pos, 0.0), axis=-1, keepdims=True)
        te = jnp.where(slot, idxs[kk], te)
        gt = jnp.where(slot, gates[kk] / gsum * ROUTED_SCALE, gt)
        ps = jnp.where(slot, pk.astype(jnp.int32), ps)
    te_ref[...] = te
    gate_ref[...] = gt
    pos_ref[...] = ps


def _router(x1, norm_g, shift, scale, router_w, router_bias, tm=256):
    t, d = x1.shape
    tm = min(tm, t)
    n_exp = router_w.shape[1]
    row = pl.BlockSpec((tm, d), lambda i: (i, 0))
    vec = pl.BlockSpec((1, d), lambda i: (0, 0))
    small = pl.BlockSpec((tm, n_exp), lambda i: (i, 0))
    return pl.pallas_call(
        _router_body,
        out_shape=[jax.ShapeDtypeStruct((t, n_exp), jnp.int32), jax.ShapeDtypeStruct((t, n_exp), F32),
                   jax.ShapeDtypeStruct((t, n_exp), jnp.int32), jax.ShapeDtypeStruct((8, n_exp), jnp.int32)],
        grid=(t // tm,),
        in_specs=[row, vec, vec, vec,
                  pl.BlockSpec((d, n_exp), lambda i: (0, 0)), pl.BlockSpec((1, n_exp), lambda i: (0, 0))],
        out_specs=[small, small, small, pl.BlockSpec((8, n_exp), lambda i: (0, 0))],
        scratch_shapes=[pltpu.VMEM((1, n_exp), F32)],
        compiler_params=_params(("arbitrary",)),
        name="moe_router",
    )(x1, norm_g, shift, scale, router_w, router_bias.reshape(1, -1))


def _dest_body(starts_ref, te_ref, pos_ref, o_ref):
    te = te_ref[...]
    acc = jnp.zeros(te.shape, jnp.int32)
    for e in range(starts_ref.shape[0]):
        acc = jnp.where(te == e, starts_ref[e], acc)
    o_ref[...] = acc + pos_ref[...]


def _dest_rows(starts, top_e, pos_sel, tm=512):
    t, n_exp = top_e.shape
    tm = min(tm, t)
    blk = pl.BlockSpec((tm, n_exp), lambda i: (i, 0))
    return pl.pallas_call(
        _dest_body,
        out_shape=jax.ShapeDtypeStruct((t, n_exp), jnp.int32),
        grid=(t // tm,),
        in_specs=[pl.BlockSpec(memory_space=pltpu.SMEM), blk, blk],
        out_specs=blk,
        compiler_params=_params(("arbitrary",)),
        name="moe_dest",
    )(starts, top_e, pos_sel)


def _pack_halves(x):
    half = x.shape[1] // 2
    lo = pltpu.bitcast(x[:, :half].astype(BF16).astype(F32), jnp.uint32)
    hi = pltpu.bitcast(x[:, half:].astype(BF16).astype(F32), jnp.uint32)
    return (lo >> 16) | (hi & jnp.uint32(0xFFFF0000))


def _unpack_halves(p):
    lo = pltpu.bitcast(p << 16, F32)
    hi = pltpu.bitcast(p & jnp.uint32(0xFFFF0000), F32)
    return lo, hi


def _dispatch_body(end_ref, nb_ref, dst_ref, x_ref, g_ref, sh_ref, sc_ref, sgu_ref, sd_ref, g2_ref,
                   base_ref, xs_hbm, hp_buf, zero_buf, sem, zsem):
    i = pl.program_id(0)
    n = pl.num_programs(0)
    tm = x_ref.shape[0]
    rows = zero_buf.shape[0]
    n_exp = end_ref.shape[0]
    n_blocks = xs_hbm.shape[0] // rows

    def drain(slot):
        for _ in range(TOP_K):
            pltpu.make_async_copy(hp_buf.at[slot], xs_hbm.at[pl.ds(0, tm), :], sem.at[slot]).wait()

    @pl.when(i == 0)
    def _():
        zero_buf[...] = jnp.zeros(zero_buf.shape, zero_buf.dtype)
        nb = nb_ref[0]

        def zero_rows(start, count):
            first = start if count == 1 else pl.multiple_of(start, 8)
            return pltpu.make_async_copy(zero_buf.at[pl.ds(0, count), :], xs_hbm.at[pl.ds(first, count), :], zsem)

        def issue(start, stop, count):
            def body(j, carry):
                zero_rows(start + j * count, count).start()
                return carry
            n_copies = (stop - start) // count
            lax.fori_loop(0, n_copies, body, 0)
            return n_copies

        def fill(e, carry):
            n1, n8 = carry
            end = end_ref[e]
            aligned = (end + 7) // 8 * 8
            stop = (end + rows - 1) // rows * rows
            return n1 + issue(end, aligned, 1), n8 + issue(aligned, stop, 8)
        n1, n8 = lax.fori_loop(0, n_exp, fill, (0, 0))
        n_tail = issue(nb * rows, n_blocks * rows, rows)

        for count, n_copies in ((1, n1), (8, n8), (rows, n_tail)):
            def wait(j, carry, count=count):
                zero_rows(0, count).wait()
                return carry
            lax.fori_loop(0, n_copies, wait, 0)

    slot = i % 2

    @pl.when(i >= 2)
    def _():
        drain(slot)

    x = x_ref[...]
    h = _norm_modulate(x, g_ref[...], sh_ref[...], sc_ref[...])
    hp_buf[slot] = _pack_halves(h)

    def body(g, carry):
        t0 = pl.multiple_of(g * SUBLANES, SUBLANES)
        for j in range(SUBLANES):
            for kk in range(TOP_K):
                pltpu.make_async_copy(hp_buf.at[slot, pl.ds(t0 + j, 1), :],
                                      xs_hbm.at[pl.ds(dst_ref[0, 0, (t0 + j) * TOP_K + kk], 1), :],
                                      sem.at[slot]).start()
        return carry
    lax.fori_loop(0, tm // SUBLANES, body, 0)

    ds = sd_ref.shape[0]
    gu = _even_matmul(h.astype(BF16), sgu_ref)
    act = (jax.nn.silu(gu[:, :ds]) * gu[:, ds:]).astype(BF16)
    base_ref[...] = x + g2_ref[...] * _dot(act, sd_ref[...])

    @pl.when(i == n - 1)
    def _():
        drain(slot)

        @pl.when(n > 1)
        def _():
            drain(1 - slot)


def _dispatch(x1, norm_g, shift, scale, sh_w_gate, sh_w_up, sh_w_down, g2, seg_end, n_used, dest, n_rows, tm=256):
    t, d = x1.shape
    tm = min(tm, t)
    nt = t // tm
    ds = sh_w_gate.shape[1]
    sgu = jnp.concatenate([sh_w_gate, sh_w_up], axis=1).astype(BF16)
    dest3 = dest.reshape(nt, 1, tm * TOP_K)
    row = pl.BlockSpec((tm, d), lambda i, e, nb: (i, 0))
    vec = pl.BlockSpec((1, d), lambda i, e, nb: (0, 0))
    grid_spec = pltpu.PrefetchScalarGridSpec(
        num_scalar_prefetch=2,
        grid=(nt,),
        in_specs=[pl.BlockSpec((1, 1, tm * TOP_K), lambda i, e, nb: (i, 0, 0), memory_space=pltpu.SMEM),
                  row, vec, vec, vec,
                  pl.BlockSpec((d, 2 * ds), lambda i, e, nb: (0, 0)),
                  pl.BlockSpec((ds, d), lambda i, e, nb: (0, 0)), vec],
        out_specs=[row, pl.BlockSpec(memory_space=pl.ANY)],
        scratch_shapes=[pltpu.VMEM((2, tm, d // 2), jnp.uint32), pltpu.VMEM((MOE_ROWS, d // 2), jnp.uint32),
                        pltpu.SemaphoreType.DMA((2,)), pltpu.SemaphoreType.DMA(())],
    )
    return pl.pallas_call(
        _dispatch_body,
        out_shape=[jax.ShapeDtypeStruct((t, d), F32), jax.ShapeDtypeStruct((n_rows, d // 2), jnp.uint32)],
        grid_spec=grid_spec,
        compiler_params=_params(("arbitrary",)),
        name="moe_dispatch",
    )(seg_end, n_used, dest3, x1, norm_g, shift, scale, sgu, sh_w_down.astype(BF16), g2)


def _expert_body(be_ref, nb_ref, first_ref, next_ref, slot_ref, x_ref, wg_hbm, wu_hbm, wd_hbm, o_ref,
                 wgu_buf, wd_buf, sem):
    b = pl.program_id(0)
    nb = nb_ref[0]
    de = wd_buf.shape[1]

    def weight_copies(e, slot):
        return [pltpu.make_async_copy(wg_hbm.at[e], wgu_buf.at[slot, :, pl.ds(0, de)], sem.at[slot]),
                pltpu.make_async_copy(wu_hbm.at[e], wgu_buf.at[slot, :, pl.ds(de, de)], sem.at[slot]),
                pltpu.make_async_copy(wd_hbm.at[e], wd_buf.at[slot], sem.at[slot])]

    @pl.when((b == 0) & (nb > 0))
    def _():
        for cp in weight_copies(be_ref[0], 0):
            cp.start()

    @pl.when(b < nb)
    def _():
        slot = slot_ref[b]

        @pl.when(first_ref[b] == 1)
        def _():
            for cp in weight_copies(0, slot):
                cp.wait()

            @pl.when(next_ref[b] >= 0)
            def _():
                for cp in weight_copies(next_ref[b], 1 - slot):
                    cp.start()

        lo, hi = _unpack_halves(x_ref[...])
        x = jnp.concatenate([lo.astype(BF16), hi.astype(BF16)], axis=1)
        gu = _even_matmul(x, wgu_buf.at[slot])
        act = (jax.nn.silu(gu[:, :de]) * gu[:, de:]).astype(BF16)
        o_ref[...] = _pack_halves(_dot(act, wd_buf[slot].astype(BF16)))

    @pl.when(b >= nb)
    def _():
        o_ref[...] = jnp.zeros(o_ref.shape, o_ref.dtype)


def _routed_experts(x_sorted, blk_expert, n_used, blk_first, blk_next, blk_slot, w_gate, w_up, w_down):
    n_exp, d, de = w_gate.shape
    rows = MOE_ROWS
    nb = x_sorted.shape[0] // rows

    def last_used(b, be, nbr, *_):
        return (jnp.maximum(jnp.minimum(b, nbr[0] - 1), 0), 0)

    any_space = pl.BlockSpec(memory_space=pl.ANY)
    grid_spec = pltpu.PrefetchScalarGridSpec(
        num_scalar_prefetch=5,
        grid=(nb,),
        in_specs=[pl.BlockSpec((rows, d // 2), last_used), any_space, any_space, any_space],
        out_specs=pl.BlockSpec((rows, d // 2), lambda b, *_: (b, 0)),
        scratch_shapes=[pltpu.VMEM((2, d, 2 * de), F32), pltpu.VMEM((2, de, d), F32),
                        pltpu.SemaphoreType.DMA((2,))],
    )
    return pl.pallas_call(
        _expert_body,
        out_shape=jax.ShapeDtypeStruct((nb * rows, d // 2), jnp.uint32),
        grid_spec=grid_spec,
        compiler_params=_params(("arbitrary",)),
        name="routed_experts",
    )(blk_expert, n_used, blk_first, blk_next, blk_slot, x_sorted, w_gate, w_up, w_down)


def _combine_body(dst_ref, dstn_ref, y_hbm, gate_ref, base_ref, g2_ref, fg_ref, o_ref, ybuf, sem):
    i = pl.program_id(0)
    n = pl.num_programs(0)
    tm = base_ref.shape[0]

    def issue(dref, slot):
        def body(g, carry):
            t0 = pl.multiple_of(g * SUBLANES, SUBLANES)
            for j in range(SUBLANES):
                for kk in range(TOP_K):
                    pltpu.make_async_copy(y_hbm.at[pl.ds(dref[0, 0, (t0 + j) * TOP_K + kk], 1), :],
                                          ybuf.at[slot, kk, pl.ds(t0 + j, 1), :], sem.at[slot]).start()
            return carry
        lax.fori_loop(0, tm // SUBLANES, body, 0)

    @pl.when(i == 0)
    def _():
        issue(dst_ref, 0)

    @pl.when(i + 1 < n)
    def _():
        issue(dstn_ref, (i + 1) % 2)

    slot = i % 2
    for kk in range(TOP_K):
        pltpu.make_async_copy(y_hbm.at[pl.ds(0, tm), :], ybuf.at[slot, kk], sem.at[slot]).wait()
    half = ybuf.shape[-1]
    r_lo = r_hi = None
    for kk in range(TOP_K):
        lo, hi = _unpack_halves(ybuf[slot, kk])
        gk = gate_ref[:, kk:kk + 1]
        r_lo = gk * lo if r_lo is None else r_lo + gk * lo
        r_hi = gk * hi if r_hi is None else r_hi + gk * hi
    x_lo = base_ref[:, :half] + g2_ref[:, :half] * r_lo
    x_hi = base_ref[:, half:] + g2_ref[:, half:] * r_hi
    ssq = jnp.sum(x_lo * x_lo, axis=-1, keepdims=True) + jnp.sum(x_hi * x_hi, axis=-1, keepdims=True)
    inv = lax.rsqrt(ssq / (2 * half) + NORM_EPS)
    o_ref[:, :half] = x_lo * inv * fg_ref[:, :half]
    o_ref[:, half:] = x_hi * inv * fg_ref[:, half:]


def _combine(y_sorted, dest, gate, base, g2, final_g):
    t, d = base.shape
    tm = min(COMBINE_TOKENS, t)
    nt = t // tm
    dest3 = dest.reshape(nt, 1, tm * TOP_K)
    n_exp = gate.shape[1]
    vec = pl.BlockSpec((1, d), lambda i: (0, 0))
    return pl.pallas_call(
        _combine_body,
        out_shape=jax.ShapeDtypeStruct((t, d), F32),
        grid=(nt,),
        in_specs=[pl.BlockSpec((1, 1, tm * TOP_K), lambda i: (i, 0, 0), memory_space=pltpu.SMEM),
                  pl.BlockSpec((1, 1, tm * TOP_K), lambda i: (jnp.minimum(i + 1, nt - 1), 0, 0),
                               memory_space=pltpu.SMEM),
                  pl.BlockSpec(memory_space=pl.ANY),
                  pl.BlockSpec((tm, n_exp), lambda i: (i, 0)),
                  pl.BlockSpec((tm, d), lambda i: (i, 0)), vec, vec],
        out_specs=pl.BlockSpec((tm, d), lambda i: (i, 0)),
        scratch_shapes=[pltpu.VMEM((2, TOP_K, tm, d // 2), jnp.uint32), pltpu.SemaphoreType.DMA((2,))],
        compiler_params=_params(("arbitrary",)),
        name="moe_combine",
    )(dest3, dest3, y_sorted, gate, base, g2, final_g.reshape(1, -1))


def _segment_tables(counts, n_tok):
    n_exp = counts.shape[0]
    rows = MOE_ROWS
    padded = (counts + rows - 1) // rows * rows
    ends = jnp.cumsum(padded)
    starts = ends - padded
    nb = n_tok * TOP_K // rows + n_exp + 1
    first_row = jnp.arange(nb, dtype=jnp.int32) * rows
    blk_expert = jnp.sum((ends[None, :] <= first_row[:, None]).astype(jnp.int32), axis=1)
    blk_expert = jnp.minimum(blk_expert, n_exp - 1)
    n_used = ends[-1] // rows
    blk = jnp.arange(nb, dtype=jnp.int32)
    prev_expert = jnp.concatenate([jnp.full((1,), -1, jnp.int32), blk_expert[:-1]])
    blk_first = ((blk_expert != prev_expert) & (blk < n_used)).astype(jnp.int32)
    blk_slot = (jnp.cumsum(blk_first) - 1) % 2
    after = ends[blk_expert] // rows
    blk_next = jnp.where(after < n_used, blk_expert[jnp.minimum(after, nb - 1)], -1)
    tables = [t.astype(jnp.int32) for t in (blk_expert, n_used.reshape(1), blk_first, blk_next, blk_slot)]
    return starts.astype(jnp.int32), (starts + counts).astype(jnp.int32), tables, nb * rows


def _layer(x, c, mod_w, mod_b, norm1_g, norm2_g, w_in, w_out, a_ln_g, a_ln_b, a_spatial_w, a_spatial_b,
           b_shift_mu, b_decay_up, b_decay_base, b_iclr_up, b_iclr_base, b_gate_up, b_kk_scale,
           b_ka_scale, b_bonus, b_gn_g, b_gn_b, router_w, router_bias, exp_w_gate, exp_w_up,
           exp_w_down, sh_w_gate, sh_w_up, sh_w_down):
    t, d = x.shape
    a_width = a_ln_g.shape[0]
    bw = b_decay_base.shape[0]
    lora_w = 512
    lora_col = 2 * a_width + 3 * bw
    assert lora_col % lora_w == 0 and w_in.shape[1] - lora_col <= lora_w

    mod = _mod_vector(c, mod_w, mod_b)
    sh1, sc1, g1, sh2, sc2, g2 = [mod[:, i * d:(i + 1) * d] for i in range(6)]

    h1 = _normmod(x, norm1_g.reshape(1, d), sh1, sc1)
    proj = _in_projection(h1, w_in.T, tn=lora_w)
    ya = _mixer_a(proj, a_ln_g, a_ln_b, a_spatial_w, a_spatial_b, a_width)
    yb = _mixer_b(proj, 2 * a_width, bw, lora_col, lora_w, b_shift_mu, b_decay_up, b_decay_base, b_iclr_up,
                  b_iclr_base, b_gate_up, b_kk_scale, b_ka_scale, b_bonus.reshape(-1), b_gn_g, b_gn_b)
    x1 = _out_projection(ya, yb, w_out, x, g1)

    n2 = norm2_g.reshape(1, d)
    top_e, gate, pos_sel, counts = _router(x1, n2, sh2, sc2, router_w, router_bias)
    starts, seg_end, blk_tables, n_rows = _segment_tables(counts[0], t)
    dest = _dest_rows(starts, top_e, pos_sel)[:, :TOP_K]
    base, x_sorted = _dispatch(x1, n2, sh2, sc2, sh_w_gate, sh_w_up, sh_w_down, g2, seg_end, blk_tables[1],
                               dest, n_rows)
    y_sorted = _routed_experts(x_sorted, *blk_tables, exp_w_gate, exp_w_up, exp_w_down)
    return base, g2, y_sorted, dest, gate


def kernel(x, c, mod_w, mod_b, norm1_g, norm2_g, w_in, w_out, a_ln_g, a_ln_b, a_spatial_w, a_spatial_b, b_shift_mu, b_decay_up, b_decay_base, b_iclr_up, b_iclr_base, b_gate_up, b_kk_scale, b_ka_scale, b_bonus, b_gn_g, b_gn_b, router_w, router_bias, exp_w_gate, exp_w_up, exp_w_down, sh_w_gate, sh_w_up, sh_w_down, final_g):
    batch, seq, d = x.shape
    assert batch == 1 and mod_w.shape[0] == 1, "single sequence, single layer"
    layer = [p[0] for p in (mod_w, mod_b, norm1_g, norm2_g, w_in, w_out, a_ln_g, a_ln_b, a_spatial_w,
                            a_spatial_b, b_shift_mu, b_decay_up, b_decay_base, b_iclr_up, b_iclr_base,
                            b_gate_up, b_kk_scale, b_ka_scale, b_bonus, b_gn_g, b_gn_b, router_w,
                            router_bias, exp_w_gate, exp_w_up, exp_w_down, sh_w_gate, sh_w_up, sh_w_down)]
    base, g2, y_sorted, dest, gate = _layer(x[0], c, *layer)
    out = _combine(y_sorted, dest, gate, base, g2, final_g)
    return out.reshape(batch, seq, d)
```

```python
import functools

import jax
import jax.numpy as jnp
from jax import lax
from jax.experimental import pallas as pl
from jax.experimental.pallas import tpu as pltpu

A_GROUP_DIM = 128
B_HEAD_DIM = 64
DECAY_LORA = 96
ICLR_LORA = 96
GATE_LORA = 256
TOP_K = 8
N_GROUPS = 8
TOPK_GROUPS = 4
ROUTED_SCALE = 2.5
NORM_EPS = 1e-6
LN_EPS = 1e-5
GN_EPS = 64e-5
KK_EPS = 1e-12

LANES = 128
SUBLANES = 8
VMEM_LIMIT_BYTES = 58 * 1024 * 1024
SCAN_CHUNK = 64
MOE_ROWS = 256
COMBINE_TOKENS = 128

F32 = jnp.float32
BF16 = jnp.bfloat16


def _params(sem, vmem=VMEM_LIMIT_BYTES):
    return pltpu.CompilerParams(dimension_semantics=sem, vmem_limit_bytes=vmem)


def _dot(a, b):
    return jnp.dot(a, b, preferred_element_type=F32)


MXU_WIDTH = 256


def _even_matmul(x, w_ref):
    hm = x.shape[0] // 2
    tiles = w_ref.shape[1] // MXU_WIDTH
    rows = [jnp.concatenate([_dot(x[r * hm:(r + 1) * hm],
                                  w_ref[:, c * MXU_WIDTH:(c + 1) * MXU_WIDTH].astype(BF16))
                             for c in range(tiles)], axis=1) for r in range(2)]
    return jnp.concatenate(rows, axis=0)


def _mod_body(c_ref, w_ref, b_ref, o_ref, s_ref):
    @pl.when(pl.program_id(0) == 0)
    def _():
        c = c_ref[...]
        s_ref[...] = jnp.broadcast_to(c * jax.nn.sigmoid(c), s_ref.shape)

    s = s_ref[...]
    cols = [jnp.sum(w_ref[:, j * LANES:(j + 1) * LANES] * s, axis=0, keepdims=True)
            for j in range(w_ref.shape[1] // LANES)]
    o_ref[...] = jnp.concatenate(cols, axis=1) + b_ref[...]


def _mod_vector(c, mod_w, mod_b, tn=512):
    d, n = mod_w.shape
    return pl.pallas_call(
        _mod_body,
        out_shape=jax.ShapeDtypeStruct((1, n), F32),
        grid=(n // tn,),
        in_specs=[pl.BlockSpec((d, 1), lambda j: (0, 0)),
                  pl.BlockSpec((d, tn), lambda j: (0, j)),
                  pl.BlockSpec((1, tn), lambda j: (0, j))],
        out_specs=pl.BlockSpec((1, tn), lambda j: (0, j)),
        scratch_shapes=[pltpu.VMEM((d, LANES), F32)],
        compiler_params=_params(("arbitrary",)),
        name="mod_vector",
    )(c.reshape(d, 1), mod_w, mod_b.reshape(1, n))


def _norm_modulate(x, g, shift, scale):
    ms = jnp.mean(x * x, axis=-1, keepdims=True)
    y = x * lax.rsqrt(ms + NORM_EPS) * g
    return y * (1.0 + scale) + shift


def _normmod_body(x_ref, g_ref, sh_ref, sc_ref, o_ref):
    o_ref[...] = _norm_modulate(x_ref[...], g_ref[...], sh_ref[...], sc_ref[...]).astype(o_ref.dtype)


def _normmod(x, g, shift, scale, tm=256):
    t, d = x.shape
    tm = min(tm, t)
    row = pl.BlockSpec((tm, d), lambda i: (i, 0))
    vec = pl.BlockSpec((1, d), lambda i: (0, 0))
    return pl.pallas_call(
        _normmod_body,
        out_shape=jax.ShapeDtypeStruct((t, d), BF16),
        grid=(t // tm,),
        in_specs=[row, vec, vec, vec],
        out_specs=row,
        compiler_params=_params(("arbitrary",)),
        name="norm_modulate",
    )(x, g, shift, scale)


def _proj_body(h_ref, wt_ref, o_ref, wb_ref, *, n_valid):
    @pl.when(pl.program_id(1) == 0)
    def _():
        wt = wt_ref[...]
        col = pl.program_id(0) * wt.shape[0] + lax.broadcasted_iota(jnp.int32, wt.shape, 0)
        wb_ref[...] = jnp.where(col < n_valid, wt, 0.0).T.astype(wb_ref.dtype)

    o_ref[...] = _dot(h_ref[...], wb_ref[...])


def _in_projection(h, w_in_t, tm=1024, tn=512):
    t, d = h.shape
    n = w_in_t.shape[0]
    tm = min(tm, t)
    n_pad = pl.cdiv(n, tn) * tn
    return pl.pallas_call(
        functools.partial(_proj_body, n_valid=n),
        out_shape=jax.ShapeDtypeStruct((t, n_pad), F32),
        grid=(n_pad // tn, t // tm),
        in_specs=[pl.BlockSpec((tm, d), lambda j, i: (i, 0)),
                  pl.BlockSpec((tn, d), lambda j, i: (j, 0))],
        out_specs=pl.BlockSpec((tm, tn), lambda j, i: (i, j)),
        scratch_shapes=[pltpu.VMEM((d, tn), BF16)],
        compiler_params=_params(("arbitrary", "arbitrary")),
        name="in_projection",
    )(h, w_in_t)


def _gelu(x):
    return 0.5 * x * (1.0 + lax.erf(x * (2.0 ** -0.5)))


def _mixer_a_body(u_ref, v_ref, lng_ref, lnb_ref, ws_ref, sb_ref, o_ref):
    tm = u_ref.shape[0]
    gd = A_GROUP_DIM
    row = lax.broadcasted_iota(jnp.int32, (gd, gd), 0)
    col = lax.broadcasted_iota(jnp.int32, (gd, gd), 1)
    causal = col <= row
    for g in range(u_ref.shape[1] // gd):
        sl = slice(g * gd, (g + 1) * gd)
        u = _gelu(u_ref[:, sl])
        v = _gelu(v_ref[:, sl])
        mu = jnp.mean(v, axis=-1, keepdims=True)
        dv = v - mu
        var = jnp.mean(dv * dv, axis=-1, keepdims=True)
        vn = ((dv * lax.rsqrt(var + LN_EPS)) * lng_ref[:, sl] + lnb_ref[:, sl]).astype(BF16)
        w = jnp.where(causal, ws_ref[g], 0.0).astype(BF16)
        for n in range(tm // gd):
            rs = slice(n * gd, (n + 1) * gd)
            mixed = _dot(w, vn[rs]) + sb_ref[g]
            o_ref[rs, sl] = (u[rs] * mixed).astype(o_ref.dtype)


def _mixer_a(proj, ln_g, ln_b, spatial_w, spatial_b, a_width, tm=256):
    t = proj.shape[0]
    tm = min(tm, t)
    groups = a_width // A_GROUP_DIM
    sb = jnp.broadcast_to(spatial_b[:, :, None], (groups, A_GROUP_DIM, A_GROUP_DIM))
    vec = pl.BlockSpec((1, a_width), lambda i: (0, 0))
    mat = pl.BlockSpec((groups, A_GROUP_DIM, A_GROUP_DIM), lambda i: (0, 0, 0))
    return pl.pallas_call(
        _mixer_a_body,
        out_shape=jax.ShapeDtypeStruct((t, a_width), BF16),
        grid=(t // tm,),
        in_specs=[pl.BlockSpec((tm, a_width), lambda i: (i, 0)),
                  pl.BlockSpec((tm, a_width), lambda i: (i, 1)),
                  vec, vec, mat, mat],
        out_specs=pl.BlockSpec((tm, a_width), lambda i: (i, 0)),
        compiler_params=_params(("arbitrary",)),
        name="mixer_a",
    )(proj, proj, ln_g.reshape(1, -1), ln_b.reshape(1, -1), spatial_w, sb)


def _split_bf16(x, parts):
    out = []
    for _ in range(parts - 1):
        hi = x.astype(BF16)
        out.append(hi)
        x = x - hi.astype(F32)
    out.append(x.astype(BF16))
    return out


def _head_sums(x):
    head0 = lax.broadcasted_iota(jnp.int32, (x.shape[0], LANES), 1) < B_HEAD_DIM
    parts = []
    for s in range(x.shape[1] // LANES):
        xs = x[:, s * LANES:(s + 1) * LANES]
        s0 = jnp.sum(jnp.where(head0, xs, 0.0), axis=-1, keepdims=True)
        s1 = jnp.sum(jnp.where(head0, 0.0, xs), axis=-1, keepdims=True)
        parts.append(jnp.where(head0, s0, s1))
    return jnp.concatenate(parts, axis=1)


def _softplus(x):
    return jnp.maximum(x, 0.0) + jnp.log1p(jnp.exp(-jnp.abs(x)))


def _scan_chunk(r_in, w_in, k_in, v_in, a_in, b_in, s_ref):
    c = SCAN_CHUNK
    ti = lax.broadcasted_iota(jnp.int32, (c, c), 0)
    tj = lax.broadcasted_iota(jnp.int32, (c, c), 1)
    tri_incl = (tj <= ti).astype(BF16)
    head0 = lax.broadcasted_iota(jnp.int32, (c, LANES), 1) < B_HEAD_DIM
    i2 = lax.broadcasted_iota(jnp.int32, (2 * c, 2 * c), 0)
    j2 = lax.broadcasted_iota(jnp.int32, (2 * c, 2 * c), 1)
    eye = (i2 == j2).astype(F32)
    strict_bd = (i2 // c == j2 // c) & (j2 < i2)
    ic = lax.broadcasted_iota(jnp.int32, (c, 2 * c), 0)
    jc = lax.broadcasted_iota(jnp.int32, (c, 2 * c), 1)
    incl_cat = (jc % c) <= ic
    contract0 = (((0,), (0,)), ((), ()))
    contract1 = (((1,), (1,)), ((), ()))

    def stack(x):
        return jnp.concatenate([jnp.where(head0, x, 0.0), jnp.where(head0, 0.0, x)], axis=0)

    def cat(xs, axis):
        return jnp.concatenate(xs, axis=axis)

    pairs = range(r_in.shape[1] // LANES)
    sls = [slice(p * LANES, (p + 1) * LANES) for p in pairs]
    w = [w_in[:, sl] for sl in sls]
    cum3 = [_dot(tri_incl, cat(_split_bf16(w[p], 3), 1)) for p in pairs]
    cum = [x[:, :LANES] + x[:, LANES:2 * LANES] + x[:, 2 * LANES:] for x in cum3]
    cum_last = [x[c - 1:c, :] for x in cum]
    e_neg = [jnp.exp(-cum[p]) for p in pairs]
    e_tail = [jnp.exp(cum_last[p] - cum[p]) for p in pairs]
    r_t = [r_in[:, sls[p]] * jnp.exp(cum[p]) for p in pairs]
    a_st = [stack(a_in[:, sls[p]] * jnp.exp(cum[p] - w[p])) for p in pairs]
    g_c = [jnp.exp(x) for x in cum_last]
    b = [b_in[:, sl] for sl in sls]
    k = [k_in[:, sl] for sl in sls]
    v_bd = [stack(v_in[:, sl]).astype(BF16) for sl in sls]
    lhs = [cat([a_st[p], r_t[p]], 0).astype(BF16) for p in pairs]
    rhs = [cat([stack(b[p] * e_neg[p]), stack(k[p] * e_neg[p])], 0).astype(BF16) for p in pairs]
    big = [lax.dot_general(lhs[p], rhs[p], contract1, preferred_element_type=F32) for p in pairs]
    a_ak = [jnp.where(strict_bd, x[:2 * c, 2 * c:], 0.0).astype(BF16) for x in big]
    a_rb = [jnp.where(incl_cat, x[2 * c:, :2 * c], 0.0).astype(BF16) for x in big]
    a_rk = [jnp.where(incl_cat, x[2 * c:, 2 * c:], 0.0).astype(BF16) for x in big]
    akv = [_dot(a_ak[p], v_bd[p]) for p in pairs]

    x = [jnp.where(strict_bd, y[:2 * c, :2 * c], 0.0) for y in big]
    t_inv = [eye + y for y in x]
    xb = [y.astype(BF16) for y in x]
    x = [_dot(y, y) for y in xb]
    span = 2
    while 2 * span < c:
        prod = [_dot(x[p].astype(BF16), cat([x[p], t_inv[p]], 1).astype(BF16)) for p in pairs]
        x = [y[:, :2 * c] for y in prod]
        t_inv = [t_inv[p] + prod[p][:, 2 * c:] for p in pairs]
        span *= 2
    t_inv = [t_inv[p] + _dot(x[p].astype(BF16), t_inv[p].astype(BF16)) for p in pairs]

    tub = [_dot(t_inv[p].astype(BF16), cat([a_st[p], akv[p]], 1).astype(BF16)).astype(BF16)
           for p in pairs]
    ry = [_dot(a_rb[p], tub[p]) for p in pairs]
    y0 = [ry[p][:, LANES:] + _dot(a_rk[p], v_bd[p]) for p in pairs]
    r_hat = [r_t[p] + ry[p][:, :LANES] for p in pairs]
    bg = [stack(b[p] * e_tail[p]).astype(BF16) for p in pairs]
    kg = [stack(k[p] * e_tail[p]).astype(BF16) for p in pairs]
    mn = [lax.dot_general(bg[p], tub[p], contract0, preferred_element_type=F32) for p in pairs]
    n = [mn[p][:, LANES:] + lax.dot_general(kg[p], v_bd[p], contract0, preferred_element_type=F32)
         for p in pairs]

    s0 = [s_ref[p] for p in pairs]
    fin = [_dot(cat([r_hat[p], mn[p][:, :LANES]], 0).astype(BF16), cat(_split_bf16(s0[p], 2), 1))
           for p in pairs]
    g_col = [jnp.sum(eye * g_c[p], axis=1, keepdims=True) for p in pairs]
    for p in pairs:
        s_ref[p] = g_col[p] * s0[p] + (fin[p][c:, :LANES] + fin[p][c:, LANES:]) + n[p]
    return cat([fin[p][:c, :LANES] + fin[p][:c, LANES:] + y0[p] for p in pairs], 1)


def _mixer_b_body(r_ref, k_ref, v_ref, l_ref, mur_ref, muk_ref, muv_ref, mul_ref, dup_ref, iup_ref, gup_ref,
                  dbase_ref, ibase_ref, kks_ref, kas_ref, bonus_ref, gng_ref, gnb_ref,
                  o_ref, s_ref, pr_ref, pk_ref, pv_ref, pl_ref):
    @pl.when(pl.program_id(0) == 0)
    def _():
        for ref in (s_ref, pr_ref, pk_ref, pv_ref, pl_ref):
            ref[...] = jnp.zeros(ref.shape, F32)

    def shifted(p_ref, prev_ref, mu_ref):
        p = p_ref[...]
        rowid = lax.broadcasted_iota(jnp.int32, p.shape, 0)
        prev = jnp.where(rowid == 0, prev_ref[...], pltpu.roll(p, 1, 0))
        prev_ref[...] = p[p.shape[0] - 1:, :]
        return p + mu_ref[...] * (prev - p)

    r = shifted(r_ref, pr_ref, mur_ref)
    k = shifted(k_ref, pk_ref, muk_ref)
    v = shifted(v_ref, pv_ref, muv_ref)
    lora = shifted(l_ref, pl_ref, mul_ref)

    dec_in = dbase_ref[...] + _dot(jnp.tanh(lora).astype(BF16), dup_ref[...])
    w_log = -_softplus(-dec_in) - 0.5
    log_decay = -jnp.exp(w_log)
    a = jax.nn.sigmoid(ibase_ref[...] + _dot(lora.astype(BF16), iup_ref[...]))
    g = _dot(jax.nn.sigmoid(lora).astype(BF16), gup_ref[...])

    kk = k * kks_ref[...]
    kk = kk / jnp.maximum(jnp.sqrt(_head_sums(kk * kk)), KK_EPS)
    k = k * (1.0 + (a - 1.0) * kas_ref[...])
    bonus_v = _head_sums(r * k * bonus_ref[...]) * v

    y = _scan_chunk(r, log_decay, k, v, -kk, kk * a, s_ref)

    inv_n = 1.0 / B_HEAD_DIM
    mu = _head_sums(y) * inv_n
    d = y - mu
    var = _head_sums(d * d) * inv_n
    yn = (d * lax.rsqrt(var + GN_EPS)) * gng_ref[...] + gnb_ref[...]
    o_ref[...] = ((yn + bonus_v) * g).astype(o_ref.dtype)


def _mixer_b(proj, a_cols, bw, lora_col, lora_w, shift_mu, decay_up, decay_base, iclr_up, iclr_base,
             gate_up, kk_scale, ka_scale, bonus, gn_g, gn_b):
    t = proj.shape[0]
    c = SCAN_CHUNK
    cb = a_cols // bw
    lb = lora_col // lora_w
    n_lora = DECAY_LORA + ICLR_LORA + GATE_LORA

    def pad_rows(w, start):
        return jnp.zeros((lora_w, bw), F32).at[start:start + w.shape[0]].set(w).astype(BF16)

    dup = pad_rows(decay_up, 0)
    iup = pad_rows(iclr_up, DECAY_LORA)
    gup = pad_rows(gate_up, DECAY_LORA + ICLR_LORA)
    mu = shift_mu.reshape(1, -1)
    mu_l = jnp.zeros((1, lora_w), F32).at[:, :n_lora].set(mu[:, 3 * bw:])

    def cur(width, blk):
        return pl.BlockSpec((c, width), lambda i: (i, blk))

    vec = pl.BlockSpec((1, bw), lambda i: (0, 0))
    vecl = pl.BlockSpec((1, lora_w), lambda i: (0, 0))
    up = pl.BlockSpec((lora_w, bw), lambda i: (0, 0))
    row = lambda x: x.reshape(1, -1)
    return pl.pallas_call(
        _mixer_b_body,
        out_shape=jax.ShapeDtypeStruct((t, bw), BF16),
        grid=(t // c,),
        in_specs=[cur(bw, cb), cur(bw, cb + 1), cur(bw, cb + 2), cur(lora_w, lb),
                  vec, vec, vec, vecl, up, up, up, vec, vec, vec, vec, vec, vec, vec],
        out_specs=pl.BlockSpec((c, bw), lambda i: (i, 0)),
        scratch_shapes=[pltpu.VMEM((bw // LANES, LANES, LANES), F32),
                        pltpu.VMEM((1, bw), F32), pltpu.VMEM((1, bw), F32), pltpu.VMEM((1, bw), F32),
                        pltpu.VMEM((1, lora_w), F32)],
        compiler_params=_params(("arbitrary",)),
        name="mixer_b",
    )(proj, proj, proj, proj, mu[:, :bw], mu[:, bw:2 * bw], mu[:, 2 * bw:3 * bw], mu_l, dup, iup, gup,
      row(decay_base), row(iclr_base), row(kk_scale), row(ka_scale), row(bonus), row(gn_g), row(gn_b))


def _out_proj_body(ya_ref, yb_ref, w_ref, x_ref, g1_ref, o_ref, wb_ref):
    @pl.when(pl.program_id(1) == 0)
    def _():
        wb_ref[...] = w_ref[...].astype(wb_ref.dtype)

    half = ya_ref.shape[1]
    mix = _dot(ya_ref[...], wb_ref[:half, :]) + _dot(yb_ref[...], wb_ref[half:, :])
    o_ref[...] = x_ref[...] + g1_ref[...] * mix


def _out_projection(ya, yb, w_out, x, g1, tm=1024, tn=512):
    t, d = x.shape
    tm = min(tm, t)
    half = ya.shape[1]
    act = pl.BlockSpec((tm, half), lambda j, i: (i, 0))
    tile = pl.BlockSpec((tm, tn), lambda j, i: (i, j))
    return pl.pallas_call(
        _out_proj_body,
        out_shape=jax.ShapeDtypeStruct((t, d), F32),
        grid=(d // tn, t // tm),
        in_specs=[act, act, pl.BlockSpec((2 * half, tn), lambda j, i: (0, j)), tile,
                  pl.BlockSpec((1, tn), lambda j, i: (0, j))],
        out_specs=tile,
        scratch_shapes=[pltpu.VMEM((2 * half, tn), BF16)],
        compiler_params=_params(("arbitrary", "arbitrary")),
        name="out_projection",
    )(ya, yb, w_out, x, g1)


def _router_body(x_ref, g_ref, sh_ref, sc_ref, rw_ref, rb_ref,
                 te_ref, gate_ref, pos_ref, cnt_ref, carry_ref):
    @pl.when(pl.program_id(0) == 0)
    def _():
        carry_ref[...] = jnp.zeros(carry_ref.shape, F32)

    x = x_ref[...]
    h = _norm_modulate(x, g_ref[...], sh_ref[...], sc_ref[...])
    tm = x.shape[0]
    n_exp = rw_ref.shape[1]
    per_group = n_exp // N_GROUPS
    neg_inf = -jnp.inf

    h_hi, h_lo = _split_bf16(h, 2)
    w_hi, w_lo = _split_bf16(rw_ref[...], 2)
    scores = jax.nn.sigmoid(_dot(h_hi, w_hi) + (_dot(h_hi, w_lo) + _dot(h_lo, w_hi)))
    sel = scores + rb_ref[...]
    lane = lax.broadcasted_iota(jnp.int32, (tm, n_exp), 1)
    gid = lane // per_group

    gscore = jnp.zeros((tm, n_exp), F32)
    for gi in range(N_GROUPS):
        in_g = gid == gi
        m1 = jnp.max(jnp.where(in_g, sel, neg_inf), axis=-1, keepdims=True)
        i1 = jnp.min(jnp.where(in_g & (sel == m1), lane, n_exp), axis=-1, keepdims=True)
        m2 = jnp.max(jnp.where(in_g & (lane != i1), sel, neg_inf), axis=-1, keepdims=True)
        gscore = jnp.where(in_g, m1 + m2, gscore)
    beaten = jnp.zeros((tm, n_exp), jnp.int32)
    for s in range(1, N_GROUPS):
        other = pltpu.roll(gscore, s * per_group, 1)
        wins = (other > gscore) | ((other == gscore) & (gid >= s))
        beaten = beaten + wins.astype(jnp.int32)
    allowed = beaten < TOPK_GROUPS

    masked = jnp.where(allowed, sel, neg_inf)
    chosen = jnp.zeros((tm, n_exp), F32)
    idxs, gates = [], []
    for _ in range(TOP_K):
        m = jnp.max(masked, axis=-1, keepdims=True)
        idx = jnp.min(jnp.where(masked == m, lane, n_exp), axis=-1, keepdims=True)
        hit = lane == idx
        gates.append(jnp.sum(jnp.where(hit, scores, 0.0), axis=-1, keepdims=True))
        idxs.append(idx)
        chosen = jnp.where(hit, 1.0, chosen)
        masked = jnp.where(hit, neg_inf, masked)
    gsum = gates[0]
    for gk in gates[1:]:
        gsum = gsum + gk

    ri = lax.broadcasted_iota(jnp.int32, (tm, tm), 0)
    ci = lax.broadcasted_iota(jnp.int32, (tm, tm), 1)
    before = (ci < ri).astype(BF16)
    pos = carry_ref[...] + _dot(before, chosen.astype(BF16))
    carry_ref[...] = carry_ref[...] + jnp.sum(chosen, axis=0, keepdims=True)
    cnt_ref[...] = jnp.broadcast_to(carry_ref[...], cnt_ref.shape).astype(jnp.int32)

    te = jnp.zeros((tm, n_exp), jnp.int32)
    gt = jnp.zeros((tm, n_exp), F32)
    ps = jnp.zeros((tm, n_exp), jnp.int32)
    for kk in range(TOP_K):
        slot = lane == kk
        hit = lane == idxs[kk]
        pk = jnp.sum(jnp.where(hit, pos, 0.0), axis=-1, keepdims=True)
        te = jnp.where(slot, idxs[kk], te)
        gt = jnp.where(slot, gates[kk] / gsum * ROUTED_SCALE, gt)
        ps = jnp.where(slot, pk.astype(jnp.int32), ps)
    te_ref[...] = te
    gate_ref[...] = gt
    pos_ref[...] = ps


def _router(x1, norm_g, shift, scale, router_w, router_bias, tm=256):
    t, d = x1.shape
    tm = min(tm, t)
    n_exp = router_w.shape[1]
    row = pl.BlockSpec((tm, d), lambda i: (i, 0))
    vec = pl.BlockSpec((1, d), lambda i: (0, 0))
    small = pl.BlockSpec((tm, n_exp), lambda i: (i, 0))
    return pl.pallas_call(
        _router_body,
        out_shape=[jax.ShapeDtypeStruct((t, n_exp), jnp.int32), jax.ShapeDtypeStruct((t, n_exp), F32),
                   jax.ShapeDtypeStruct((t, n_exp), jnp.int32), jax.ShapeDtypeStruct((8, n_exp), jnp.int32)],
        grid=(t // tm,),
        in_specs=[row, vec, vec, vec,
                  pl.BlockSpec((d, n_exp), lambda i: (0, 0)), pl.BlockSpec((1, n_exp), lambda i: (0, 0))],
        out_specs=[small, small, small, pl.BlockSpec((8, n_exp), lambda i: (0, 0))],
        scratch_shapes=[pltpu.VMEM((1, n_exp), F32)],
        compiler_params=_params(("arbitrary",)),
        name="moe_router",
    )(x1, norm_g, shift, scale, router_w, router_bias.reshape(1, -1))


def _dest_body(starts_ref, te_ref, pos_ref, o_ref):
    te = te_ref[...]
    acc = jnp.zeros(te.shape, jnp.int32)
    for e in range(starts_ref.shape[0]):
        acc = jnp.where(te == e, starts_ref[e], acc)
    o_ref[...] = acc + pos_ref[...]


def _dest_rows(starts, top_e, pos_sel, tm=512):
    t, n_exp = top_e.shape
    tm = min(tm, t)
    blk = pl.BlockSpec((tm, n_exp), lambda i: (i, 0))
    return pl.pallas_call(
        _dest_body,
        out_shape=jax.ShapeDtypeStruct((t, n_exp), jnp.int32),
        grid=(t // tm,),
        in_specs=[pl.BlockSpec(memory_space=pltpu.SMEM), blk, blk],
        out_specs=blk,
        compiler_params=_params(("arbitrary",)),
        name="moe_dest",
    )(starts, top_e, pos_sel)


def _pack_halves(x):
    half = x.shape[1] // 2
    lo = pltpu.bitcast(x[:, :half].astype(BF16).astype(F32), jnp.uint32)
    hi = pltpu.bitcast(x[:, half:].astype(BF16).astype(F32), jnp.uint32)
    return (lo >> 16) | (hi & jnp.uint32(0xFFFF0000))


def _unpack_halves(p):
    lo = pltpu.bitcast(p << 16, F32)
    hi = pltpu.bitcast(p & jnp.uint32(0xFFFF0000), F32)
    return lo, hi


def _dispatch_body(end_ref, nb_ref, dst_ref, x_ref, g_ref, sh_ref, sc_ref, sgu_ref, sd_ref, g2_ref,
                   base_ref, xs_hbm, hp_buf, zero_buf, sem, zsem):
    i = pl.program_id(0)
    n = pl.num_programs(0)
    tm = x_ref.shape[0]
    rows = zero_buf.shape[0]
    n_exp = end_ref.shape[0]
    n_blocks = xs_hbm.shape[0] // rows

    def drain(slot):
        for _ in range(TOP_K):
            pltpu.make_async_copy(hp_buf.at[slot], xs_hbm.at[pl.ds(0, tm), :], sem.at[slot]).wait()

    @pl.when(i == 0)
    def _():
        zero_buf[...] = jnp.zeros(zero_buf.shape, zero_buf.dtype)
        nb = nb_ref[0]

        def zero_rows(start, count):
            first = start if count == 1 else pl.multiple_of(start, 8)
            return pltpu.make_async_copy(zero_buf.at[pl.ds(0, count), :], xs_hbm.at[pl.ds(first, count), :], zsem)

        def issue(start, stop, count):
            def body(j, carry):
                zero_rows(start + j * count, count).start()
                return carry
            n_copies = (stop - start) // count
            lax.fori_loop(0, n_copies, body, 0)
            return n_copies

        def fill(e, carry):
            n1, n8 = carry
            end = end_ref[e]
            aligned = (end + 7) // 8 * 8
            stop = (end + rows - 1) // rows * rows
            return n1 + issue(end, aligned, 1), n8 + issue(aligned, stop, 8)
        n1, n8 = lax.fori_loop(0, n_exp, fill, (0, 0))
        n_tail = issue(nb * rows, n_blocks * rows, rows)

        for count, n_copies in ((1, n1), (8, n8), (rows, n_tail)):
            def wait(j, carry, count=count):
                zero_rows(0, count).wait()
                return carry
            lax.fori_loop(0, n_copies, wait, 0)

    slot = i % 2

    @pl.when(i >= 2)
    def _():
        drain(slot)

    x = x_ref[...]
    h = _norm_modulate(x, g_ref[...], sh_ref[...], sc_ref[...])
    hp_buf[slot] = _pack_halves(h)

    def body(g, carry):
        t0 = pl.multiple_of(g * SUBLANES, SUBLANES)
        for j in range(SUBLANES):
            for kk in range(TOP_K):
                pltpu.make_async_copy(hp_buf.at[slot, pl.ds(t0 + j, 1), :],
                                      xs_hbm.at[pl.ds(dst_ref[0, 0, (t0 + j) * TOP_K + kk], 1), :],
                                      sem.at[slot]).start()
        return carry
    lax.fori_loop(0, tm // SUBLANES, body, 0)

    ds = sd_ref.shape[0]
    gu = _even_matmul(h.astype(BF16), sgu_ref)
    act = (jax.nn.silu(gu[:, :ds]) * gu[:, ds:]).astype(BF16)
    base_ref[...] = x + g2_ref[...] * _dot(act, sd_ref[...])

    @pl.when(i == n - 1)
    def _():
        drain(slot)

        @pl.when(n > 1)
        def _():
            drain(1 - slot)


def _dispatch(x1, norm_g, shift, scale, sh_w_gate, sh_w_up, sh_w_down, g2, seg_end, n_used, dest, n_rows, tm=256):
    t, d = x1.shape
    tm = min(tm, t)
    nt = t // tm
    ds = sh_w_gate.shape[1]
    sgu = jnp.concatenate([sh_w_gate, sh_w_up], axis=1).astype(BF16)
    dest3 = dest.reshape(nt, 1, tm * TOP_K)
    row = pl.BlockSpec((tm, d), lambda i, e, nb: (i, 0))
    vec = pl.BlockSpec((1, d), lambda i, e, nb: (0, 0))
    grid_spec = pltpu.PrefetchScalarGridSpec(
        num_scalar_prefetch=2,
        grid=(nt,),
        in_specs=[pl.BlockSpec((1, 1, tm * TOP_K), lambda i, e, nb: (i, 0, 0), memory_space=pltpu.SMEM),
                  row, vec, vec, vec,
                  pl.BlockSpec((d, 2 * ds), lambda i, e, nb: (0, 0)),
                  pl.BlockSpec((ds, d), lambda i, e, nb: (0, 0)), vec],
        out_specs=[row, pl.BlockSpec(memory_space=pl.ANY)],
        scratch_shapes=[pltpu.VMEM((2, tm, d // 2), jnp.uint32), pltpu.VMEM((MOE_ROWS, d // 2), jnp.uint32),
                        pltpu.SemaphoreType.DMA((2,)), pltpu.SemaphoreType.DMA(())],
    )
    return pl.pallas_call(
        _dispatch_body,
        out_shape=[jax.ShapeDtypeStruct((t, d), F32), jax.ShapeDtypeStruct((n_rows, d // 2), jnp.uint32)],
        grid_spec=grid_spec,
        compiler_params=_params(("arbitrary",)),
        name="moe_dispatch",
    )(seg_end, n_used, dest3, x1, norm_g, shift, scale, sgu, sh_w_down.astype(BF16), g2)


def _expert_body(be_ref, nb_ref, first_ref, next_ref, slot_ref, x_ref, wg_hbm, wu_hbm, wd_hbm, o_ref,
                 wgu_buf, wd_buf, sem):
    b = pl.program_id(0)
    nb = nb_ref[0]
    de = wd_buf.shape[1]

    def weight_copies(e, slot):
        return [pltpu.make_async_copy(wg_hbm.at[e], wgu_buf.at[slot, :, pl.ds(0, de)], sem.at[slot]),
                pltpu.make_async_copy(wu_hbm.at[e], wgu_buf.at[slot, :, pl.ds(de, de)], sem.at[slot]),
                pltpu.make_async_copy(wd_hbm.at[e], wd_buf.at[slot], sem.at[slot])]

    @pl.when((b == 0) & (nb > 0))
    def _():
        for cp in weight_copies(be_ref[0], 0):
            cp.start()

    @pl.when(b < nb)
    def _():
        slot = slot_ref[b]

        @pl.when(first_ref[b] == 1)
        def _():
            for cp in weight_copies(0, slot):
                cp.wait()

            @pl.when(next_ref[b] >= 0)
            def _():
                for cp in weight_copies(next_ref[b], 1 - slot):
                    cp.start(priority=1)

        lo, hi = _unpack_halves(x_ref[...])
        x = jnp.concatenate([lo.astype(BF16), hi.astype(BF16)], axis=1)
        gu = _even_matmul(x, wgu_buf.at[slot])
        act = (jax.nn.silu(gu[:, :de]) * gu[:, de:]).astype(BF16)
        o_ref[...] = _pack_halves(_dot(act, wd_buf[slot].astype(BF16)))

    @pl.when(b >= nb)
    def _():
        o_ref[...] = jnp.zeros(o_ref.shape, o_ref.dtype)


def _routed_experts(x_sorted, blk_expert, n_used, blk_first, blk_next, blk_slot, w_gate, w_up, w_down):
    n_exp, d, de = w_gate.shape
    rows = MOE_ROWS
    nb = x_sorted.shape[0] // rows

    def last_used(b, be, nbr, *_):
        return (jnp.maximum(jnp.minimum(b, nbr[0] - 1), 0), 0)

    any_space = pl.BlockSpec(memory_space=pl.ANY)
    grid_spec = pltpu.PrefetchScalarGridSpec(
        num_scalar_prefetch=5,
        grid=(nb,),
        in_specs=[pl.BlockSpec((rows, d // 2), last_used), any_space, any_space, any_space],
        out_specs=pl.BlockSpec((rows, d // 2), lambda b, *_: (b, 0)),
        scratch_shapes=[pltpu.VMEM((2, d, 2 * de), F32), pltpu.VMEM((2, de, d), F32),
                        pltpu.SemaphoreType.DMA((2,))],
    )
    return pl.pallas_call(
        _expert_body,
        out_shape=jax.ShapeDtypeStruct((nb * rows, d // 2), jnp.uint32),
        grid_spec=grid_spec,
        compiler_params=_params(("arbitrary",)),
        name="routed_experts",
    )(blk_expert, n_used, blk_first, blk_next, blk_slot, x_sorted, w_gate, w_up, w_down)


def _combine_body(dst_ref, dstn_ref, y_hbm, gate_ref, base_ref, g2_ref, fg_ref, o_ref, ybuf, sem):
    i = pl.program_id(0)
    n = pl.num_programs(0)
    tm = base_ref.shape[0]

    def issue(dref, slot):
        def body(g, carry):
            t0 = pl.multiple_of(g * SUBLANES, SUBLANES)
            for j in range(SUBLANES):
                for kk in range(TOP_K):
                    pltpu.make_async_copy(y_hbm.at[pl.ds(dref[0, 0, (t0 + j) * TOP_K + kk], 1), :],
                                          ybuf.at[slot, kk, pl.ds(t0 + j, 1), :], sem.at[slot]).start()
            return carry
        lax.fori_loop(0, tm // SUBLANES, body, 0)

    @pl.when(i == 0)
    def _():
        issue(dst_ref, 0)

    @pl.when(i + 1 < n)
    def _():
        issue(dstn_ref, (i + 1) % 2)

    slot = i % 2
    for kk in range(TOP_K):
        pltpu.make_async_copy(y_hbm.at[pl.ds(0, tm), :], ybuf.at[slot, kk], sem.at[slot]).wait()
    half = ybuf.shape[-1]
    r_lo = r_hi = None
    for kk in range(TOP_K):
        lo, hi = _unpack_halves(ybuf[slot, kk])
        gk = gate_ref[:, kk:kk + 1]
        r_lo = gk * lo if r_lo is None else r_lo + gk * lo
        r_hi = gk * hi if r_hi is None else r_hi + gk * hi
    x_lo = base_ref[:, :half] + g2_ref[:, :half] * r_lo
    x_hi = base_ref[:, half:] + g2_ref[:, half:] * r_hi
    ssq = jnp.sum(x_lo * x_lo, axis=-1, keepdims=True) + jnp.sum(x_hi * x_hi, axis=-1, keepdims=True)
    inv = lax.rsqrt(ssq / (2 * half) + NORM_EPS)
    o_ref[:, :half] = x_lo * inv * fg_ref[:, :half]
    o_ref[:, half:] = x_hi * inv * fg_ref[:, half:]


def _combine(y_sorted, dest, gate, base, g2, final_g):
    t, d = base.shape
    tm = min(COMBINE_TOKENS, t)
    nt = t // tm
    dest3 = dest.reshape(nt, 1, tm * TOP_K)
    n_exp = gate.shape[1]
    vec = pl.BlockSpec((1, d), lambda i: (0, 0))
    return pl.pallas_call(
        _combine_body,
        out_shape=jax.ShapeDtypeStruct((t, d), F32),
        grid=(nt,),
        in_specs=[pl.BlockSpec((1, 1, tm * TOP_K), lambda i: (i, 0, 0), memory_space=pltpu.SMEM),
                  pl.BlockSpec((1, 1, tm * TOP_K), lambda i: (jnp.minimum(i + 1, nt - 1), 0, 0),
                               memory_space=pltpu.SMEM),
                  pl.BlockSpec(memory_space=pl.ANY),
                  pl.BlockSpec((tm, n_exp), lambda i: (i, 0)),
                  pl.BlockSpec((tm, d), lambda i: (i, 0)), vec, vec],
        out_specs=pl.BlockSpec((tm, d), lambda i: (i, 0)),
        scratch_shapes=[pltpu.VMEM((2, TOP_K, tm, d // 2), jnp.uint32), pltpu.SemaphoreType.DMA((2,))],
        compiler_params=_params(("arbitrary",)),
        name="moe_combine",
    )(dest3, dest3, y_sorted, gate, base, g2, final_g.reshape(1, -1))


def _segment_tables(counts, n_tok):
    n_exp = counts.shape[0]
    rows = MOE_ROWS
    padded = (counts + rows - 1) // rows * rows
    ends = jnp.cumsum(padded)
    starts = ends - padded
    nb = n_tok * TOP_K // rows + n_exp + 1
    first_row = jnp.arange(nb, dtype=jnp.int32) * rows
    blk_expert = jnp.sum((ends[None, :] <= first_row[:, None]).astype(jnp.int32), axis=1)
    blk_expert = jnp.minimum(blk_expert, n_exp - 1)
    n_used = ends[-1] // rows
    blk = jnp.arange(nb, dtype=jnp.int32)
    prev_expert = jnp.concatenate([jnp.full((1,), -1, jnp.int32), blk_expert[:-1]])
    blk_first = ((blk_expert != prev_expert) & (blk < n_used)).astype(jnp.int32)
    blk_slot = (jnp.cumsum(blk_first) - 1) % 2
    after = ends[blk_expert] // rows
    blk_next = jnp.where(after < n_used, blk_expert[jnp.minimum(after, nb - 1)], -1)
    tables = [t.astype(jnp.int32) for t in (blk_expert, n_used.reshape(1), blk_first, blk_next, blk_slot)]
    return starts.astype(jnp.int32), (starts + counts).astype(jnp.int32), tables, nb * rows


def _layer(x, c, mod_w, mod_b, norm1_g, norm2_g, w_in, w_out, a_ln_g, a_ln_b, a_spatial_w, a_spatial_b,
           b_shift_mu, b_decay_up, b_decay_base, b_iclr_up, b_iclr_base, b_gate_up, b_kk_scale,
           b_ka_scale, b_bonus, b_gn_g, b_gn_b, router_w, router_bias, exp_w_gate, exp_w_up,
           exp_w_down, sh_w_gate, sh_w_up, sh_w_down):
    t, d = x.shape
    a_width = a_ln_g.shape[0]
    bw = b_decay_base.shape[0]
    lora_w = 512
    lora_col = 2 * a_width + 3 * bw
    assert lora_col % lora_w == 0 and w_in.shape[1] - lora_col <= lora_w

    mod = _mod_vector(c, mod_w, mod_b)
    sh1, sc1, g1, sh2, sc2, g2 = [mod[:, i * d:(i + 1) * d] for i in range(6)]

    h1 = _normmod(x, norm1_g.reshape(1, d), sh1, sc1)
    proj = _in_projection(h1, w_in.T, tn=lora_w)
    ya = _mixer_a(proj, a_ln_g, a_ln_b, a_spatial_w, a_spatial_b, a_width)
    yb = _mixer_b(proj, 2 * a_width, bw, lora_col, lora_w, b_shift_mu, b_decay_up, b_decay_base, b_iclr_up,
                  b_iclr_base, b_gate_up, b_kk_scale, b_ka_scale, b_bonus.reshape(-1), b_gn_g, b_gn_b)
    x1 = _out_projection(ya, yb, w_out, x, g1)

    n2 = norm2_g.reshape(1, d)
    top_e, gate, pos_sel, counts = _router(x1, n2, sh2, sc2, router_w, router_bias)
    starts, seg_end, blk_tables, n_rows = _segment_tables(counts[0], t)
    dest = _dest_rows(starts, top_e, pos_sel)[:, :TOP_K]
    base, x_sorted = _dispatch(x1, n2, sh2, sc2, sh_w_gate, sh_w_up, sh_w_down, g2, seg_end, blk_tables[1],
                               dest, n_rows)
    y_sorted = _routed_experts(x_sorted, *blk_tables, exp_w_gate, exp_w_up, exp_w_down)
    return base, g2, y_sorted, dest, gate


def kernel(x, c, mod_w, mod_b, norm1_g, norm2_g, w_in, w_out, a_ln_g, a_ln_b, a_spatial_w, a_spatial_b, b_shift_mu, b_decay_up, b_decay_base, b_iclr_up, b_iclr_base, b_gate_up, b_kk_scale, b_ka_scale, b_bonus, b_gn_g, b_gn_b, router_w, router_bias, exp_w_gate, exp_w_up, exp_w_down, sh_w_gate, sh_w_up, sh_w_down, final_g):
    batch, seq, d = x.shape
    assert batch == 1 and mod_w.shape[0] == 1, "single sequence, single layer"
    layer = [p[0] for p in (mod_w, mod_b, norm1_g, norm2_g, w_in, w_out, a_ln_g, a_ln_b, a_spatial_w,
                            a_spatial_b, b_shift_mu, b_decay_up, b_decay_base, b_iclr_up, b_iclr_base,
                            b_gate_up, b_kk_scale, b_ka_scale, b_bonus, b_gn_g, b_gn_b, router_w,
                            router_bias, exp_w_gate, exp_w_up, exp_w_down, sh_w_gate, sh_w_up, sh_w_down)]
    base, g2, y_sorted, dest, gate = _layer(x[0], c, *layer)
    out = _combine(y_sorted, dest, gate, base, g2, final_g)
    return out.reshape(batch, seq, d)
```

```python
import functools

import jax
import jax.numpy as jnp
from jax import lax
from jax.experimental import pallas as pl
from jax.experimental.pallas import tpu as pltpu

A_GROUP_DIM = 128
B_HEAD_DIM = 64
DECAY_LORA = 96
ICLR_LORA = 96
GATE_LORA = 256
TOP_K = 8
N_GROUPS = 8
TOPK_GROUPS = 4
ROUTED_SCALE = 2.5
NORM_EPS = 1e-6
LN_EPS = 1e-5
GN_EPS = 64e-5
KK_EPS = 1e-12

LANES = 128
SUBLANES = 8
VMEM_LIMIT_BYTES = 58 * 1024 * 1024
SCAN_CHUNK = 64
MOE_ROWS = 256
COMBINE_TOKENS = 128

F32 = jnp.float32
BF16 = jnp.bfloat16


def _params(sem, vmem=VMEM_LIMIT_BYTES):
    return pltpu.CompilerParams(dimension_semantics=sem, vmem_limit_bytes=vmem)


def _dot(a, b):
    return jnp.dot(a, b, preferred_element_type=F32)


MXU_WIDTH = 256


def _even_matmul(x, w_ref):
    hm = x.shape[0] // 2
    tiles = w_ref.shape[1] // MXU_WIDTH
    rows = [jnp.concatenate([_dot(x[r * hm:(r + 1) * hm],
                                  w_ref[:, c * MXU_WIDTH:(c + 1) * MXU_WIDTH].astype(BF16))
                             for c in range(tiles)], axis=1) for r in range(2)]
    return jnp.concatenate(rows, axis=0)


def _mod_body(c_ref, w_ref, b_ref, o_ref, s_ref):
    @pl.when(pl.program_id(0) == 0)
    def _():
        c = c_ref[...]
        s_ref[...] = jnp.broadcast_to(c * jax.nn.sigmoid(c), s_ref.shape)

    s = s_ref[...]
    cols = [jnp.sum(w_ref[:, j * LANES:(j + 1) * LANES] * s, axis=0, keepdims=True)
            for j in range(w_ref.shape[1] // LANES)]
    o_ref[...] = jnp.concatenate(cols, axis=1) + b_ref[...]


def _mod_vector(c, mod_w, mod_b, tn=512):
    d, n = mod_w.shape
    return pl.pallas_call(
        _mod_body,
        out_shape=jax.ShapeDtypeStruct((1, n), F32),
        grid=(n // tn,),
        in_specs=[pl.BlockSpec((d, 1), lambda j: (0, 0)),
                  pl.BlockSpec((d, tn), lambda j: (0, j)),
                  pl.BlockSpec((1, tn), lambda j: (0, j))],
        out_specs=pl.BlockSpec((1, tn), lambda j: (0, j)),
        scratch_shapes=[pltpu.VMEM((d, LANES), F32)],
        compiler_params=_params(("arbitrary",)),
        name="mod_vector",
    )(c.reshape(d, 1), mod_w, mod_b.reshape(1, n))


def _norm_modulate(x, g, shift, scale):
    ms = jnp.mean(x * x, axis=-1, keepdims=True)
    y = x * lax.rsqrt(ms + NORM_EPS) * g
    return y * (1.0 + scale) + shift


def _normmod_body(x_ref, g_ref, sh_ref, sc_ref, o_ref):
    o_ref[...] = _norm_modulate(x_ref[...], g_ref[...], sh_ref[...], sc_ref[...]).astype(o_ref.dtype)


def _normmod(x, g, shift, scale, tm=256):
    t, d = x.shape
    tm = min(tm, t)
    row = pl.BlockSpec((tm, d), lambda i: (i, 0))
    vec = pl.BlockSpec((1, d), lambda i: (0, 0))
    return pl.pallas_call(
        _normmod_body,
        out_shape=jax.ShapeDtypeStruct((t, d), BF16),
        grid=(t // tm,),
        in_specs=[row, vec, vec, vec],
        out_specs=row,
        compiler_params=_params(("arbitrary",)),
        name="norm_modulate",
    )(x, g, shift, scale)


def _proj_body(h_ref, wt_ref, o_ref, wb_ref, *, n_valid):
    @pl.when(pl.program_id(1) == 0)
    def _():
        wt = wt_ref[...]
        col = pl.program_id(0) * wt.shape[0] + lax.broadcasted_iota(jnp.int32, wt.shape, 0)
        wb_ref[...] = jnp.where(col < n_valid, wt, 0.0).T.astype(wb_ref.dtype)

    o_ref[...] = _dot(h_ref[...], wb_ref[...])


def _in_projection(h, w_in_t, tm=1024, tn=512):
    t, d = h.shape
    n = w_in_t.shape[0]
    tm = min(tm, t)
    n_pad = pl.cdiv(n, tn) * tn
    return pl.pallas_call(
        functools.partial(_proj_body, n_valid=n),
        out_shape=jax.ShapeDtypeStruct((t, n_pad), F32),
        grid=(n_pad // tn, t // tm),
        in_specs=[pl.BlockSpec((tm, d), lambda j, i: (i, 0)),
                  pl.BlockSpec((tn, d), lambda j, i: (j, 0))],
        out_specs=pl.BlockSpec((tm, tn), lambda j, i: (i, j)),
        scratch_shapes=[pltpu.VMEM((d, tn), BF16)],
        compiler_params=_params(("arbitrary", "arbitrary")),
        name="in_projection",
    )(h, w_in_t)


def _gelu(x):
    return 0.5 * x * (1.0 + lax.erf(x * (2.0 ** -0.5)))


def _mixer_a_body(u_ref, v_ref, lng_ref, lnb_ref, ws_ref, sb_ref, o_ref):
    tm = u_ref.shape[0]
    gd = A_GROUP_DIM
    row = lax.broadcasted_iota(jnp.int32, (gd, gd), 0)
    col = lax.broadcasted_iota(jnp.int32, (gd, gd), 1)
    causal = col <= row
    for g in range(u_ref.shape[1] // gd):
        sl = slice(g * gd, (g + 1) * gd)
        u = _gelu(u_ref[:, sl])
        v = _gelu(v_ref[:, sl])
        mu = jnp.mean(v, axis=-1, keepdims=True)
        dv = v - mu
        var = jnp.mean(dv * dv, axis=-1, keepdims=True)
        vn = ((dv * lax.rsqrt(var + LN_EPS)) * lng_ref[:, sl] + lnb_ref[:, sl]).astype(BF16)
        w = jnp.where(causal, ws_ref[g], 0.0).astype(BF16)
        for n in range(tm // gd):
            rs = slice(n * gd, (n + 1) * gd)
            mixed = _dot(w, vn[rs]) + sb_ref[g]
            o_ref[rs, sl] = (u[rs] * mixed).astype(o_ref.dtype)


def _mixer_a(proj, ln_g, ln_b, spatial_w, spatial_b, a_width, tm=256):
    t = proj.shape[0]
    tm = min(tm, t)
    groups = a_width // A_GROUP_DIM
    sb = jnp.broadcast_to(spatial_b[:, :, None], (groups, A_GROUP_DIM, A_GROUP_DIM))
    vec = pl.BlockSpec((1, a_width), lambda i: (0, 0))
    mat = pl.BlockSpec((groups, A_GROUP_DIM, A_GROUP_DIM), lambda i: (0, 0, 0))
    return pl.pallas_call(
        _mixer_a_body,
        out_shape=jax.ShapeDtypeStruct((t, a_width), BF16),
        grid=(t // tm,),
        in_specs=[pl.BlockSpec((tm, a_width), lambda i: (i, 0)),
                  pl.BlockSpec((tm, a_width), lambda i: (i, 1)),
                  vec, vec, mat, mat],
        out_specs=pl.BlockSpec((tm, a_width), lambda i: (i, 0)),
        compiler_params=_params(("arbitrary",)),
        name="mixer_a",
    )(proj, proj, ln_g.reshape(1, -1), ln_b.reshape(1, -1), spatial_w, sb)


def _split_bf16(x, parts):
    out = []
    for _ in range(parts - 1):
        hi = x.astype(BF16)
        out.append(hi)
        x = x - hi.astype(F32)
    out.append(x.astype(BF16))
    return out


def _head_sums(x):
    head0 = lax.broadcasted_iota(jnp.int32, (x.shape[0], LANES), 1) < B_HEAD_DIM
    parts = []
    for s in range(x.shape[1] // LANES):
        xs = x[:, s * LANES:(s + 1) * LANES]
        s0 = jnp.sum(jnp.where(head0, xs, 0.0), axis=-1, keepdims=True)
        s1 = jnp.sum(jnp.where(head0, 0.0, xs), axis=-1, keepdims=True)
        parts.append(jnp.where(head0, s0, s1))
    return jnp.concatenate(parts, axis=1)


def _softplus(x):
    return jnp.maximum(x, 0.0) + jnp.log1p(jnp.exp(-jnp.abs(x)))


def _scan_chunk(r_in, w_in, k_in, v_in, a_in, b_in, s_ref):
    c = SCAN_CHUNK
    ti = lax.broadcasted_iota(jnp.int32, (c, c), 0)
    tj = lax.broadcasted_iota(jnp.int32, (c, c), 1)
    tri_incl = (tj <= ti).astype(BF16)
    head0 = lax.broadcasted_iota(jnp.int32, (c, LANES), 1) < B_HEAD_DIM
    i2 = lax.broadcasted_iota(jnp.int32, (2 * c, 2 * c), 0)
    j2 = lax.broadcasted_iota(jnp.int32, (2 * c, 2 * c), 1)
    eye = (i2 == j2).astype(F32)
    strict_bd = (i2 // c == j2 // c) & (j2 < i2)
    ic = lax.broadcasted_iota(jnp.int32, (c, 2 * c), 0)
    jc = lax.broadcasted_iota(jnp.int32, (c, 2 * c), 1)
    incl_cat = (jc % c) <= ic
    contract0 = (((0,), (0,)), ((), ()))
    contract1 = (((1,), (1,)), ((), ()))

    def stack(x):
        return jnp.concatenate([jnp.where(head0, x, 0.0), jnp.where(head0, 0.0, x)], axis=0)

    def cat(xs, axis):
        return jnp.concatenate(xs, axis=axis)

    pairs = range(r_in.shape[1] // LANES)
    sls = [slice(p * LANES, (p + 1) * LANES) for p in pairs]
    w = [w_in[:, sl] for sl in sls]
    cum3 = [_dot(tri_incl, cat(_split_bf16(w[p], 3), 1)) for p in pairs]
    cum = [x[:, :LANES] + x[:, LANES:2 * LANES] + x[:, 2 * LANES:] for x in cum3]
    cum_last = [x[c - 1:c, :] for x in cum]
    e_neg = [jnp.exp(-cum[p]) for p in pairs]
    e_tail = [jnp.exp(cum_last[p] - cum[p]) for p in pairs]
    r_t = [r_in[:, sls[p]] * jnp.exp(cum[p]) for p in pairs]
    a_st = [stack(a_in[:, sls[p]] * jnp.exp(cum[p] - w[p])) for p in pairs]
    g_c = [jnp.exp(x) for x in cum_last]
    b = [b_in[:, sl] for sl in sls]
    k = [k_in[:, sl] for sl in sls]
    v_bd = [stack(v_in[:, sl]).astype(BF16) for sl in sls]
    lhs = [cat([a_st[p], r_t[p]], 0).astype(BF16) for p in pairs]
    rhs = [cat([stack(b[p] * e_neg[p]), stack(k[p] * e_neg[p])], 0).astype(BF16) for p in pairs]
    big = [lax.dot_general(lhs[p], rhs[p], contract1, preferred_element_type=F32) for p in pairs]
    a_ak = [jnp.where(strict_bd, x[:2 * c, 2 * c:], 0.0).astype(BF16) for x in big]
    a_rb = [jnp.where(incl_cat, x[2 * c:, :2 * c], 0.0).astype(BF16) for x in big]
    a_rk = [jnp.where(incl_cat, x[2 * c:, 2 * c:], 0.0).astype(BF16) for x in big]
    akv = [_dot(a_ak[p], v_bd[p]) for p in pairs]

    x = [jnp.where(strict_bd, y[:2 * c, :2 * c], 0.0) for y in big]
    t_inv = [eye + y for y in x]
    xb = [y.astype(BF16) for y in x]
    x = [_dot(y, y) for y in xb]
    span = 2
    while 2 * span < c:
        prod = [_dot(x[p].astype(BF16), cat([x[p], t_inv[p]], 1).astype(BF16)) for p in pairs]
        x = [y[:, :2 * c] for y in prod]
        t_inv = [t_inv[p] + prod[p][:, 2 * c:] for p in pairs]
        span *= 2
    t_inv = [t_inv[p] + _dot(x[p].astype(BF16), t_inv[p].astype(BF16)) for p in pairs]

    tub = [_dot(t_inv[p].astype(BF16), cat([a_st[p], akv[p]], 1).astype(BF16)).astype(BF16)
           for p in pairs]
    ry = [_dot(a_rb[p], tub[p]) for p in pairs]
    y0 = [ry[p][:, LANES:] + _dot(a_rk[p], v_bd[p]) for p in pairs]
    r_hat = [r_t[p] + ry[p][:, :LANES] for p in pairs]
    bg = [stack(b[p] * e_tail[p]).astype(BF16) for p in pairs]
    kg = [stack(k[p] * e_tail[p]).astype(BF16) for p in pairs]
    mn = [lax.dot_general(bg[p], tub[p], contract0, preferred_element_type=F32) for p in pairs]
    n = [mn[p][:, LANES:] + lax.dot_general(kg[p], v_bd[p], contract0, preferred_element_type=F32)
         for p in pairs]

    s0 = [s_ref[p] for p in pairs]
    fin = [_dot(cat([r_hat[p], mn[p][:, :LANES]], 0).astype(BF16), cat(_split_bf16(s0[p], 2), 1))
           for p in pairs]
    g_col = [jnp.sum(eye * g_c[p], axis=1, keepdims=True) for p in pairs]
    for p in pairs:
        s_ref[p] = g_col[p] * s0[p] + (fin[p][c:, :LANES] + fin[p][c:, LANES:]) + n[p]
    return cat([fin[p][:c, :LANES] + fin[p][:c, LANES:] + y0[p] for p in pairs], 1)


def _mixer_b_body(r_ref, k_ref, v_ref, l_ref, mur_ref, muk_ref, muv_ref, mul_ref, dup_ref, iup_ref, gup_ref,
                  dbase_ref, ibase_ref, kks_ref, kas_ref, bonus_ref, gng_ref, gnb_ref,
                  o_ref, s_ref, pr_ref, pk_ref, pv_ref, pl_ref):
    @pl.when(pl.program_id(0) == 0)
    def _():
        for ref in (s_ref, pr_ref, pk_ref, pv_ref, pl_ref):
            ref[...] = jnp.zeros(ref.shape, F32)

    def shifted(p_ref, prev_ref, mu_ref):
        p = p_ref[...]
        rowid = lax.broadcasted_iota(jnp.int32, p.shape, 0)
        prev = jnp.where(rowid == 0, prev_ref[...], pltpu.roll(p, 1, 0))
        prev_ref[...] = p[p.shape[0] - 1:, :]
        return p + mu_ref[...] * (prev - p)

    r = shifted(r_ref, pr_ref, mur_ref)
    k = shifted(k_ref, pk_ref, muk_ref)
    v = shifted(v_ref, pv_ref, muv_ref)
    lora = shifted(l_ref, pl_ref, mul_ref)

    dec_in = dbase_ref[...] + _dot(jnp.tanh(lora).astype(BF16), dup_ref[...])
    w_log = -_softplus(-dec_in) - 0.5
    log_decay = -jnp.exp(w_log)
    a = jax.nn.sigmoid(ibase_ref[...] + _dot(lora.astype(BF16), iup_ref[...]))
    g = _dot(jax.nn.sigmoid(lora).astype(BF16), gup_ref[...])

    kk = k * kks_ref[...]
    kk = kk / jnp.maximum(jnp.sqrt(_head_sums(kk * kk)), KK_EPS)
    k = k * (1.0 + (a - 1.0) * kas_ref[...])
    bonus_v = _head_sums(r * k * bonus_ref[...]) * v

    y = _scan_chunk(r, log_decay, k, v, -kk, kk * a, s_ref)

    inv_n = 1.0 / B_HEAD_DIM
    mu = _head_sums(y) * inv_n
    d = y - mu
    var = _head_sums(d * d) * inv_n
    yn = (d * lax.rsqrt(var + GN_EPS)) * gng_ref[...] + gnb_ref[...]
    o_ref[...] = ((yn + bonus_v) * g).astype(o_ref.dtype)


def _mixer_b(proj, a_cols, bw, lora_col, lora_w, shift_mu, decay_up, decay_base, iclr_up, iclr_base,
             gate_up, kk_scale, ka_scale, bonus, gn_g, gn_b):
    t = proj.shape[0]
    c = SCAN_CHUNK
    cb = a_cols // bw
    lb = lora_col // lora_w
    n_lora = DECAY_LORA + ICLR_LORA + GATE_LORA

    def pad_rows(w, start):
        return jnp.zeros((lora_w, bw), F32).at[start:start + w.shape[0]].set(w).astype(BF16)

    dup = pad_rows(decay_up, 0)
    iup = pad_rows(iclr_up, DECAY_LORA)
    gup = pad_rows(gate_up, DECAY_LORA + ICLR_LORA)
    mu = shift_mu.reshape(1, -1)
    mu_l = jnp.zeros((1, lora_w), F32).at[:, :n_lora].set(mu[:, 3 * bw:])

    def cur(width, blk):
        return pl.BlockSpec((c, width), lambda i: (i, blk))

    vec = pl.BlockSpec((1, bw), lambda i: (0, 0))
    vecl = pl.BlockSpec((1, lora_w), lambda i: (0, 0))
    up = pl.BlockSpec((lora_w, bw), lambda i: (0, 0))
    row = lambda x: x.reshape(1, -1)
    return pl.pallas_call(
        _mixer_b_body,
        out_shape=jax.ShapeDtypeStruct((t, bw), BF16),
        grid=(t // c,),
        in_specs=[cur(bw, cb), cur(bw, cb + 1), cur(bw, cb + 2), cur(lora_w, lb),
                  vec, vec, vec, vecl, up, up, up, vec, vec, vec, vec, vec, vec, vec],
        out_specs=pl.BlockSpec((c, bw), lambda i: (i, 0)),
        scratch_shapes=[pltpu.VMEM((bw // LANES, LANES, LANES), F32),
                        pltpu.VMEM((1, bw), F32), pltpu.VMEM((1, bw), F32), pltpu.VMEM((1, bw), F32),
                        pltpu.VMEM((1, lora_w), F32)],
        compiler_params=_params(("arbitrary",)),
        name="mixer_b",
    )(proj, proj, proj, proj, mu[:, :bw], mu[:, bw:2 * bw], mu[:, 2 * bw:3 * bw], mu_l, dup, iup, gup,
      row(decay_base), row(iclr_base), row(kk_scale), row(ka_scale), row(bonus), row(gn_g), row(gn_b))


def _out_proj_body(ya_ref, yb_ref, w_ref, x_ref, g1_ref, o_ref, wb_ref):
    @pl.when(pl.program_id(1) == 0)
    def _():
        wb_ref[...] = w_ref[...].astype(wb_ref.dtype)

    half = ya_ref.shape[1]
    mix = _dot(ya_ref[...], wb_ref[:half, :]) + _dot(yb_ref[...], wb_ref[half:, :])
    o_ref[...] = x_ref[...] + g1_ref[...] * mix


def _out_projection(ya, yb, w_out, x, g1, tm=1024, tn=512):
    t, d = x.shape
    tm = min(tm, t)
    half = ya.shape[1]
    act = pl.BlockSpec((tm, half), lambda j, i: (i, 0))
    tile = pl.BlockSpec((tm, tn), lambda j, i: (i, j))
    return pl.pallas_call(
        _out_proj_body,
        out_shape=jax.ShapeDtypeStruct((t, d), F32),
        grid=(d // tn, t // tm),
        in_specs=[act, act, pl.BlockSpec((2 * half, tn), lambda j, i: (0, j)), tile,
                  pl.BlockSpec((1, tn), lambda j, i: (0, j))],
        out_specs=tile,
        scratch_shapes=[pltpu.VMEM((2 * half, tn), BF16)],
        compiler_params=_params(("arbitrary", "arbitrary")),
        name="out_projection",
    )(ya, yb, w_out, x, g1)


def _router_body(x_ref, g_ref, sh_ref, sc_ref, rw_ref, rb_ref, sgu_ref, sd_ref, g2_ref,
                 te_ref, gate_ref, pos_ref, cnt_ref, hp_ref, base_ref, carry_ref):
    @pl.when(pl.program_id(0) == 0)
    def _():
        carry_ref[...] = jnp.zeros(carry_ref.shape, F32)

    x = x_ref[...]
    h = _norm_modulate(x, g_ref[...], sh_ref[...], sc_ref[...])
    hp_ref[...] = _pack_halves(h)
    ds = sd_ref.shape[0]
    gu = _even_matmul(h.astype(BF16), sgu_ref)
    act = (jax.nn.silu(gu[:, :ds]) * gu[:, ds:]).astype(BF16)
    base_ref[...] = x + g2_ref[...] * _dot(act, sd_ref[...])
    tm = x.shape[0]
    n_exp = rw_ref.shape[1]
    per_group = n_exp // N_GROUPS
    neg_inf = -jnp.inf

    h_hi, h_lo = _split_bf16(h, 2)
    w_hi, w_lo = _split_bf16(rw_ref[...], 2)
    scores = jax.nn.sigmoid(_dot(h_hi, w_hi) + (_dot(h_hi, w_lo) + _dot(h_lo, w_hi)))
    sel = scores + rb_ref[...]
    lane = lax.broadcasted_iota(jnp.int32, (tm, n_exp), 1)
    gid = lane // per_group

    gscore = jnp.zeros((tm, n_exp), F32)
    for gi in range(N_GROUPS):
        in_g = gid == gi
        m1 = jnp.max(jnp.where(in_g, sel, neg_inf), axis=-1, keepdims=True)
        i1 = jnp.min(jnp.where(in_g & (sel == m1), lane, n_exp), axis=-1, keepdims=True)
        m2 = jnp.max(jnp.where(in_g & (lane != i1), sel, neg_inf), axis=-1, keepdims=True)
        gscore = jnp.where(in_g, m1 + m2, gscore)
    beaten = jnp.zeros((tm, n_exp), jnp.int32)
    for s in range(1, N_GROUPS):
        other = pltpu.roll(gscore, s * per_group, 1)
        wins = (other > gscore) | ((other == gscore) & (gid >= s))
        beaten = beaten + wins.astype(jnp.int32)
    allowed = beaten < TOPK_GROUPS

    masked = jnp.where(allowed, sel, neg_inf)
    chosen = jnp.zeros((tm, n_exp), F32)
    idxs, gates = [], []
    for _ in range(TOP_K):
        m = jnp.max(masked, axis=-1, keepdims=True)
        idx = jnp.min(jnp.where(masked == m, lane, n_exp), axis=-1, keepdims=True)
        hit = lane == idx
        gates.append(jnp.sum(jnp.where(hit, scores, 0.0), axis=-1, keepdims=True))
        idxs.append(idx)
        chosen = jnp.where(hit, 1.0, chosen)
        masked = jnp.where(hit, neg_inf, masked)
    gsum = gates[0]
    for gk in gates[1:]:
        gsum = gsum + gk

    ri = lax.broadcasted_iota(jnp.int32, (tm, tm), 0)
    ci = lax.broadcasted_iota(jnp.int32, (tm, tm), 1)
    before = (ci < ri).astype(BF16)
    pos = carry_ref[...] + _dot(before, chosen.astype(BF16))
    carry_ref[...] = carry_ref[...] + jnp.sum(chosen, axis=0, keepdims=True)
    cnt_ref[...] = jnp.broadcast_to(carry_ref[...], cnt_ref.shape).astype(jnp.int32)

    te = jnp.zeros((tm, n_exp), jnp.int32)
    gt = jnp.zeros((tm, n_exp), F32)
    ps = jnp.zeros((tm, n_exp), jnp.int32)
    for kk in range(TOP_K):
        slot = lane == kk
        hit = lane == idxs[kk]
        pk = jnp.sum(jnp.where(hit, pos, 0.0), axis=-1, keepdims=True)
        te = jnp.where(slot, idxs[kk], te)
        gt = jnp.where(slot, gates[kk] / gsum * ROUTED_SCALE, gt)
        ps = jnp.where(slot, pk.astype(jnp.int32), ps)
    te_ref[...] = te
    gate_ref[...] = gt
    pos_ref[...] = ps


def _router(x1, norm_g, shift, scale, router_w, router_bias, sh_w_gate, sh_w_up, sh_w_down, g2, tm=256):
    t, d = x1.shape
    tm = min(tm, t)
    n_exp = router_w.shape[1]
    ds = sh_w_gate.shape[1]
    sgu = jnp.concatenate([sh_w_gate, sh_w_up], axis=1).astype(BF16)
    row = pl.BlockSpec((tm, d), lambda i: (i, 0))
    vec = pl.BlockSpec((1, d), lambda i: (0, 0))
    small = pl.BlockSpec((tm, n_exp), lambda i: (i, 0))
    return pl.pallas_call(
        _router_body,
        out_shape=[jax.ShapeDtypeStruct((t, n_exp), jnp.int32), jax.ShapeDtypeStruct((t, n_exp), F32),
                   jax.ShapeDtypeStruct((t, n_exp), jnp.int32), jax.ShapeDtypeStruct((8, n_exp), jnp.int32),
                   jax.ShapeDtypeStruct((t, d // 2), jnp.uint32), jax.ShapeDtypeStruct((t, d), F32)],
        grid=(t // tm,),
        in_specs=[row, vec, vec, vec,
                  pl.BlockSpec((d, n_exp), lambda i: (0, 0)), pl.BlockSpec((1, n_exp), lambda i: (0, 0)),
                  pl.BlockSpec((d, 2 * ds), lambda i: (0, 0)), pl.BlockSpec((ds, d), lambda i: (0, 0)), vec],
        out_specs=[small, small, small, pl.BlockSpec((8, n_exp), lambda i: (0, 0)),
                   pl.BlockSpec((tm, d // 2), lambda i: (i, 0)), row],
        scratch_shapes=[pltpu.VMEM((1, n_exp), F32)],
        compiler_params=_params(("arbitrary",)),
        name="moe_router",
    )(x1, norm_g, shift, scale, router_w, router_bias.reshape(1, -1), sgu, sh_w_down.astype(BF16), g2)


def _dest_body(starts_ref, te_ref, pos_ref, o_ref):
    te = te_ref[...]
    acc = jnp.zeros(te.shape, jnp.int32)
    for e in range(starts_ref.shape[0]):
        acc = jnp.where(te == e, starts_ref[e], acc)
    o_ref[...] = acc + pos_ref[...]


def _dest_rows(starts, top_e, pos_sel, tm=512):
    t, n_exp = top_e.shape
    tm = min(tm, t)
    blk = pl.BlockSpec((tm, n_exp), lambda i: (i, 0))
    return pl.pallas_call(
        _dest_body,
        out_shape=jax.ShapeDtypeStruct((t, n_exp), jnp.int32),
        grid=(t // tm,),
        in_specs=[pl.BlockSpec(memory_space=pltpu.SMEM), blk, blk],
        out_specs=blk,
        compiler_params=_params(("arbitrary",)),
        name="moe_dest",
    )(starts, top_e, pos_sel)


def _pack_halves(x):
    half = x.shape[1] // 2
    lo = pltpu.bitcast(x[:, :half].astype(BF16).astype(F32), jnp.uint32)
    hi = pltpu.bitcast(x[:, half:].astype(BF16).astype(F32), jnp.uint32)
    return (lo >> 16) | (hi & jnp.uint32(0xFFFF0000))


def _unpack_halves(p):
    lo = pltpu.bitcast(p << 16, F32)
    hi = pltpu.bitcast(p & jnp.uint32(0xFFFF0000), F32)
    return lo, hi


def _dispatch_body(end_ref, nb_ref, dst_ref, hp_ref, xs_hbm, hp_buf, zero_buf, sem, zsem):
    i = pl.program_id(0)
    n = pl.num_programs(0)
    tm = hp_ref.shape[0]
    rows = zero_buf.shape[0]
    n_exp = end_ref.shape[0]
    n_blocks = xs_hbm.shape[0] // rows

    def drain(slot):
        for _ in range(TOP_K):
            pltpu.make_async_copy(hp_buf.at[slot], xs_hbm.at[pl.ds(0, tm), :], sem.at[slot]).wait()

    @pl.when(i == 0)
    def _():
        zero_buf[...] = jnp.zeros(zero_buf.shape, zero_buf.dtype)
        nb = nb_ref[0]

        def zero_rows(start, count):
            first = start if count == 1 else pl.multiple_of(start, 8)
            return pltpu.make_async_copy(zero_buf.at[pl.ds(0, count), :], xs_hbm.at[pl.ds(first, count), :], zsem)

        def issue(start, stop, count):
            def body(j, carry):
                zero_rows(start + j * count, count).start()
                return carry
            n_copies = (stop - start) // count
            lax.fori_loop(0, n_copies, body, 0)
            return n_copies

        def fill(e, carry):
            n1, n8 = carry
            end = end_ref[e]
            aligned = (end + 7) // 8 * 8
            stop = (end + rows - 1) // rows * rows
            return n1 + issue(end, aligned, 1), n8 + issue(aligned, stop, 8)
        n1, n8 = lax.fori_loop(0, n_exp, fill, (0, 0))
        n_tail = issue(nb * rows, n_blocks * rows, rows)

        for count, n_copies in ((1, n1), (8, n8), (rows, n_tail)):
            def wait(j, carry, count=count):
                zero_rows(0, count).wait()
                return carry
            lax.fori_loop(0, n_copies, wait, 0)

    slot = i % 2

    @pl.when(i >= 2)
    def _():
        drain(slot)

    hp_buf[slot] = hp_ref[...]

    def body(g, carry):
        t0 = pl.multiple_of(g * SUBLANES, SUBLANES)
        for j in range(SUBLANES):
            for kk in range(TOP_K):
                pltpu.make_async_copy(hp_buf.at[slot, pl.ds(t0 + j, 1), :],
                                      xs_hbm.at[pl.ds(dst_ref[0, 0, (t0 + j) * TOP_K + kk], 1), :],
                                      sem.at[slot]).start()
        return carry
    lax.fori_loop(0, tm // SUBLANES, body, 0)

    @pl.when(i == n - 1)
    def _():
        drain(slot)

        @pl.when(n > 1)
        def _():
            drain(1 - slot)


def _dispatch(h_packed, seg_end, n_used, dest, n_rows, tm=256):
    t, half = h_packed.shape
    tm = min(tm, t)
    nt = t // tm
    dest3 = dest.reshape(nt, 1, tm * TOP_K)
    grid_spec = pltpu.PrefetchScalarGridSpec(
        num_scalar_prefetch=2,
        grid=(nt,),
        in_specs=[pl.BlockSpec((1, 1, tm * TOP_K), lambda i, e, nb: (i, 0, 0), memory_space=pltpu.SMEM),
                  pl.BlockSpec((tm, half), lambda i, e, nb: (i, 0))],
        out_specs=pl.BlockSpec(memory_space=pl.ANY),
        scratch_shapes=[pltpu.VMEM((2, tm, half), jnp.uint32), pltpu.VMEM((MOE_ROWS, half), jnp.uint32),
                        pltpu.SemaphoreType.DMA((2,)), pltpu.SemaphoreType.DMA(())],
    )
    return pl.pallas_call(
        _dispatch_body,
        out_shape=jax.ShapeDtypeStruct((n_rows, half), jnp.uint32),
        grid_spec=grid_spec,
        compiler_params=_params(("arbitrary",)),
        name="moe_dispatch",
    )(seg_end, n_used, dest3, h_packed)


def _expert_body(be_ref, nb_ref, first_ref, next_ref, slot_ref, x_ref, wg_hbm, wu_hbm, wd_hbm, o_ref,
                 wgu_buf, wd_buf, sem):
    b = pl.program_id(0)
    nb = nb_ref[0]
    de = wd_buf.shape[1]

    def weight_copies(e, slot):
        return [pltpu.make_async_copy(wg_hbm.at[e], wgu_buf.at[slot, :, pl.ds(0, de)], sem.at[slot]),
                pltpu.make_async_copy(wu_hbm.at[e], wgu_buf.at[slot, :, pl.ds(de, de)], sem.at[slot]),
                pltpu.make_async_copy(wd_hbm.at[e], wd_buf.at[slot], sem.at[slot])]

    @pl.when((b == 0) & (nb > 0))
    def _():
        for cp in weight_copies(be_ref[0], 0):
            cp.start(priority=1)

    @pl.when(b < nb)
    def _():
        slot = slot_ref[b]

        @pl.when(first_ref[b] == 1)
        def _():
            for cp in weight_copies(0, slot):
                cp.wait()

            @pl.when(next_ref[b] >= 0)
            def _():
                for cp in weight_copies(next_ref[b], 1 - slot):
                    cp.start(priority=1)

        lo, hi = _unpack_halves(x_ref[...])
        x = jnp.concatenate([lo.astype(BF16), hi.astype(BF16)], axis=1)
        gu = _even_matmul(x, wgu_buf.at[slot])
        act = (jax.nn.silu(gu[:, :de]) * gu[:, de:]).astype(BF16)
        o_ref[...] = _pack_halves(_dot(act, wd_buf[slot].astype(BF16)))

    @pl.when(b >= nb)
    def _():
        o_ref[...] = jnp.zeros(o_ref.shape, o_ref.dtype)


def _routed_experts(x_sorted, blk_expert, n_used, blk_first, blk_next, blk_slot, w_gate, w_up, w_down):
    n_exp, d, de = w_gate.shape
    rows = MOE_ROWS
    nb = x_sorted.shape[0] // rows

    def last_used(b, be, nbr, *_):
        return (jnp.maximum(jnp.minimum(b, nbr[0] - 1), 0), 0)

    any_space = pl.BlockSpec(memory_space=pl.ANY)
    grid_spec = pltpu.PrefetchScalarGridSpec(
        num_scalar_prefetch=5,
        grid=(nb,),
        in_specs=[pl.BlockSpec((rows, d // 2), last_used), any_space, any_space, any_space],
        out_specs=pl.BlockSpec((rows, d // 2), lambda b, *_: (b, 0)),
        scratch_shapes=[pltpu.VMEM((2, d, 2 * de), F32), pltpu.VMEM((2, de, d), F32),
                        pltpu.SemaphoreType.DMA((2,))],
    )
    return pl.pallas_call(
        _expert_body,
        out_shape=jax.ShapeDtypeStruct((nb * rows, d // 2), jnp.uint32),
        grid_spec=grid_spec,
        compiler_params=_params(("arbitrary",)),
        name="routed_experts",
    )(blk_expert, n_used, blk_first, blk_next, blk_slot, x_sorted, w_gate, w_up, w_down)


def _combine_body(dst_ref, dstn_ref, y_hbm, gate_ref, base_ref, g2_ref, fg_ref, o_ref, ybuf, sem):
    i = pl.program_id(0)
    n = pl.num_programs(0)
    tm = base_ref.shape[0]

    def issue(dref, slot):
        def body(g, carry):
            t0 = pl.multiple_of(g * SUBLANES, SUBLANES)
            for j in range(SUBLANES):
                for kk in range(TOP_K):
                    pltpu.make_async_copy(y_hbm.at[pl.ds(dref[0, 0, (t0 + j) * TOP_K + kk], 1), :],
                                          ybuf.at[slot, kk, pl.ds(t0 + j, 1), :], sem.at[slot]).start()
            return carry
        lax.fori_loop(0, tm // SUBLANES, body, 0)

    @pl.when(i == 0)
    def _():
        issue(dst_ref, 0)

    @pl.when(i + 1 < n)
    def _():
        issue(dstn_ref, (i + 1) % 2)

    slot = i % 2
    for kk in range(TOP_K):
        pltpu.make_async_copy(y_hbm.at[pl.ds(0, tm), :], ybuf.at[slot, kk], sem.at[slot]).wait()
    half = ybuf.shape[-1]
    r_lo = r_hi = None
    for kk in range(TOP_K):
        lo, hi = _unpack_halves(ybuf[slot, kk])
        gk = gate_ref[:, kk:kk + 1]
        r_lo = gk * lo if r_lo is None else r_lo + gk * lo
        r_hi = gk * hi if r_hi is None else r_hi + gk * hi
    x_lo = base_ref[:, :half] + g2_ref[:, :half] * r_lo
    x_hi = base_ref[:, half:] + g2_ref[:, half:] * r_hi
    ssq = jnp.sum(x_lo * x_lo, axis=-1, keepdims=True) + jnp.sum(x_hi * x_hi, axis=-1, keepdims=True)
    inv = lax.rsqrt(ssq / (2 * half) + NORM_EPS)
    o_ref[:, :half] = x_lo * inv * fg_ref[:, :half]
    o_ref[:, half:] = x_hi * inv * fg_ref[:, half:]


def _combine(y_sorted, dest, gate, base, g2, final_g):
    t, d = base.shape
    tm = min(COMBINE_TOKENS, t)
    nt = t // tm
    dest3 = dest.reshape(nt, 1, tm * TOP_K)
    n_exp = gate.shape[1]
    vec = pl.BlockSpec((1, d), lambda i: (0, 0))
    return pl.pallas_call(
        _combine_body,
        out_shape=jax.ShapeDtypeStruct((t, d), F32),
        grid=(nt,),
        in_specs=[pl.BlockSpec((1, 1, tm * TOP_K), lambda i: (i, 0, 0), memory_space=pltpu.SMEM),
                  pl.BlockSpec((1, 1, tm * TOP_K), lambda i: (jnp.minimum(i + 1, nt - 1), 0, 0),
                               memory_space=pltpu.SMEM),
                  pl.BlockSpec(memory_space=pl.ANY),
                  pl.BlockSpec((tm, n_exp), lambda i: (i, 0)),
                  pl.BlockSpec((tm, d), lambda i: (i, 0)), vec, vec],
        out_specs=pl.BlockSpec((tm, d), lambda i: (i, 0)),
        scratch_shapes=[pltpu.VMEM((2, TOP_K, tm, d // 2), jnp.uint32), pltpu.SemaphoreType.DMA((2,))],
        compiler_params=_params(("arbitrary",)),
        name="moe_combine",
    )(dest3, dest3, y_sorted, gate, base, g2, final_g.reshape(1, -1))


def _segment_tables(counts, n_tok):
    n_exp = counts.shape[0]
    rows = MOE_ROWS
    padded = (counts + rows - 1) // rows * rows
    ends = jnp.cumsum(padded)
    starts = ends - padded
    nb = n_tok * TOP_K // rows + n_exp + 1
    first_row = jnp.arange(nb, dtype=jnp.int32) * rows
    blk_expert = jnp.sum((ends[None, :] <= first_row[:, None]).astype(jnp.int32), axis=1)
    blk_expert = jnp.minimum(blk_expert, n_exp - 1)
    n_used = ends[-1] // rows
    blk = jnp.arange(nb, dtype=jnp.int32)
    prev_expert = jnp.concatenate([jnp.full((1,), -1, jnp.int32), blk_expert[:-1]])
    blk_first = ((blk_expert != prev_expert) & (blk < n_used)).astype(jnp.int32)
    blk_slot = (jnp.cumsum(blk_first) - 1) % 2
    after = ends[blk_expert] // rows
    blk_next = jnp.where(after < n_used, blk_expert[jnp.minimum(after, nb - 1)], -1)
    tables = [t.astype(jnp.int32) for t in (blk_expert, n_used.reshape(1), blk_first, blk_next, blk_slot)]
    return starts.astype(jnp.int32), (starts + counts).astype(jnp.int32), tables, nb * rows


def _layer(x, c, mod_w, mod_b, norm1_g, norm2_g, w_in, w_out, a_ln_g, a_ln_b, a_spatial_w, a_spatial_b,
           b_shift_mu, b_decay_up, b_decay_base, b_iclr_up, b_iclr_base, b_gate_up, b_kk_scale,
           b_ka_scale, b_bonus, b_gn_g, b_gn_b, router_w, router_bias, exp_w_gate, exp_w_up,
           exp_w_down, sh_w_gate, sh_w_up, sh_w_down):
    t, d = x.shape
    a_width = a_ln_g.shape[0]
    bw = b_decay_base.shape[0]
    lora_w = 512
    lora_col = 2 * a_width + 3 * bw
    assert lora_col % lora_w == 0 and w_in.shape[1] - lora_col <= lora_w

    mod = _mod_vector(c, mod_w, mod_b)
    sh1, sc1, g1, sh2, sc2, g2 = [mod[:, i * d:(i + 1) * d] for i in range(6)]

    h1 = _normmod(x, norm1_g.reshape(1, d), sh1, sc1)
    proj = _in_projection(h1, w_in.T, tn=lora_w)
    ya = _mixer_a(proj, a_ln_g, a_ln_b, a_spatial_w, a_spatial_b, a_width)
    yb = _mixer_b(proj, 2 * a_width, bw, lora_col, lora_w, b_shift_mu, b_decay_up, b_decay_base, b_iclr_up,
                  b_iclr_base, b_gate_up, b_kk_scale, b_ka_scale, b_bonus.reshape(-1), b_gn_g, b_gn_b)
    x1 = _out_projection(ya, yb, w_out, x, g1)

    n2 = norm2_g.reshape(1, d)
    top_e, gate, pos_sel, counts, h_packed, base = _router(
        x1, n2, sh2, sc2, router_w, router_bias, sh_w_gate, sh_w_up, sh_w_down, g2)
    starts, seg_end, blk_tables, n_rows = _segment_tables(counts[0], t)
    dest = _dest_rows(starts, top_e, pos_sel)[:, :TOP_K]
    x_sorted = _dispatch(h_packed, seg_end, blk_tables[1], dest, n_rows)
    y_sorted = _routed_experts(x_sorted, *blk_tables, exp_w_gate, exp_w_up, exp_w_down)
    return base, g2, y_sorted, dest, gate


def kernel(x, c, mod_w, mod_b, norm1_g, norm2_g, w_in, w_out, a_ln_g, a_ln_b, a_spatial_w, a_spatial_b, b_shift_mu, b_decay_up, b_decay_base, b_iclr_up, b_iclr_base, b_gate_up, b_kk_scale, b_ka_scale, b_bonus, b_gn_g, b_gn_b, router_w, router_bias, exp_w_gate, exp_w_up, exp_w_down, sh_w_gate, sh_w_up, sh_w_down, final_g):
    batch, seq, d = x.shape
    assert batch == 1 and mod_w.shape[0] == 1, "single sequence, single layer"
    layer = [p[0] for p in (mod_w, mod_b, norm1_g, norm2_g, w_in, w_out, a_ln_g, a_ln_b, a_spatial_w,
                            a_spatial_b, b_shift_mu, b_decay_up, b_decay_base, b_iclr_up, b_iclr_base,
                            b_gate_up, b_kk_scale, b_ka_scale, b_bonus, b_gn_g, b_gn_b, router_w,
                            router_bias, exp_w_gate, exp_w_up, exp_w_down, sh_w_gate, sh_w_up, sh_w_down)]
    base, g2, y_sorted, dest, gate = _layer(x[0], c, *layer)
    out = _combine(y_sorted, dest, gate, base, g2, final_g)
    return out.reshape(batch, seq, d)
```

```python
import functools

import jax
import jax.numpy as jnp
from jax import lax
from jax.experimental import pallas as pl
from jax.experimental.pallas import tpu as pltpu

A_GROUP_DIM = 128
B_HEAD_DIM = 64
DECAY_LORA = 96
ICLR_LORA = 96
GATE_LORA = 256
TOP_K = 8
N_GROUPS = 8
TOPK_GROUPS = 4
ROUTED_SCALE = 2.5
NORM_EPS = 1e-6
LN_EPS = 1e-5
GN_EPS = 64e-5
KK_EPS = 1e-12

LANES = 128
SUBLANES = 8
MXU_WIDTH = 256
VMEM_LIMIT_BYTES = 58 * 1024 * 1024

ROW_TILE = 256
MATMUL_ROWS = 1024
MATMUL_COLS = 512
SCAN_CHUNK = 64
MOE_ROWS = 256
COMBINE_TOKENS = 128

F32 = jnp.float32
BF16 = jnp.bfloat16


def _params(sem):
    return pltpu.CompilerParams(dimension_semantics=sem, vmem_limit_bytes=VMEM_LIMIT_BYTES)


def _dot(a, b):
    return jnp.dot(a, b, preferred_element_type=F32)


def _even_matmul(x, w_ref):
    hm = x.shape[0] // 2
    tiles = w_ref.shape[1] // MXU_WIDTH
    rows = [jnp.concatenate([_dot(x[r * hm:(r + 1) * hm],
                                  w_ref[:, c * MXU_WIDTH:(c + 1) * MXU_WIDTH].astype(BF16))
                             for c in range(tiles)], axis=1) for r in range(2)]
    return jnp.concatenate(rows, axis=0)


def _mod_body(c_ref, w_ref, b_ref, o_ref, s_ref):
    @pl.when(pl.program_id(0) == 0)
    def _():
        c = c_ref[...]
        s_ref[...] = jnp.broadcast_to(c * jax.nn.sigmoid(c), s_ref.shape)

    s = s_ref[...]
    cols = [jnp.sum(w_ref[:, j * LANES:(j + 1) * LANES] * s, axis=0, keepdims=True)
            for j in range(w_ref.shape[1] // LANES)]
    o_ref[...] = jnp.concatenate(cols, axis=1) + b_ref[...]


def _mod_vector(c, mod_w, mod_b, tn=MATMUL_COLS):
    d, n = mod_w.shape
    return pl.pallas_call(
        _mod_body,
        out_shape=jax.ShapeDtypeStruct((1, n), F32),
        grid=(n // tn,),
        in_specs=[pl.BlockSpec((d, 1), lambda j: (0, 0)),
                  pl.BlockSpec((d, tn), lambda j: (0, j)),
                  pl.BlockSpec((1, tn), lambda j: (0, j))],
        out_specs=pl.BlockSpec((1, tn), lambda j: (0, j)),
        scratch_shapes=[pltpu.VMEM((d, LANES), F32)],
        compiler_params=_params(("arbitrary",)),
        name="mod_vector",
    )(c.reshape(d, 1), mod_w, mod_b.reshape(1, n))


def _norm_modulate(x, g, shift, scale):
    ms = jnp.mean(x * x, axis=-1, keepdims=True)
    y = x * lax.rsqrt(ms + NORM_EPS) * g
    return y * (1.0 + scale) + shift


def _normmod_body(x_ref, g_ref, sh_ref, sc_ref, o_ref):
    o_ref[...] = _norm_modulate(x_ref[...], g_ref[...], sh_ref[...], sc_ref[...]).astype(o_ref.dtype)


def _normmod(x, g, shift, scale, tm=ROW_TILE):
    t, d = x.shape
    tm = min(tm, t)
    row = pl.BlockSpec((tm, d), lambda i: (i, 0))
    vec = pl.BlockSpec((1, d), lambda i: (0, 0))
    return pl.pallas_call(
        _normmod_body,
        out_shape=jax.ShapeDtypeStruct((t, d), BF16),
        grid=(t // tm,),
        in_specs=[row, vec, vec, vec],
        out_specs=row,
        compiler_params=_params(("arbitrary",)),
        name="norm_modulate",
    )(x, g, shift, scale)


def _proj_body(h_ref, wt_ref, o_ref, wb_ref, *, n_valid):
    @pl.when(pl.program_id(1) == 0)
    def _():
        wt = wt_ref[...]
        col = pl.program_id(0) * wt.shape[0] + lax.broadcasted_iota(jnp.int32, wt.shape, 0)
        wb_ref[...] = jnp.where(col < n_valid, wt, 0.0).T.astype(wb_ref.dtype)

    o_ref[...] = _dot(h_ref[...], wb_ref[...])


def _in_projection(h, w_in_t, tm=MATMUL_ROWS, tn=MATMUL_COLS):
    t, d = h.shape
    n = w_in_t.shape[0]
    tm = min(tm, t)
    n_pad = pl.cdiv(n, tn) * tn
    return pl.pallas_call(
        functools.partial(_proj_body, n_valid=n),
        out_shape=jax.ShapeDtypeStruct((t, n_pad), F32),
        grid=(n_pad // tn, t // tm),
        in_specs=[pl.BlockSpec((tm, d), lambda j, i: (i, 0)),
                  pl.BlockSpec((tn, d), lambda j, i: (j, 0))],
        out_specs=pl.BlockSpec((tm, tn), lambda j, i: (i, j)),
        scratch_shapes=[pltpu.VMEM((d, tn), BF16)],
        compiler_params=_params(("arbitrary", "arbitrary")),
        name="in_projection",
    )(h, w_in_t)


def _gelu(x):
    return 0.5 * x * (1.0 + lax.erf(x * (2.0 ** -0.5)))


def _mixer_a_body(u_ref, v_ref, lng_ref, lnb_ref, ws_ref, sb_ref, o_ref):
    tm = u_ref.shape[0]
    gd = A_GROUP_DIM
    row = lax.broadcasted_iota(jnp.int32, (gd, gd), 0)
    col = lax.broadcasted_iota(jnp.int32, (gd, gd), 1)
    causal = col <= row
    for g in range(u_ref.shape[1] // gd):
        sl = slice(g * gd, (g + 1) * gd)
        u = _gelu(u_ref[:, sl])
        v = _gelu(v_ref[:, sl])
        mu = jnp.mean(v, axis=-1, keepdims=True)
        dv = v - mu
        var = jnp.mean(dv * dv, axis=-1, keepdims=True)
        vn = ((dv * lax.rsqrt(var + LN_EPS)) * lng_ref[:, sl] + lnb_ref[:, sl]).astype(BF16)
        w = jnp.where(causal, ws_ref[g], 0.0).astype(BF16)
        for n in range(tm // gd):
            rs = slice(n * gd, (n + 1) * gd)
            mixed = _dot(w, vn[rs]) + sb_ref[g]
            o_ref[rs, sl] = (u[rs] * mixed).astype(o_ref.dtype)


def _mixer_a(proj, ln_g, ln_b, spatial_w, spatial_b, a_width, tm=ROW_TILE):
    t = proj.shape[0]
    tm = min(tm, t)
    groups = a_width // A_GROUP_DIM
    sb = jnp.broadcast_to(spatial_b[:, :, None], (groups, A_GROUP_DIM, A_GROUP_DIM))
    vec = pl.BlockSpec((1, a_width), lambda i: (0, 0))
    mat = pl.BlockSpec((groups, A_GROUP_DIM, A_GROUP_DIM), lambda i: (0, 0, 0))
    return pl.pallas_call(
        _mixer_a_body,
        out_shape=jax.ShapeDtypeStruct((t, a_width), BF16),
        grid=(t // tm,),
        in_specs=[pl.BlockSpec((tm, a_width), lambda i: (i, 0)),
                  pl.BlockSpec((tm, a_width), lambda i: (i, 1)),
                  vec, vec, mat, mat],
        out_specs=pl.BlockSpec((tm, a_width), lambda i: (i, 0)),
        compiler_params=_params(("arbitrary",)),
        name="mixer_a",
    )(proj, proj, ln_g.reshape(1, -1), ln_b.reshape(1, -1), spatial_w, sb)


def _split_bf16(x, parts):
    out = []
    for _ in range(parts - 1):
        hi = x.astype(BF16)
        out.append(hi)
        x = x - hi.astype(F32)
    out.append(x.astype(BF16))
    return out


def _head_sums(x):
    head0 = lax.broadcasted_iota(jnp.int32, (x.shape[0], LANES), 1) < B_HEAD_DIM
    parts = []
    for s in range(x.shape[1] // LANES):
        xs = x[:, s * LANES:(s + 1) * LANES]
        s0 = jnp.sum(jnp.where(head0, xs, 0.0), axis=-1, keepdims=True)
        s1 = jnp.sum(jnp.where(head0, 0.0, xs), axis=-1, keepdims=True)
        parts.append(jnp.where(head0, s0, s1))
    return jnp.concatenate(parts, axis=1)


def _softplus(x):
    return jnp.maximum(x, 0.0) + jnp.log1p(jnp.exp(-jnp.abs(x)))


def _scan_chunk(r_in, w_in, k_in, v_in, a_in, b_in, s_ref):
    c = SCAN_CHUNK
    ti = lax.broadcasted_iota(jnp.int32, (c, c), 0)
    tj = lax.broadcasted_iota(jnp.int32, (c, c), 1)
    tri_incl = (tj <= ti).astype(BF16)
    head0 = lax.broadcasted_iota(jnp.int32, (c, LANES), 1) < B_HEAD_DIM
    i2 = lax.broadcasted_iota(jnp.int32, (2 * c, 2 * c), 0)
    j2 = lax.broadcasted_iota(jnp.int32, (2 * c, 2 * c), 1)
    eye = (i2 == j2).astype(F32)
    strict_bd = (i2 // c == j2 // c) & (j2 < i2)
    ic = lax.broadcasted_iota(jnp.int32, (c, 2 * c), 0)
    jc = lax.broadcasted_iota(jnp.int32, (c, 2 * c), 1)
    incl_cat = (jc % c) <= ic
    contract0 = (((0,), (0,)), ((), ()))
    contract1 = (((1,), (1,)), ((), ()))

    def stack(x):
        return jnp.concatenate([jnp.where(head0, x, 0.0), jnp.where(head0, 0.0, x)], axis=0)

    def cat(xs, axis):
        return jnp.concatenate(xs, axis=axis)

    pairs = range(r_in.shape[1] // LANES)
    sls = [slice(p * LANES, (p + 1) * LANES) for p in pairs]
    w = [w_in[:, sl] for sl in sls]
    cum3 = [_dot(tri_incl, cat(_split_bf16(w[p], 3), 1)) for p in pairs]
    cum = [x[:, :LANES] + x[:, LANES:2 * LANES] + x[:, 2 * LANES:] for x in cum3]
    cum_last = [x[c - 1:c, :] for x in cum]
    e_neg = [jnp.exp(-cum[p]) for p in pairs]
    e_tail = [jnp.exp(cum_last[p] - cum[p]) for p in pairs]
    r_t = [r_in[:, sls[p]] * jnp.exp(cum[p]) for p in pairs]
    a_st = [stack(a_in[:, sls[p]] * jnp.exp(cum[p] - w[p])) for p in pairs]
    g_c = [jnp.exp(x) for x in cum_last]
    b = [b_in[:, sl] for sl in sls]
    k = [k_in[:, sl] for sl in sls]
    v_bd = [stack(v_in[:, sl]).astype(BF16) for sl in sls]
    lhs = [cat([a_st[p], r_t[p]], 0).astype(BF16) for p in pairs]
    rhs = [cat([stack(b[p] * e_neg[p]), stack(k[p] * e_neg[p])], 0).astype(BF16) for p in pairs]
    big = [lax.dot_general(lhs[p], rhs[p], contract1, preferred_element_type=F32) for p in pairs]
    a_ak = [jnp.where(strict_bd, x[:2 * c, 2 * c:], 0.0).astype(BF16) for x in big]
    a_rb = [jnp.where(incl_cat, x[2 * c:, :2 * c], 0.0).astype(BF16) for x in big]
    a_rk = [jnp.where(incl_cat, x[2 * c:, 2 * c:], 0.0).astype(BF16) for x in big]
    akv = [_dot(a_ak[p], v_bd[p]) for p in pairs]

    x = [jnp.where(strict_bd, y[:2 * c, :2 * c], 0.0) for y in big]
    t_inv = [eye + y for y in x]
    xb = [y.astype(BF16) for y in x]
    x = [_dot(y, y) for y in xb]
    span = 2
    while 2 * span < c:
        prod = [_dot(x[p].astype(BF16), cat([x[p], t_inv[p]], 1).astype(BF16)) for p in pairs]
        x = [y[:, :2 * c] for y in prod]
        t_inv = [t_inv[p] + prod[p][:, 2 * c:] for p in pairs]
        span *= 2
    t_inv = [t_inv[p] + _dot(x[p].astype(BF16), t_inv[p].astype(BF16)) for p in pairs]

    tub = [_dot(t_inv[p].astype(BF16), cat([a_st[p], akv[p]], 1).astype(BF16)).astype(BF16)
           for p in pairs]
    ry = [_dot(a_rb[p], tub[p]) for p in pairs]
    y0 = [ry[p][:, LANES:] + _dot(a_rk[p], v_bd[p]) for p in pairs]
    r_hat = [r_t[p] + ry[p][:, :LANES] for p in pairs]
    bg = [stack(b[p] * e_tail[p]).astype(BF16) for p in pairs]
    kg = [stack(k[p] * e_tail[p]).astype(BF16) for p in pairs]
    mn = [lax.dot_general(bg[p], tub[p], contract0, preferred_element_type=F32) for p in pairs]
    n = [mn[p][:, LANES:] + lax.dot_general(kg[p], v_bd[p], contract0, preferred_element_type=F32)
         for p in pairs]

    s0 = [s_ref[p] for p in pairs]
    fin = [_dot(cat([r_hat[p], mn[p][:, :LANES]], 0).astype(BF16), cat(_split_bf16(s0[p], 2), 1))
           for p in pairs]
    g_col = [jnp.sum(eye * g_c[p], axis=1, keepdims=True) for p in pairs]
    for p in pairs:
        s_ref[p] = g_col[p] * s0[p] + (fin[p][c:, :LANES] + fin[p][c:, LANES:]) + n[p]
    return cat([fin[p][:c, :LANES] + fin[p][:c, LANES:] + y0[p] for p in pairs], 1)


def _mixer_b_body(r_ref, k_ref, v_ref, l_ref, mur_ref, muk_ref, muv_ref, mul_ref, dup_ref, iup_ref, gup_ref,
                  dbase_ref, ibase_ref, kks_ref, kas_ref, bonus_ref, gng_ref, gnb_ref,
                  o_ref, s_ref, pr_ref, pk_ref, pv_ref, pl_ref):
    @pl.when(pl.program_id(0) == 0)
    def _():
        for ref in (s_ref, pr_ref, pk_ref, pv_ref, pl_ref):
            ref[...] = jnp.zeros(ref.shape, F32)

    def shifted(p_ref, prev_ref, mu_ref):
        p = p_ref[...]
        rowid = lax.broadcasted_iota(jnp.int32, p.shape, 0)
        prev = jnp.where(rowid == 0, prev_ref[...], pltpu.roll(p, 1, 0))
        prev_ref[...] = p[p.shape[0] - 1:, :]
        return p + mu_ref[...] * (prev - p)

    r = shifted(r_ref, pr_ref, mur_ref)
    k = shifted(k_ref, pk_ref, muk_ref)
    v = shifted(v_ref, pv_ref, muv_ref)
    lora = shifted(l_ref, pl_ref, mul_ref)

    dec_in = dbase_ref[...] + _dot(jnp.tanh(lora).astype(BF16), dup_ref[...])
    w_log = -_softplus(-dec_in) - 0.5
    log_decay = -jnp.exp(w_log)
    a = jax.nn.sigmoid(ibase_ref[...] + _dot(lora.astype(BF16), iup_ref[...]))
    g = _dot(jax.nn.sigmoid(lora).astype(BF16), gup_ref[...])

    kk = k * kks_ref[...]
    kk = kk / jnp.maximum(jnp.sqrt(_head_sums(kk * kk)), KK_EPS)
    k = k * (1.0 + (a - 1.0) * kas_ref[...])
    bonus_v = _head_sums(r * k * bonus_ref[...]) * v

    y = _scan_chunk(r, log_decay, k, v, -kk, kk * a, s_ref)

    inv_n = 1.0 / B_HEAD_DIM
    mu = _head_sums(y) * inv_n
    d = y - mu
    var = _head_sums(d * d) * inv_n
    yn = (d * lax.rsqrt(var + GN_EPS)) * gng_ref[...] + gnb_ref[...]
    o_ref[...] = ((yn + bonus_v) * g).astype(o_ref.dtype)


def _mixer_b(proj, a_cols, bw, lora_col, lora_w, shift_mu, decay_up, decay_base, iclr_up, iclr_base,
             gate_up, kk_scale, ka_scale, bonus, gn_g, gn_b):
    t = proj.shape[0]
    c = SCAN_CHUNK
    cb = a_cols // bw
    lb = lora_col // lora_w
    n_lora = DECAY_LORA + ICLR_LORA + GATE_LORA

    def pad_rows(w, start):
        return jnp.zeros((lora_w, bw), F32).at[start:start + w.shape[0]].set(w).astype(BF16)

    dup = pad_rows(decay_up, 0)
    iup = pad_rows(iclr_up, DECAY_LORA)
    gup = pad_rows(gate_up, DECAY_LORA + ICLR_LORA)
    mu = shift_mu.reshape(1, -1)
    mu_l = jnp.zeros((1, lora_w), F32).at[:, :n_lora].set(mu[:, 3 * bw:])

    def cur(width, blk):
        return pl.BlockSpec((c, width), lambda i: (i, blk))

    vec = pl.BlockSpec((1, bw), lambda i: (0, 0))
    vecl = pl.BlockSpec((1, lora_w), lambda i: (0, 0))
    up = pl.BlockSpec((lora_w, bw), lambda i: (0, 0))
    row = lambda x: x.reshape(1, -1)
    return pl.pallas_call(
        _mixer_b_body,
        out_shape=jax.ShapeDtypeStruct((t, bw), BF16),
        grid=(t // c,),
        in_specs=[cur(bw, cb), cur(bw, cb + 1), cur(bw, cb + 2), cur(lora_w, lb),
                  vec, vec, vec, vecl, up, up, up, vec, vec, vec, vec, vec, vec, vec],
        out_specs=pl.BlockSpec((c, bw), lambda i: (i, 0)),
        scratch_shapes=[pltpu.VMEM((bw // LANES, LANES, LANES), F32),
                        pltpu.VMEM((1, bw), F32), pltpu.VMEM((1, bw), F32), pltpu.VMEM((1, bw), F32),
                        pltpu.VMEM((1, lora_w), F32)],
        compiler_params=_params(("arbitrary",)),
        name="mixer_b",
    )(proj, proj, proj, proj, mu[:, :bw], mu[:, bw:2 * bw], mu[:, 2 * bw:3 * bw], mu_l, dup, iup, gup,
      row(decay_base), row(iclr_base), row(kk_scale), row(ka_scale), row(bonus), row(gn_g), row(gn_b))


def _out_proj_body(ya_ref, yb_ref, w_ref, x_ref, g1_ref, o_ref, wb_ref):
    @pl.when(pl.program_id(1) == 0)
    def _():
        wb_ref[...] = w_ref[...].astype(wb_ref.dtype)

    half = ya_ref.shape[1]
    mix = _dot(ya_ref[...], wb_ref[:half, :]) + _dot(yb_ref[...], wb_ref[half:, :])
    o_ref[...] = x_ref[...] + g1_ref[...] * mix


def _out_projection(ya, yb, w_out, x, g1, tm=MATMUL_ROWS, tn=MATMUL_COLS):
    t, d = x.shape
    tm = min(tm, t)
    half = ya.shape[1]
    act = pl.BlockSpec((tm, half), lambda j, i: (i, 0))
    tile = pl.BlockSpec((tm, tn), lambda j, i: (i, j))
    return pl.pallas_call(
        _out_proj_body,
        out_shape=jax.ShapeDtypeStruct((t, d), F32),
        grid=(d // tn, t // tm),
        in_specs=[act, act, pl.BlockSpec((2 * half, tn), lambda j, i: (0, j)), tile,
                  pl.BlockSpec((1, tn), lambda j, i: (0, j))],
        out_specs=tile,
        scratch_shapes=[pltpu.VMEM((2 * half, tn), BF16)],
        compiler_params=_params(("arbitrary", "arbitrary")),
        name="out_projection",
    )(ya, yb, w_out, x, g1)


def _router_body(x_ref, g_ref, sh_ref, sc_ref, rw_ref, rb_ref, sgu_ref, sd_ref, g2_ref,
                 te_ref, gate_ref, pos_ref, cnt_ref, hp_ref, base_ref, carry_ref):
    @pl.when(pl.program_id(0) == 0)
    def _():
        carry_ref[...] = jnp.zeros(carry_ref.shape, F32)

    x = x_ref[...]
    h = _norm_modulate(x, g_ref[...], sh_ref[...], sc_ref[...])
    hp_ref[...] = _pack_halves(h)
    ds = sd_ref.shape[0]
    gu = _even_matmul(h.astype(BF16), sgu_ref)
    act = (jax.nn.silu(gu[:, :ds]) * gu[:, ds:]).astype(BF16)
    base_ref[...] = x + g2_ref[...] * _dot(act, sd_ref[...])
    tm = x.shape[0]
    n_exp = rw_ref.shape[1]
    per_group = n_exp // N_GROUPS
    neg_inf = -jnp.inf

    h_hi, h_lo = _split_bf16(h, 2)
    w_hi, w_lo = _split_bf16(rw_ref[...], 2)
    scores = jax.nn.sigmoid(_dot(h_hi, w_hi) + (_dot(h_hi, w_lo) + _dot(h_lo, w_hi)))
    sel = scores + rb_ref[...]
    lane = lax.broadcasted_iota(jnp.int32, (tm, n_exp), 1)
    gid = lane // per_group

    gscore = jnp.zeros((tm, n_exp), F32)
    for gi in range(N_GROUPS):
        in_g = gid == gi
        m1 = jnp.max(jnp.where(in_g, sel, neg_inf), axis=-1, keepdims=True)
        i1 = jnp.min(jnp.where(in_g & (sel == m1), lane, n_exp), axis=-1, keepdims=True)
        m2 = jnp.max(jnp.where(in_g & (lane != i1), sel, neg_inf), axis=-1, keepdims=True)
        gscore = jnp.where(in_g, m1 + m2, gscore)
    beaten = jnp.zeros((tm, n_exp), jnp.int32)
    for s in range(1, N_GROUPS):
        other = pltpu.roll(gscore, s * per_group, 1)
        wins = (other > gscore) | ((other == gscore) & (gid >= s))
        beaten = beaten + wins.astype(jnp.int32)
    allowed = beaten < TOPK_GROUPS

    masked = jnp.where(allowed, sel, neg_inf)
    chosen = jnp.zeros((tm, n_exp), F32)
    idxs, gates = [], []
    for _ in range(TOP_K):
        m = jnp.max(masked, axis=-1, keepdims=True)
        idx = jnp.min(jnp.where(masked == m, lane, n_exp), axis=-1, keepdims=True)
        hit = lane == idx
        gates.append(jnp.sum(jnp.where(hit, scores, 0.0), axis=-1, keepdims=True))
        idxs.append(idx)
        chosen = jnp.where(hit, 1.0, chosen)
        masked = jnp.where(hit, neg_inf, masked)
    gsum = gates[0]
    for gk in gates[1:]:
        gsum = gsum + gk

    ri = lax.broadcasted_iota(jnp.int32, (tm, tm), 0)
    ci = lax.broadcasted_iota(jnp.int32, (tm, tm), 1)
    before = (ci < ri).astype(BF16)
    pos = carry_ref[...] + _dot(before, chosen.astype(BF16))
    carry_ref[...] = carry_ref[...] + jnp.sum(chosen, axis=0, keepdims=True)
    cnt_ref[...] = jnp.broadcast_to(carry_ref[...], cnt_ref.shape).astype(jnp.int32)

    te = jnp.zeros((tm, n_exp), jnp.int32)
    gt = jnp.zeros((tm, n_exp), F32)
    ps = jnp.zeros((tm, n_exp), jnp.int32)
    for kk in range(TOP_K):
        slot = lane == kk
        hit = lane == idxs[kk]
        pk = jnp.sum(jnp.where(hit, pos, 0.0), axis=-1, keepdims=True)
        te = jnp.where(slot, idxs[kk], te)
        gt = jnp.where(slot, gates[kk] / gsum * ROUTED_SCALE, gt)
        ps = jnp.where(slot, pk.astype(jnp.int32), ps)
    te_ref[...] = te
    gate_ref[...] = gt
    pos_ref[...] = ps


def _router(x1, norm_g, shift, scale, router_w, router_bias, sh_w_gate, sh_w_up, sh_w_down, g2, tm=ROW_TILE):
    t, d = x1.shape
    tm = min(tm, t)
    n_exp = router_w.shape[1]
    ds = sh_w_gate.shape[1]
    sgu = jnp.concatenate([sh_w_gate, sh_w_up], axis=1).astype(BF16)
    row = pl.BlockSpec((tm, d), lambda i: (i, 0))
    vec = pl.BlockSpec((1, d), lambda i: (0, 0))
    small = pl.BlockSpec((tm, n_exp), lambda i: (i, 0))
    return pl.pallas_call(
        _router_body,
        out_shape=[jax.ShapeDtypeStruct((t, n_exp), jnp.int32), jax.ShapeDtypeStruct((t, n_exp), F32),
                   jax.ShapeDtypeStruct((t, n_exp), jnp.int32), jax.ShapeDtypeStruct((8, n_exp), jnp.int32),
                   jax.ShapeDtypeStruct((t, d // 2), jnp.uint32), jax.ShapeDtypeStruct((t, d), F32)],
        grid=(t // tm,),
        in_specs=[row, vec, vec, vec,
                  pl.BlockSpec((d, n_exp), lambda i: (0, 0)), pl.BlockSpec((1, n_exp), lambda i: (0, 0)),
                  pl.BlockSpec((d, 2 * ds), lambda i: (0, 0)), pl.BlockSpec((ds, d), lambda i: (0, 0)), vec],
        out_specs=[small, small, small, pl.BlockSpec((8, n_exp), lambda i: (0, 0)),
                   pl.BlockSpec((tm, d // 2), lambda i: (i, 0)), row],
        scratch_shapes=[pltpu.VMEM((1, n_exp), F32)],
        compiler_params=_params(("arbitrary",)),
        name="moe_router",
    )(x1, norm_g, shift, scale, router_w, router_bias.reshape(1, -1), sgu, sh_w_down.astype(BF16), g2)


def _dest_body(starts_ref, te_ref, pos_ref, o_ref):
    te = te_ref[...]
    acc = jnp.zeros(te.shape, jnp.int32)
    for e in range(starts_ref.shape[0]):
        acc = jnp.where(te == e, starts_ref[e], acc)
    o_ref[...] = acc + pos_ref[...]


def _dest_rows(starts, top_e, pos_sel, tm=2 * ROW_TILE):
    t, n_exp = top_e.shape
    tm = min(tm, t)
    blk = pl.BlockSpec((tm, n_exp), lambda i: (i, 0))
    return pl.pallas_call(
        _dest_body,
        out_shape=jax.ShapeDtypeStruct((t, n_exp), jnp.int32),
        grid=(t // tm,),
        in_specs=[pl.BlockSpec(memory_space=pltpu.SMEM), blk, blk],
        out_specs=blk,
        compiler_params=_params(("arbitrary",)),
        name="moe_dest",
    )(starts, top_e, pos_sel)


def _pack_halves(x):
    half = x.shape[1] // 2
    lo = pltpu.bitcast(x[:, :half].astype(BF16).astype(F32), jnp.uint32)
    hi = pltpu.bitcast(x[:, half:].astype(BF16).astype(F32), jnp.uint32)
    return (lo >> 16) | (hi & jnp.uint32(0xFFFF0000))


def _unpack_halves(p):
    lo = pltpu.bitcast(p << 16, F32)
    hi = pltpu.bitcast(p & jnp.uint32(0xFFFF0000), F32)
    return lo, hi


def _dispatch_body(end_ref, nb_ref, dst_ref, hp_ref, xs_hbm, hp_buf, zero_buf, sem, zsem):
    i = pl.program_id(0)
    n = pl.num_programs(0)
    tm = hp_ref.shape[0]
    rows = zero_buf.shape[0]
    n_exp = end_ref.shape[0]
    n_blocks = xs_hbm.shape[0] // rows

    def drain(slot):
        for _ in range(TOP_K):
            pltpu.make_async_copy(hp_buf.at[slot], xs_hbm.at[pl.ds(0, tm), :], sem.at[slot]).wait()

    @pl.when(i == 0)
    def _():
        zero_buf[...] = jnp.zeros(zero_buf.shape, zero_buf.dtype)
        nb = nb_ref[0]

        def zero_rows(start, count):
            first = start if count == 1 else pl.multiple_of(start, 8)
            return pltpu.make_async_copy(zero_buf.at[pl.ds(0, count), :], xs_hbm.at[pl.ds(first, count), :], zsem)

        def issue(start, stop, count):
            def body(j, carry):
                zero_rows(start + j * count, count).start()
                return carry
            n_copies = (stop - start) // count
            lax.fori_loop(0, n_copies, body, 0)
            return n_copies

        def fill(e, carry):
            n1, n8 = carry
            end = end_ref[e]
            aligned = (end + 7) // 8 * 8
            stop = (end + rows - 1) // rows * rows
            return n1 + issue(end, aligned, 1), n8 + issue(aligned, stop, 8)
        n1, n8 = lax.fori_loop(0, n_exp, fill, (0, 0))
        n_tail = issue(nb * rows, n_blocks * rows, rows)

        for count, n_copies in ((1, n1), (8, n8), (rows, n_tail)):
            def wait(j, carry, count=count):
                zero_rows(0, count).wait()
                return carry
            lax.fori_loop(0, n_copies, wait, 0)

    slot = i % 2

    @pl.when(i >= 2)
    def _():
        drain(slot)

    hp_buf[slot] = hp_ref[...]

    def body(g, carry):
        t0 = pl.multiple_of(g * SUBLANES, SUBLANES)
        for j in range(SUBLANES):
            for kk in range(TOP_K):
                pltpu.make_async_copy(hp_buf.at[slot, pl.ds(t0 + j, 1), :],
                                      xs_hbm.at[pl.ds(dst_ref[0, 0, (t0 + j) * TOP_K + kk], 1), :],
                                      sem.at[slot]).start()
        return carry
    lax.fori_loop(0, tm // SUBLANES, body, 0)

    @pl.when(i == n - 1)
    def _():
        drain(slot)

        @pl.when(n > 1)
        def _():
            drain(1 - slot)


def _dispatch(h_packed, seg_end, n_used, dest, n_rows, tm=ROW_TILE):
    t, half = h_packed.shape
    tm = min(tm, t)
    nt = t // tm
    dest3 = dest.reshape(nt, 1, tm * TOP_K)
    grid_spec = pltpu.PrefetchScalarGridSpec(
        num_scalar_prefetch=2,
        grid=(nt,),
        in_specs=[pl.BlockSpec((1, 1, tm * TOP_K), lambda i, e, nb: (i, 0, 0), memory_space=pltpu.SMEM),
                  pl.BlockSpec((tm, half), lambda i, e, nb: (i, 0))],
        out_specs=pl.BlockSpec(memory_space=pl.ANY),
        scratch_shapes=[pltpu.VMEM((2, tm, half), jnp.uint32), pltpu.VMEM((MOE_ROWS, half), jnp.uint32),
                        pltpu.SemaphoreType.DMA((2,)), pltpu.SemaphoreType.DMA(())],
    )
    return pl.pallas_call(
        _dispatch_body,
        out_shape=jax.ShapeDtypeStruct((n_rows, half), jnp.uint32),
        grid_spec=grid_spec,
        compiler_params=_params(("arbitrary",)),
        name="moe_dispatch",
    )(seg_end, n_used, dest3, h_packed)


def _expert_body(be_ref, nb_ref, first_ref, next_ref, slot_ref, x_ref, wg_hbm, wu_hbm, wd_hbm, o_ref,
                 wgu_buf, wd_buf, sem):
    b = pl.program_id(0)
    nb = nb_ref[0]
    de = wd_buf.shape[1]

    def weight_copies(e, slot):
        return [pltpu.make_async_copy(wg_hbm.at[e], wgu_buf.at[slot, :, pl.ds(0, de)], sem.at[slot]),
                pltpu.make_async_copy(wu_hbm.at[e], wgu_buf.at[slot, :, pl.ds(de, de)], sem.at[slot]),
                pltpu.make_async_copy(wd_hbm.at[e], wd_buf.at[slot], sem.at[slot])]

    @pl.when((b == 0) & (nb > 0))
    def _():
        for cp in weight_copies(be_ref[0], 0):
            cp.start(priority=1)

    @pl.when(b < nb)
    def _():
        slot = slot_ref[b]

        @pl.when(first_ref[b] == 1)
        def _():
            for cp in weight_copies(0, slot):
                cp.wait()

            @pl.when(next_ref[b] >= 0)
            def _():
                for cp in weight_copies(next_ref[b], 1 - slot):
                    cp.start(priority=1)

        lo, hi = _unpack_halves(x_ref[...])
        x = jnp.concatenate([lo.astype(BF16), hi.astype(BF16)], axis=1)
        gu = _even_matmul(x, wgu_buf.at[slot])
        act = (jax.nn.silu(gu[:, :de]) * gu[:, de:]).astype(BF16)
        o_ref[...] = _pack_halves(_dot(act, wd_buf[slot].astype(BF16)))

    @pl.when(b >= nb)
    def _():
        o_ref[...] = jnp.zeros(o_ref.shape, o_ref.dtype)


def _routed_experts(x_sorted, blk_expert, n_used, blk_first, blk_next, blk_slot, w_gate, w_up, w_down):
    n_exp, d, de = w_gate.shape
    rows = MOE_ROWS
    nb = x_sorted.shape[0] // rows

    def last_used(b, be, nbr, *_):
        return (jnp.maximum(jnp.minimum(b, nbr[0] - 1), 0), 0)

    any_space = pl.BlockSpec(memory_space=pl.ANY)
    grid_spec = pltpu.PrefetchScalarGridSpec(
        num_scalar_prefetch=5,
        grid=(nb,),
        in_specs=[pl.BlockSpec((rows, d // 2), last_used), any_space, any_space, any_space],
        out_specs=pl.BlockSpec((rows, d // 2), lambda b, *_: (b, 0)),
        scratch_shapes=[pltpu.VMEM((2, d, 2 * de), F32), pltpu.VMEM((2, de, d), F32),
                        pltpu.SemaphoreType.DMA((2,))],
    )
    return pl.pallas_call(
        _expert_body,
        out_shape=jax.ShapeDtypeStruct((nb * rows, d // 2), jnp.uint32),
        grid_spec=grid_spec,
        compiler_params=_params(("arbitrary",)),
        name="routed_experts",
    )(blk_expert, n_used, blk_first, blk_next, blk_slot, x_sorted, w_gate, w_up, w_down)


def _combine_body(dst_ref, dstn_ref, y_hbm, gate_ref, base_ref, g2_ref, fg_ref, o_ref, ybuf, sem):
    i = pl.program_id(0)
    n = pl.num_programs(0)
    tm = base_ref.shape[0]

    def issue(dref, slot):
        def body(g, carry):
            t0 = pl.multiple_of(g * SUBLANES, SUBLANES)
            for j in range(SUBLANES):
                for kk in range(TOP_K):
                    pltpu.make_async_copy(y_hbm.at[pl.ds(dref[0, 0, (t0 + j) * TOP_K + kk], 1), :],
                                          ybuf.at[slot, kk, pl.ds(t0 + j, 1), :], sem.at[slot]).start(priority=kk % 2)
            return carry
        lax.fori_loop(0, tm // SUBLANES, body, 0)

    @pl.when(i == 0)
    def _():
        issue(dst_ref, 0)

    @pl.when(i + 1 < n)
    def _():
        issue(dstn_ref, (i + 1) % 2)

    slot = i % 2
    for kk in range(TOP_K):
        pltpu.make_async_copy(y_hbm.at[pl.ds(0, tm), :], ybuf.at[slot, kk], sem.at[slot]).wait()
    half = ybuf.shape[-1]
    r_lo = r_hi = None
    for kk in range(TOP_K):
        lo, hi = _unpack_halves(ybuf[slot, kk])
        gk = gate_ref[:, kk:kk + 1]
        r_lo = gk * lo if r_lo is None else r_lo + gk * lo
        r_hi = gk * hi if r_hi is None else r_hi + gk * hi
    x_lo = base_ref[:, :half] + g2_ref[:, :half] * r_lo
    x_hi = base_ref[:, half:] + g2_ref[:, half:] * r_hi
    ssq = jnp.sum(x_lo * x_lo, axis=-1, keepdims=True) + jnp.sum(x_hi * x_hi, axis=-1, keepdims=True)
    inv = lax.rsqrt(ssq / (2 * half) + NORM_EPS)
    o_ref[:, :half] = x_lo * inv * fg_ref[:, :half]
    o_ref[:, half:] = x_hi * inv * fg_ref[:, half:]


def _combine(y_sorted, dest, gate, base, g2, final_g):
    t, d = base.shape
    tm = min(COMBINE_TOKENS, t)
    nt = t // tm
    dest3 = dest.reshape(nt, 1, tm * TOP_K)
    n_exp = gate.shape[1]
    vec = pl.BlockSpec((1, d), lambda i: (0, 0))
    return pl.pallas_call(
        _combine_body,
        out_shape=jax.ShapeDtypeStruct((t, d), F32),
        grid=(nt,),
        in_specs=[pl.BlockSpec((1, 1, tm * TOP_K), lambda i: (i, 0, 0), memory_space=pltpu.SMEM),
                  pl.BlockSpec((1, 1, tm * TOP_K), lambda i: (jnp.minimum(i + 1, nt - 1), 0, 0),
                               memory_space=pltpu.SMEM),
                  pl.BlockSpec(memory_space=pl.ANY),
                  pl.BlockSpec((tm, n_exp), lambda i: (i, 0)),
                  pl.BlockSpec((tm, d), lambda i: (i, 0)), vec, vec],
        out_specs=pl.BlockSpec((tm, d), lambda i: (i, 0)),
        scratch_shapes=[pltpu.VMEM((2, TOP_K, tm, d // 2), jnp.uint32), pltpu.SemaphoreType.DMA((2,))],
        compiler_params=_params(("arbitrary",)),
        name="moe_combine",
    )(dest3, dest3, y_sorted, gate, base, g2, final_g.reshape(1, -1))


def _segment_tables(counts, n_tok):
    n_exp = counts.shape[0]
    rows = MOE_ROWS
    padded = (counts + rows - 1) // rows * rows
    ends = jnp.cumsum(padded)
    starts = ends - padded
    nb = n_tok * TOP_K // rows + n_exp + 1
    first_row = jnp.arange(nb, dtype=jnp.int32) * rows
    blk_expert = jnp.sum((ends[None, :] <= first_row[:, None]).astype(jnp.int32), axis=1)
    blk_expert = jnp.minimum(blk_expert, n_exp - 1)
    n_used = ends[-1] // rows
    blk = jnp.arange(nb, dtype=jnp.int32)
    prev_expert = jnp.concatenate([jnp.full((1,), -1, jnp.int32), blk_expert[:-1]])
    blk_first = ((blk_expert != prev_expert) & (blk < n_used)).astype(jnp.int32)
    blk_slot = (jnp.cumsum(blk_first) - 1) % 2
    after = ends[blk_expert] // rows
    blk_next = jnp.where(after < n_used, blk_expert[jnp.minimum(after, nb - 1)], -1)
    tables = [t.astype(jnp.int32) for t in (blk_expert, n_used.reshape(1), blk_first, blk_next, blk_slot)]
    return starts.astype(jnp.int32), (starts + counts).astype(jnp.int32), tables, nb * rows


def _layer(x, c, mod_w, mod_b, norm1_g, norm2_g, w_in, w_out, a_ln_g, a_ln_b, a_spatial_w, a_spatial_b,
           b_shift_mu, b_decay_up, b_decay_base, b_iclr_up, b_iclr_base, b_gate_up, b_kk_scale,
           b_ka_scale, b_bonus, b_gn_g, b_gn_b, router_w, router_bias, exp_w_gate, exp_w_up,
           exp_w_down, sh_w_gate, sh_w_up, sh_w_down):
    t, d = x.shape
    a_width = a_ln_g.shape[0]
    bw = b_decay_base.shape[0]
    lora_w = MATMUL_COLS
    lora_col = 2 * a_width + 3 * bw
    assert lora_col % lora_w == 0 and w_in.shape[1] - lora_col <= lora_w

    mod = _mod_vector(c, mod_w, mod_b)
    sh1, sc1, g1, sh2, sc2, g2 = [mod[:, i * d:(i + 1) * d] for i in range(6)]

    h1 = _normmod(x, norm1_g.reshape(1, d), sh1, sc1)
    proj = _in_projection(h1, w_in.T, tn=lora_w)
    ya = _mixer_a(proj, a_ln_g, a_ln_b, a_spatial_w, a_spatial_b, a_width)
    yb = _mixer_b(proj, 2 * a_width, bw, lora_col, lora_w, b_shift_mu, b_decay_up, b_decay_base, b_iclr_up,
                  b_iclr_base, b_gate_up, b_kk_scale, b_ka_scale, b_bonus.reshape(-1), b_gn_g, b_gn_b)
    x1 = _out_projection(ya, yb, w_out, x, g1)

    n2 = norm2_g.reshape(1, d)
    top_e, gate, pos_sel, counts, h_packed, base = _router(
        x1, n2, sh2, sc2, router_w, router_bias, sh_w_gate, sh_w_up, sh_w_down, g2)
    starts, seg_end, blk_tables, n_rows = _segment_tables(counts[0], t)
    dest = _dest_rows(starts, top_e, pos_sel)[:, :TOP_K]
    x_sorted = _dispatch(h_packed, seg_end, blk_tables[1], dest, n_rows)
    y_sorted = _routed_experts(x_sorted, *blk_tables, exp_w_gate, exp_w_up, exp_w_down)
    return base, g2, y_sorted, dest, gate


def kernel(x, c, mod_w, mod_b, norm1_g, norm2_g, w_in, w_out, a_ln_g, a_ln_b, a_spatial_w, a_spatial_b, b_shift_mu, b_decay_up, b_decay_base, b_iclr_up, b_iclr_base, b_gate_up, b_kk_scale, b_ka_scale, b_bonus, b_gn_g, b_gn_b, router_w, router_bias, exp_w_gate, exp_w_up, exp_w_down, sh_w_gate, sh_w_up, sh_w_down, final_g):
    batch, seq, d = x.shape
    assert batch == 1 and mod_w.shape[0] == 1, "single sequence, single layer"
    layer = [p[0] for p in (mod_w, mod_b, norm1_g, norm2_g, w_in, w_out, a_ln_g, a_ln_b, a_spatial_w,
                            a_spatial_b, b_shift_mu, b_decay_up, b_decay_base, b_iclr_up, b_iclr_base,
                            b_gate_up, b_kk_scale, b_ka_scale, b_bonus, b_gn_g, b_gn_b, router_w,
                            router_bias, exp_w_gate, exp_w_up, exp_w_down, sh_w_gate, sh_w_up, sh_w_down)]
    base, g2, y_sorted, dest, gate = _layer(x[0], c, *layer)
    out = _combine(y_sorted, dest, gate, base, g2, final_g)
    return out.reshape(batch, seq, d)
```

```python
import functools

import jax
import jax.numpy as jnp
from jax import lax
from jax.experimental import pallas as pl
from jax.experimental.pallas import tpu as pltpu

A_GROUP_DIM = 128
B_HEAD_DIM = 64
DECAY_LORA = 96
ICLR_LORA = 96
GATE_LORA = 256
TOP_K = 8
N_GROUPS = 8
TOPK_GROUPS = 4
ROUTED_SCALE = 2.5
NORM_EPS = 1e-6
LN_EPS = 1e-5
GN_EPS = 64e-5
KK_EPS = 1e-12

LANES = 128
SUBLANES = 8
MXU_WIDTH = 256
VMEM_LIMIT_BYTES = 58 * 1024 * 1024

ROW_TILE = 256
MATMUL_ROWS = 1024
MATMUL_COLS = 512
SCAN_CHUNK = 64
MOE_ROWS = 256
COMBINE_TOKENS = 128

F32 = jnp.float32
BF16 = jnp.bfloat16


def _params(sem):
    return pltpu.CompilerParams(dimension_semantics=sem, vmem_limit_bytes=VMEM_LIMIT_BYTES)


def _dot(a, b):
    return jnp.dot(a, b, preferred_element_type=F32)


def _even_matmul(x, w_ref):
    hm = x.shape[0] // 2
    tiles = w_ref.shape[1] // MXU_WIDTH
    rows = [jnp.concatenate([_dot(x[r * hm:(r + 1) * hm],
                                  w_ref[:, c * MXU_WIDTH:(c + 1) * MXU_WIDTH].astype(BF16))
                             for c in range(tiles)], axis=1) for r in range(2)]
    return jnp.concatenate(rows, axis=0)


def _mod_body(c_ref, w_ref, b_ref, o_ref, s_ref):
    @pl.when(pl.program_id(0) == 0)
    def _():
        c = c_ref[...]
        s_ref[...] = jnp.broadcast_to(c * jax.nn.sigmoid(c), s_ref.shape)

    s = s_ref[...]
    cols = [jnp.sum(w_ref[:, j * LANES:(j + 1) * LANES] * s, axis=0, keepdims=True)
            for j in range(w_ref.shape[1] // LANES)]
    o_ref[...] = jnp.concatenate(cols, axis=1) + b_ref[...]


def _mod_vector(c, mod_w, mod_b, tn=MATMUL_COLS):
    d, n = mod_w.shape
    return pl.pallas_call(
        _mod_body,
        out_shape=jax.ShapeDtypeStruct((1, n), F32),
        grid=(n // tn,),
        in_specs=[pl.BlockSpec((d, 1), lambda j: (0, 0)),
                  pl.BlockSpec((d, tn), lambda j: (0, j)),
                  pl.BlockSpec((1, tn), lambda j: (0, j))],
        out_specs=pl.BlockSpec((1, tn), lambda j: (0, j)),
        scratch_shapes=[pltpu.VMEM((d, LANES), F32)],
        compiler_params=_params(("arbitrary",)),
        name="mod_vector",
    )(c.reshape(d, 1), mod_w, mod_b.reshape(1, n))


def _norm_modulate(x, g, shift, scale):
    ms = jnp.mean(x * x, axis=-1, keepdims=True)
    y = x * lax.rsqrt(ms + NORM_EPS) * g
    return y * (1.0 + scale) + shift


def _normmod_body(x_ref, g_ref, sh_ref, sc_ref, o_ref):
    o_ref[...] = _norm_modulate(x_ref[...], g_ref[...], sh_ref[...], sc_ref[...]).astype(o_ref.dtype)


def _normmod(x, g, shift, scale, tm=ROW_TILE):
    t, d = x.shape
    tm = min(tm, t)
    row = pl.BlockSpec((tm, d), lambda i: (i, 0))
    vec = pl.BlockSpec((1, d), lambda i: (0, 0))
    return pl.pallas_call(
        _normmod_body,
        out_shape=jax.ShapeDtypeStruct((t, d), BF16),
        grid=(t // tm,),
        in_specs=[row, vec, vec, vec],
        out_specs=row,
        compiler_params=_params(("arbitrary",)),
        name="norm_modulate",
    )(x, g, shift, scale)


def _proj_body(h_ref, wt_ref, o_ref, wb_ref, *, n_valid):
    @pl.when(pl.program_id(1) == 0)
    def _():
        wt = wt_ref[...]
        col = pl.program_id(0) * wt.shape[0] + lax.broadcasted_iota(jnp.int32, wt.shape, 0)
        wb_ref[...] = jnp.where(col < n_valid, wt, 0.0).T.astype(wb_ref.dtype)

    o_ref[...] = _dot(h_ref[...], wb_ref[...])


def _in_projection(h, w_in_t, tm=MATMUL_ROWS, tn=MATMUL_COLS):
    t, d = h.shape
    n = w_in_t.shape[0]
    tm = min(tm, t)
    n_pad = pl.cdiv(n, tn) * tn
    return pl.pallas_call(
        functools.partial(_proj_body, n_valid=n),
        out_shape=jax.ShapeDtypeStruct((t, n_pad), F32),
        grid=(n_pad // tn, t // tm),
        in_specs=[pl.BlockSpec((tm, d), lambda j, i: (i, 0)),
                  pl.BlockSpec((tn, d), lambda j, i: (j, 0))],
        out_specs=pl.BlockSpec((tm, tn), lambda j, i: (i, j)),
        scratch_shapes=[pltpu.VMEM((d, tn), BF16)],
        compiler_params=_params(("arbitrary", "arbitrary")),
        name="in_projection",
    )(h, w_in_t)


def _gelu(x):
    return 0.5 * x * (1.0 + lax.erf(x * (2.0 ** -0.5)))


def _mixer_a_body(u_ref, v_ref, lng_ref, lnb_ref, ws_ref, sb_ref, o_ref):
    tm = u_ref.shape[0]
    gd = A_GROUP_DIM
    row = lax.broadcasted_iota(jnp.int32, (gd, gd), 0)
    col = lax.broadcasted_iota(jnp.int32, (gd, gd), 1)
    causal = col <= row
    for g in range(u_ref.shape[1] // gd):
        sl = slice(g * gd, (g + 1) * gd)
        u = _gelu(u_ref[:, sl])
        v = _gelu(v_ref[:, sl])
        mu = jnp.mean(v, axis=-1, keepdims=True)
        dv = v - mu
        var = jnp.mean(dv * dv, axis=-1, keepdims=True)
        vn = ((dv * lax.rsqrt(var + LN_EPS)) * lng_ref[:, sl] + lnb_ref[:, sl]).astype(BF16)
        w = jnp.where(causal, ws_ref[g], 0.0).astype(BF16)
        for n in range(tm // gd):
            rs = slice(n * gd, (n + 1) * gd)
            mixed = _dot(w, vn[rs]) + sb_ref[g]
            o_ref[rs, sl] = (u[rs] * mixed).astype(o_ref.dtype)


def _mixer_a(proj, ln_g, ln_b, spatial_w, spatial_b, a_width, tm=ROW_TILE):
    t = proj.shape[0]
    tm = min(tm, t)
    groups = a_width // A_GROUP_DIM
    sb = jnp.broadcast_to(spatial_b[:, :, None], (groups, A_GROUP_DIM, A_GROUP_DIM))
    vec = pl.BlockSpec((1, a_width), lambda i: (0, 0))
    mat = pl.BlockSpec((groups, A_GROUP_DIM, A_GROUP_DIM), lambda i: (0, 0, 0))
    return pl.pallas_call(
        _mixer_a_body,
        out_shape=jax.ShapeDtypeStruct((t, a_width), BF16),
        grid=(t // tm,),
        in_specs=[pl.BlockSpec((tm, a_width), lambda i: (i, 0)),
                  pl.BlockSpec((tm, a_width), lambda i: (i, 1)),
                  vec, vec, mat, mat],
        out_specs=pl.BlockSpec((tm, a_width), lambda i: (i, 0)),
        compiler_params=_params(("arbitrary",)),
        name="mixer_a",
    )(proj, proj, ln_g.reshape(1, -1), ln_b.reshape(1, -1), spatial_w, sb)


def _split_bf16(x, parts):
    out = []
    for _ in range(parts - 1):
        hi = x.astype(BF16)
        out.append(hi)
        x = x - hi.astype(F32)
    out.append(x.astype(BF16))
    return out


def _head_sums(x):
    head0 = lax.broadcasted_iota(jnp.int32, (x.shape[0], LANES), 1) < B_HEAD_DIM
    parts = []
    for s in range(x.shape[1] // LANES):
        xs = x[:, s * LANES:(s + 1) * LANES]
        s0 = jnp.sum(jnp.where(head0, xs, 0.0), axis=-1, keepdims=True)
        s1 = jnp.sum(jnp.where(head0, 0.0, xs), axis=-1, keepdims=True)
        parts.append(jnp.where(head0, s0, s1))
    return jnp.concatenate(parts, axis=1)


def _softplus(x):
    return jnp.maximum(x, 0.0) + jnp.log1p(jnp.exp(-jnp.abs(x)))


def _scan_chunk(r_in, w_in, k_in, v_in, a_in, b_in, s_ref):
    c = SCAN_CHUNK
    ti = lax.broadcasted_iota(jnp.int32, (c, c), 0)
    tj = lax.broadcasted_iota(jnp.int32, (c, c), 1)
    tri_incl = (tj <= ti).astype(BF16)
    head0 = lax.broadcasted_iota(jnp.int32, (c, LANES), 1) < B_HEAD_DIM
    i2 = lax.broadcasted_iota(jnp.int32, (2 * c, 2 * c), 0)
    j2 = lax.broadcasted_iota(jnp.int32, (2 * c, 2 * c), 1)
    eye = (i2 == j2).astype(F32)
    strict_bd = (i2 // c == j2 // c) & (j2 < i2)
    ic = lax.broadcasted_iota(jnp.int32, (c, 2 * c), 0)
    jc = lax.broadcasted_iota(jnp.int32, (c, 2 * c), 1)
    incl_cat = (jc % c) <= ic
    contract0 = (((0,), (0,)), ((), ()))
    contract1 = (((1,), (1,)), ((), ()))

    def stack(x):
        return jnp.concatenate([jnp.where(head0, x, 0.0), jnp.where(head0, 0.0, x)], axis=0)

    def cat(xs, axis):
        return jnp.concatenate(xs, axis=axis)

    pairs = range(r_in.shape[1] // LANES)
    sls = [slice(p * LANES, (p + 1) * LANES) for p in pairs]
    w = [w_in[:, sl] for sl in sls]
    cum3 = [_dot(tri_incl, cat(_split_bf16(w[p], 3), 1)) for p in pairs]
    cum = [x[:, :LANES] + x[:, LANES:2 * LANES] + x[:, 2 * LANES:] for x in cum3]
    cum_last = [x[c - 1:c, :] for x in cum]
    e_neg = [jnp.exp(-cum[p]) for p in pairs]
    e_tail = [jnp.exp(cum_last[p] - cum[p]) for p in pairs]
    r_t = [r_in[:, sls[p]] * jnp.exp(cum[p]) for p in pairs]
    a_st = [stack(a_in[:, sls[p]] * jnp.exp(cum[p] - w[p])) for p in pairs]
    g_c = [jnp.exp(x) for x in cum_last]
    b = [b_in[:, sl] for sl in sls]
    k = [k_in[:, sl] for sl in sls]
    v_bd = [stack(v_in[:, sl]).astype(BF16) for sl in sls]
    lhs = [cat([a_st[p], r_t[p]], 0).astype(BF16) for p in pairs]
    rhs = [cat([stack(b[p] * e_neg[p]), stack(k[p] * e_neg[p])], 0).astype(BF16) for p in pairs]
    big = [lax.dot_general(lhs[p], rhs[p], contract1, preferred_element_type=F32) for p in pairs]
    a_ak = [jnp.where(strict_bd, x[:2 * c, 2 * c:], 0.0).astype(BF16) for x in big]
    a_rb = [jnp.where(incl_cat, x[2 * c:, :2 * c], 0.0).astype(BF16) for x in big]
    a_rk = [jnp.where(incl_cat, x[2 * c:, 2 * c:], 0.0).astype(BF16) for x in big]
    akv = [_dot(a_ak[p], v_bd[p]) for p in pairs]

    x = [jnp.where(strict_bd, y[:2 * c, :2 * c], 0.0) for y in big]
    t_inv = [eye + y for y in x]
    xb = [y.astype(BF16) for y in x]
    x = [_dot(y, y) for y in xb]
    span = 2
    while 2 * span < c:
        prod = [_dot(x[p].astype(BF16), cat([x[p], t_inv[p]], 1).astype(BF16)) for p in pairs]
        x = [y[:, :2 * c] for y in prod]
        t_inv = [t_inv[p] + prod[p][:, 2 * c:] for p in pairs]
        span *= 2
    t_inv = [t_inv[p] + _dot(x[p].astype(BF16), t_inv[p].astype(BF16)) for p in pairs]

    tub = [_dot(t_inv[p].astype(BF16), cat([a_st[p], akv[p]], 1).astype(BF16)).astype(BF16)
           for p in pairs]
    ry = [_dot(a_rb[p], tub[p]) for p in pairs]
    y0 = [ry[p][:, LANES:] + _dot(a_rk[p], v_bd[p]) for p in pairs]
    r_hat = [r_t[p] + ry[p][:, :LANES] for p in pairs]
    bg = [stack(b[p] * e_tail[p]).astype(BF16) for p in pairs]
    kg = [stack(k[p] * e_tail[p]).astype(BF16) for p in pairs]
    mn = [lax.dot_general(bg[p], tub[p], contract0, preferred_element_type=F32) for p in pairs]
    n = [mn[p][:, LANES:] + lax.dot_general(kg[p], v_bd[p], contract0, preferred_element_type=F32)
         for p in pairs]

    s0 = [s_ref[p] for p in pairs]
    fin = [_dot(cat([r_hat[p], mn[p][:, :LANES]], 0).astype(BF16), cat(_split_bf16(s0[p], 2), 1))
           for p in pairs]
    g_col = [jnp.sum(eye * g_c[p], axis=1, keepdims=True) for p in pairs]
    for p in pairs:
        s_ref[p] = g_col[p] * s0[p] + (fin[p][c:, :LANES] + fin[p][c:, LANES:]) + n[p]
    return cat([fin[p][:c, :LANES] + fin[p][:c, LANES:] + y0[p] for p in pairs], 1)


def _mixer_b_body(r_ref, k_ref, v_ref, l_ref, mur_ref, muk_ref, muv_ref, mul_ref, dup_ref, iup_ref, gup_ref,
                  dbase_ref, ibase_ref, kks_ref, kas_ref, bonus_ref, gng_ref, gnb_ref,
                  o_ref, s_ref, pr_ref, pk_ref, pv_ref, pl_ref):
    @pl.when(pl.program_id(0) == 0)
    def _():
        for ref in (s_ref, pr_ref, pk_ref, pv_ref, pl_ref):
            ref[...] = jnp.zeros(ref.shape, F32)

    def shifted(p_ref, prev_ref, mu_ref):
        p = p_ref[...]
        rowid = lax.broadcasted_iota(jnp.int32, p.shape, 0)
        prev = jnp.where(rowid == 0, prev_ref[...], pltpu.roll(p, 1, 0))
        prev_ref[...] = p[p.shape[0] - 1:, :]
        return p + mu_ref[...] * (prev - p)

    r = shifted(r_ref, pr_ref, mur_ref)
    k = shifted(k_ref, pk_ref, muk_ref)
    v = shifted(v_ref, pv_ref, muv_ref)
    lora = shifted(l_ref, pl_ref, mul_ref)

    dec_in = dbase_ref[...] + _dot(jnp.tanh(lora).astype(BF16), dup_ref[...])
    w_log = -_softplus(-dec_in) - 0.5
    log_decay = -jnp.exp(w_log)
    a = jax.nn.sigmoid(ibase_ref[...] + _dot(lora.astype(BF16), iup_ref[...]))
    g = _dot(jax.nn.sigmoid(lora).astype(BF16), gup_ref[...])

    kk = k * kks_ref[...]
    kk = kk / jnp.maximum(jnp.sqrt(_head_sums(kk * kk)), KK_EPS)
    k = k * (1.0 + (a - 1.0) * kas_ref[...])
    bonus_v = _head_sums(r * k * bonus_ref[...]) * v

    y = _scan_chunk(r, log_decay, k, v, -kk, kk * a, s_ref)

    inv_n = 1.0 / B_HEAD_DIM
    mu = _head_sums(y) * inv_n
    d = y - mu
    var = _head_sums(d * d) * inv_n
    yn = (d * lax.rsqrt(var + GN_EPS)) * gng_ref[...] + gnb_ref[...]
    o_ref[...] = ((yn + bonus_v) * g).astype(o_ref.dtype)


def _mixer_b(proj, a_cols, bw, lora_col, lora_w, shift_mu, decay_up, decay_base, iclr_up, iclr_base,
             gate_up, kk_scale, ka_scale, bonus, gn_g, gn_b):
    t = proj.shape[0]
    c = SCAN_CHUNK
    cb = a_cols // bw
    lb = lora_col // lora_w
    n_lora = DECAY_LORA + ICLR_LORA + GATE_LORA

    def pad_rows(w, start):
        return jnp.zeros((lora_w, bw), F32).at[start:start + w.shape[0]].set(w).astype(BF16)

    dup = pad_rows(decay_up, 0)
    iup = pad_rows(iclr_up, DECAY_LORA)
    gup = pad_rows(gate_up, DECAY_LORA + ICLR_LORA)
    mu = shift_mu.reshape(1, -1)
    mu_l = jnp.zeros((1, lora_w), F32).at[:, :n_lora].set(mu[:, 3 * bw:])

    def cur(width, blk):
        return pl.BlockSpec((c, width), lambda i: (i, blk))

    vec = pl.BlockSpec((1, bw), lambda i: (0, 0))
    vecl = pl.BlockSpec((1, lora_w), lambda i: (0, 0))
    up = pl.BlockSpec((lora_w, bw), lambda i: (0, 0))
    row = lambda x: x.reshape(1, -1)
    return pl.pallas_call(
        _mixer_b_body,
        out_shape=jax.ShapeDtypeStruct((t, bw), BF16),
        grid=(t // c,),
        in_specs=[cur(bw, cb), cur(bw, cb + 1), cur(bw, cb + 2), cur(lora_w, lb),
                  vec, vec, vec, vecl, up, up, up, vec, vec, vec, vec, vec, vec, vec],
        out_specs=pl.BlockSpec((c, bw), lambda i: (i, 0)),
        scratch_shapes=[pltpu.VMEM((bw // LANES, LANES, LANES), F32),
                        pltpu.VMEM((1, bw), F32), pltpu.VMEM((1, bw), F32), pltpu.VMEM((1, bw), F32),
                        pltpu.VMEM((1, lora_w), F32)],
        compiler_params=_params(("arbitrary",)),
        name="mixer_b",
    )(proj, proj, proj, proj, mu[:, :bw], mu[:, bw:2 * bw], mu[:, 2 * bw:3 * bw], mu_l, dup, iup, gup,
      row(decay_base), row(iclr_base), row(kk_scale), row(ka_scale), row(bonus), row(gn_g), row(gn_b))


def _out_proj_body(ya_ref, yb_ref, w_ref, x_ref, g1_ref, o_ref, wb_ref):
    @pl.when(pl.program_id(1) == 0)
    def _():
        wb_ref[...] = w_ref[...].astype(wb_ref.dtype)

    half = ya_ref.shape[1]
    mix = _dot(ya_ref[...], wb_ref[:half, :]) + _dot(yb_ref[...], wb_ref[half:, :])
    o_ref[...] = x_ref[...] + g1_ref[...] * mix


def _out_projection(ya, yb, w_out, x, g1, tm=MATMUL_ROWS, tn=MATMUL_COLS):
    t, d = x.shape
    tm = min(tm, t)
    half = ya.shape[1]
    act = pl.BlockSpec((tm, half), lambda j, i: (i, 0))
    tile = pl.BlockSpec((tm, tn), lambda j, i: (i, j))
    return pl.pallas_call(
        _out_proj_body,
        out_shape=jax.ShapeDtypeStruct((t, d), F32),
        grid=(d // tn, t // tm),
        in_specs=[act, act, pl.BlockSpec((2 * half, tn), lambda j, i: (0, j)), tile,
                  pl.BlockSpec((1, tn), lambda j, i: (0, j))],
        out_specs=tile,
        scratch_shapes=[pltpu.VMEM((2 * half, tn), BF16)],
        compiler_params=_params(("arbitrary", "arbitrary")),
        name="out_projection",
    )(ya, yb, w_out, x, g1)


def _router_body(x_ref, g_ref, sh_ref, sc_ref, rw_ref, rb_ref, sgu_ref, sd_ref, g2_ref,
                 te_ref, gate_ref, pos_ref, cnt_ref, hp_ref, base_ref, carry_ref):
    @pl.when(pl.program_id(0) == 0)
    def _():
        carry_ref[...] = jnp.zeros(carry_ref.shape, F32)

    x = x_ref[...]
    h = _norm_modulate(x, g_ref[...], sh_ref[...], sc_ref[...])
    hp_ref[...] = _pack_halves(h)
    ds = sd_ref.shape[0]
    gu = _even_matmul(h.astype(BF16), sgu_ref)
    act = (jax.nn.silu(gu[:, :ds]) * gu[:, ds:]).astype(BF16)
    base_ref[...] = x + g2_ref[...] * _dot(act, sd_ref[...])
    tm = x.shape[0]
    n_exp = rw_ref.shape[1]
    per_group = n_exp // N_GROUPS
    neg_inf = -jnp.inf

    h_hi, h_lo = _split_bf16(h, 2)
    w_hi, w_lo = _split_bf16(rw_ref[...], 2)
    scores = jax.nn.sigmoid(_dot(h_hi, w_hi) + (_dot(h_hi, w_lo) + _dot(h_lo, w_hi)))
    sel = scores + rb_ref[...]
    lane = lax.broadcasted_iota(jnp.int32, (tm, n_exp), 1)
    gid = lane // per_group

    gscore = jnp.zeros((tm, n_exp), F32)
    for gi in range(N_GROUPS):
        in_g = gid == gi
        m1 = jnp.max(jnp.where(in_g, sel, neg_inf), axis=-1, keepdims=True)
        i1 = jnp.min(jnp.where(in_g & (sel == m1), lane, n_exp), axis=-1, keepdims=True)
        m2 = jnp.max(jnp.where(in_g & (lane != i1), sel, neg_inf), axis=-1, keepdims=True)
        gscore = jnp.where(in_g, m1 + m2, gscore)
    beaten = jnp.zeros((tm, n_exp), jnp.int32)
    for s in range(1, N_GROUPS):
        other = pltpu.roll(gscore, s * per_group, 1)
        wins = (other > gscore) | ((other == gscore) & (gid >= s))
        beaten = beaten + wins.astype(jnp.int32)
    allowed = beaten < TOPK_GROUPS

    masked = jnp.where(allowed, sel, neg_inf)
    chosen = jnp.zeros((tm, n_exp), F32)
    idxs, gates = [], []
    for _ in range(TOP_K):
        m = jnp.max(masked, axis=-1, keepdims=True)
        idx = jnp.min(jnp.where(masked == m, lane, n_exp), axis=-1, keepdims=True)
        hit = lane == idx
        gates.append(jnp.sum(jnp.where(hit, scores, 0.0), axis=-1, keepdims=True))
        idxs.append(idx)
        chosen = jnp.where(hit, 1.0, chosen)
        masked = jnp.where(hit, neg_inf, masked)
    gsum = gates[0]
    for gk in gates[1:]:
        gsum = gsum + gk

    ri = lax.broadcasted_iota(jnp.int32, (tm, tm), 0)
    ci = lax.broadcasted_iota(jnp.int32, (tm, tm), 1)
    before = (ci < ri).astype(BF16)
    pos = carry_ref[...] + _dot(before, chosen.astype(BF16))
    carry_ref[...] = carry_ref[...] + jnp.sum(chosen, axis=0, keepdims=True)
    cnt_ref[...] = jnp.broadcast_to(carry_ref[...], cnt_ref.shape).astype(jnp.int32)

    te = jnp.zeros((tm, n_exp), jnp.int32)
    gt = jnp.zeros((tm, n_exp), F32)
    ps = jnp.zeros((tm, n_exp), jnp.int32)
    for kk in range(TOP_K):
        slot = lane == kk
        hit = lane == idxs[kk]
        pk = jnp.sum(jnp.where(hit, pos, 0.0), axis=-1, keepdims=True)
        te = jnp.where(slot, idxs[kk], te)
        gt = jnp.where(slot, gates[kk] / gsum * ROUTED_SCALE, gt)
        ps = jnp.where(slot, pk.astype(jnp.int32), ps)
    te_ref[...] = te
    gate_ref[...] = gt
    pos_ref[...] = ps


def _router(x1, norm_g, shift, scale, router_w, router_bias, sh_w_gate, sh_w_up, sh_w_down, g2, tm=ROW_TILE):
    t, d = x1.shape
    tm = min(tm, t)
    n_exp = router_w.shape[1]
    ds = sh_w_gate.shape[1]
    sgu = jnp.concatenate([sh_w_gate, sh_w_up], axis=1).astype(BF16)
    row = pl.BlockSpec((tm, d), lambda i: (i, 0))
    vec = pl.BlockSpec((1, d), lambda i: (0, 0))
    small = pl.BlockSpec((tm, n_exp), lambda i: (i, 0))
    return pl.pallas_call(
        _router_body,
        out_shape=[jax.ShapeDtypeStruct((t, n_exp), jnp.int32), jax.ShapeDtypeStruct((t, n_exp), F32),
                   jax.ShapeDtypeStruct((t, n_exp), jnp.int32), jax.ShapeDtypeStruct((8, n_exp), jnp.int32),
                   jax.ShapeDtypeStruct((t, d // 2), jnp.uint32), jax.ShapeDtypeStruct((t, d), F32)],
        grid=(t // tm,),
        in_specs=[row, vec, vec, vec,
                  pl.BlockSpec((d, n_exp), lambda i: (0, 0)), pl.BlockSpec((1, n_exp), lambda i: (0, 0)),
                  pl.BlockSpec((d, 2 * ds), lambda i: (0, 0)), pl.BlockSpec((ds, d), lambda i: (0, 0)), vec],
        out_specs=[small, small, small, pl.BlockSpec((8, n_exp), lambda i: (0, 0)),
                   pl.BlockSpec((tm, d // 2), lambda i: (i, 0)), row],
        scratch_shapes=[pltpu.VMEM((1, n_exp), F32)],
        compiler_params=_params(("arbitrary",)),
        name="moe_router",
    )(x1, norm_g, shift, scale, router_w, router_bias.reshape(1, -1), sgu, sh_w_down.astype(BF16), g2)


def _dest_body(starts_ref, te_ref, pos_ref, o_ref):
    te = te_ref[...]
    acc = jnp.zeros(te.shape, jnp.int32)
    for e in range(starts_ref.shape[0]):
        acc = jnp.where(te == e, starts_ref[e], acc)
    o_ref[...] = acc + pos_ref[...]


def _dest_rows(starts, top_e, pos_sel, tm=2 * ROW_TILE):
    t, n_exp = top_e.shape
    tm = min(tm, t)
    blk = pl.BlockSpec((tm, n_exp), lambda i: (i, 0))
    return pl.pallas_call(
        _dest_body,
        out_shape=jax.ShapeDtypeStruct((t, n_exp), jnp.int32),
        grid=(t // tm,),
        in_specs=[pl.BlockSpec(memory_space=pltpu.SMEM), blk, blk],
        out_specs=blk,
        compiler_params=_params(("arbitrary",)),
        name="moe_dest",
    )(starts, top_e, pos_sel)


def _pack_halves(x):
    half = x.shape[1] // 2
    lo = pltpu.bitcast(x[:, :half].astype(BF16).astype(F32), jnp.uint32)
    hi = pltpu.bitcast(x[:, half:].astype(BF16).astype(F32), jnp.uint32)
    return (lo >> 16) | (hi & jnp.uint32(0xFFFF0000))


def _unpack_halves(p):
    lo = pltpu.bitcast(p << 16, F32)
    hi = pltpu.bitcast(p & jnp.uint32(0xFFFF0000), F32)
    return lo, hi


def _dispatch_body(end_ref, nb_ref, dst_ref, hp_ref, xs_hbm, hp_buf, zero_buf, sem, zsem):
    i = pl.program_id(0)
    n = pl.num_programs(0)
    tm = hp_ref.shape[0]
    rows = zero_buf.shape[0]
    n_exp = end_ref.shape[0]
    n_blocks = xs_hbm.shape[0] // rows

    def drain(slot):
        for _ in range(TOP_K):
            pltpu.make_async_copy(hp_buf.at[slot], xs_hbm.at[pl.ds(0, tm), :], sem.at[slot]).wait()

    @pl.when(i == 0)
    def _():
        zero_buf[...] = jnp.zeros(zero_buf.shape, zero_buf.dtype)
        nb = nb_ref[0]

        def zero_rows(start, count):
            first = start if count == 1 else pl.multiple_of(start, 8)
            return pltpu.make_async_copy(zero_buf.at[pl.ds(0, count), :], xs_hbm.at[pl.ds(first, count), :], zsem)

        def issue(start, stop, count):
            def body(j, carry):
                zero_rows(start + j * count, count).start()
                return carry
            n_copies = (stop - start) // count
            lax.fori_loop(0, n_copies, body, 0)
            return n_copies

        def fill(e, carry):
            n1, n8 = carry
            end = end_ref[e]
            aligned = (end + 7) // 8 * 8
            stop = (end + rows - 1) // rows * rows
            return n1 + issue(end, aligned, 1), n8 + issue(aligned, stop, 8)
        n1, n8 = lax.fori_loop(0, n_exp, fill, (0, 0))
        n_tail = issue(nb * rows, n_blocks * rows, rows)

        for count, n_copies in ((1, n1), (8, n8), (rows, n_tail)):
            def wait(j, carry, count=count):
                zero_rows(0, count).wait()
                return carry
            lax.fori_loop(0, n_copies, wait, 0)

    slot = i % 2

    @pl.when(i >= 2)
    def _():
        drain(slot)

    hp_buf[slot] = hp_ref[...]

    def body(g, carry):
        t0 = pl.multiple_of(g * SUBLANES, SUBLANES)
        for j in range(SUBLANES):
            for kk in range(TOP_K):
                pltpu.make_async_copy(hp_buf.at[slot, pl.ds(t0 + j, 1), :],
                                      xs_hbm.at[pl.ds(dst_ref[0, 0, (t0 + j) * TOP_K + kk], 1), :],
                                      sem.at[slot]).start()
        return carry
    lax.fori_loop(0, tm // SUBLANES, body, 0)

    @pl.when(i == n - 1)
    def _():
        drain(slot)

        @pl.when(n > 1)
        def _():
            drain(1 - slot)


def _dispatch(h_packed, seg_end, n_used, dest, n_rows, tm=ROW_TILE):
    t, half = h_packed.shape
    tm = min(tm, t)
    nt = t // tm
    dest3 = dest.reshape(nt, 1, tm * TOP_K)
    grid_spec = pltpu.PrefetchScalarGridSpec(
        num_scalar_prefetch=2,
        grid=(nt,),
        in_specs=[pl.BlockSpec((1, 1, tm * TOP_K), lambda i, e, nb: (i, 0, 0), memory_space=pltpu.SMEM),
                  pl.BlockSpec((tm, half), lambda i, e, nb: (i, 0))],
        out_specs=pl.BlockSpec(memory_space=pl.ANY),
        scratch_shapes=[pltpu.VMEM((2, tm, half), jnp.uint32), pltpu.VMEM((MOE_ROWS, half), jnp.uint32),
                        pltpu.SemaphoreType.DMA((2,)), pltpu.SemaphoreType.DMA(())],
    )
    return pl.pallas_call(
        _dispatch_body,
        out_shape=jax.ShapeDtypeStruct((n_rows, half), jnp.uint32),
        grid_spec=grid_spec,
        compiler_params=_params(("arbitrary",)),
        name="moe_dispatch",
    )(seg_end, n_used, dest3, h_packed)


def _expert_body(be_ref, nb_ref, first_ref, next_ref, slot_ref, x_ref, wg_hbm, wu_hbm, wd_hbm, o_ref,
                 wgu_buf, wd_buf, sem):
    b = pl.program_id(0)
    nb = nb_ref[0]
    de = wd_buf.shape[1]

    def weight_copies(e, slot):
        return [pltpu.make_async_copy(wg_hbm.at[e], wgu_buf.at[slot, :, pl.ds(0, de)], sem.at[slot]),
                pltpu.make_async_copy(wu_hbm.at[e], wgu_buf.at[slot, :, pl.ds(de, de)], sem.at[slot]),
                pltpu.make_async_copy(wd_hbm.at[e], wd_buf.at[slot], sem.at[slot])]

    def start_weights(e, slot):
        for cp, queue in zip(weight_copies(e, slot), (1, 1, 0)):
            cp.start(priority=queue)

    @pl.when((b == 0) & (nb > 0))
    def _():
        start_weights(be_ref[0], 0)

    @pl.when(b < nb)
    def _():
        slot = slot_ref[b]

        @pl.when(first_ref[b] == 1)
        def _():
            for cp in weight_copies(0, slot):
                cp.wait()

            @pl.when(next_ref[b] >= 0)
            def _():
                start_weights(next_ref[b], 1 - slot)

        lo, hi = _unpack_halves(x_ref[...])
        x = jnp.concatenate([lo.astype(BF16), hi.astype(BF16)], axis=1)
        gu = _even_matmul(x, wgu_buf.at[slot])
        act = (jax.nn.silu(gu[:, :de]) * gu[:, de:]).astype(BF16)
        o_ref[...] = _pack_halves(_dot(act, wd_buf[slot].astype(BF16)))

    @pl.when(b >= nb)
    def _():
        o_ref[...] = jnp.zeros(o_ref.shape, o_ref.dtype)


def _routed_experts(x_sorted, blk_expert, n_used, blk_first, blk_next, blk_slot, w_gate, w_up, w_down):
    n_exp, d, de = w_gate.shape
    rows = MOE_ROWS
    nb = x_sorted.shape[0] // rows

    def last_used(b, be, nbr, *_):
        return (jnp.maximum(jnp.minimum(b, nbr[0] - 1), 0), 0)

    any_space = pl.BlockSpec(memory_space=pl.ANY)
    grid_spec = pltpu.PrefetchScalarGridSpec(
        num_scalar_prefetch=5,
        grid=(nb,),
        in_specs=[pl.BlockSpec((rows, d // 2), last_used), any_space, any_space, any_space],
        out_specs=pl.BlockSpec((rows, d // 2), lambda b, *_: (b, 0)),
        scratch_shapes=[pltpu.VMEM((2, d, 2 * de), F32), pltpu.VMEM((2, de, d), F32),
                        pltpu.SemaphoreType.DMA((2,))],
    )
    return pl.pallas_call(
        _expert_body,
        out_shape=jax.ShapeDtypeStruct((nb * rows, d // 2), jnp.uint32),
        grid_spec=grid_spec,
        compiler_params=_params(("arbitrary",)),
        name="routed_experts",
    )(blk_expert, n_used, blk_first, blk_next, blk_slot, x_sorted, w_gate, w_up, w_down)


def _combine_body(dst_ref, dstn_ref, y_hbm, gate_ref, base_ref, g2_ref, fg_ref, o_ref, ybuf, sem):
    i = pl.program_id(0)
    n = pl.num_programs(0)
    tm = base_ref.shape[0]

    def issue(dref, slot):
        def body(g, carry):
            t0 = pl.multiple_of(g * SUBLANES, SUBLANES)
            for j in range(SUBLANES):
                for kk in range(TOP_K):
                    pltpu.make_async_copy(y_hbm.at[pl.ds(dref[0, 0, (t0 + j) * TOP_K + kk], 1), :],
                                          ybuf.at[slot, kk, pl.ds(t0 + j, 1), :], sem.at[slot]).start(priority=kk % 2)
            return carry
        lax.fori_loop(0, tm // SUBLANES, body, 0)

    @pl.when(i == 0)
    def _():
        issue(dst_ref, 0)

    @pl.when(i + 1 < n)
    def _():
        issue(dstn_ref, (i + 1) % 2)

    slot = i % 2
    for kk in range(TOP_K):
        pltpu.make_async_copy(y_hbm.at[pl.ds(0, tm), :], ybuf.at[slot, kk], sem.at[slot]).wait()
    half = ybuf.shape[-1]
    r_lo = r_hi = None
    for kk in range(TOP_K):
        lo, hi = _unpack_halves(ybuf[slot, kk])
        gk = gate_ref[:, kk:kk + 1]
        r_lo = gk * lo if r_lo is None else r_lo + gk * lo
        r_hi = gk * hi if r_hi is None else r_hi + gk * hi
    x_lo = base_ref[:, :half] + g2_ref[:, :half] * r_lo
    x_hi = base_ref[:, half:] + g2_ref[:, half:] * r_hi
    ssq = jnp.sum(x_lo * x_lo, axis=-1, keepdims=True) + jnp.sum(x_hi * x_hi, axis=-1, keepdims=True)
    inv = lax.rsqrt(ssq / (2 * half) + NORM_EPS)
    o_ref[:, :half] = x_lo * inv * fg_ref[:, :half]
    o_ref[:, half:] = x_hi * inv * fg_ref[:, half:]


def _combine(y_sorted, dest, gate, base, g2, final_g):
    t, d = base.shape
    tm = min(COMBINE_TOKENS, t)
    nt = t // tm
    dest3 = dest.reshape(nt, 1, tm * TOP_K)
    n_exp = gate.shape[1]
    vec = pl.BlockSpec((1, d), lambda i: (0, 0))
    return pl.pallas_call(
        _combine_body,
        out_shape=jax.ShapeDtypeStruct((t, d), F32),
        grid=(nt,),
        in_specs=[pl.BlockSpec((1, 1, tm * TOP_K), lambda i: (i, 0, 0), memory_space=pltpu.SMEM),
                  pl.BlockSpec((1, 1, tm * TOP_K), lambda i: (jnp.minimum(i + 1, nt - 1), 0, 0),
                               memory_space=pltpu.SMEM),
                  pl.BlockSpec(memory_space=pl.ANY),
                  pl.BlockSpec((tm, n_exp), lambda i: (i, 0)),
                  pl.BlockSpec((tm, d), lambda i: (i, 0)), vec, vec],
        out_specs=pl.BlockSpec((tm, d), lambda i: (i, 0)),
        scratch_shapes=[pltpu.VMEM((2, TOP_K, tm, d // 2), jnp.uint32), pltpu.SemaphoreType.DMA((2,))],
        compiler_params=_params(("arbitrary",)),
        name="moe_combine",
    )(dest3, dest3, y_sorted, gate, base, g2, final_g.reshape(1, -1))


def _segment_tables(counts, n_tok):
    n_exp = counts.shape[0]
    rows = MOE_ROWS
    padded = (counts + rows - 1) // rows * rows
    ends = jnp.cumsum(padded)
    starts = ends - padded
    nb = n_tok * TOP_K // rows + n_exp + 1
    first_row = jnp.arange(nb, dtype=jnp.int32) * rows
    blk_expert = jnp.sum((ends[None, :] <= first_row[:, None]).astype(jnp.int32), axis=1)
    blk_expert = jnp.minimum(blk_expert, n_exp - 1)
    n_used = ends[-1] // rows
    blk = jnp.arange(nb, dtype=jnp.int32)
    prev_expert = jnp.concatenate([jnp.full((1,), -1, jnp.int32), blk_expert[:-1]])
    blk_first = ((blk_expert != prev_expert) & (blk < n_used)).astype(jnp.int32)
    blk_slot = (jnp.cumsum(blk_first) - 1) % 2
    after = ends[blk_expert] // rows
    blk_next = jnp.where(after < n_used, blk_expert[jnp.minimum(after, nb - 1)], -1)
    tables = [t.astype(jnp.int32) for t in (blk_expert, n_used.reshape(1), blk_first, blk_next, blk_slot)]
    return starts.astype(jnp.int32), (starts + counts).astype(jnp.int32), tables, nb * rows


def _layer(x, c, mod_w, mod_b, norm1_g, norm2_g, w_in, w_out, a_ln_g, a_ln_b, a_spatial_w, a_spatial_b,
           b_shift_mu, b_decay_up, b_decay_base, b_iclr_up, b_iclr_base, b_gate_up, b_kk_scale,
           b_ka_scale, b_bonus, b_gn_g, b_gn_b, router_w, router_bias, exp_w_gate, exp_w_up,
           exp_w_down, sh_w_gate, sh_w_up, sh_w_down):
    t, d = x.shape
    a_width = a_ln_g.shape[0]
    bw = b_decay_base.shape[0]
    lora_w = MATMUL_COLS
    lora_col = 2 * a_width + 3 * bw
    assert lora_col % lora_w == 0 and w_in.shape[1] - lora_col <= lora_w

    mod = _mod_vector(c, mod_w, mod_b)
    sh1, sc1, g1, sh2, sc2, g2 = [mod[:, i * d:(i + 1) * d] for i in range(6)]

    h1 = _normmod(x, norm1_g.reshape(1, d), sh1, sc1)
    proj = _in_projection(h1, w_in.T, tn=lora_w)
    ya = _mixer_a(proj, a_ln_g, a_ln_b, a_spatial_w, a_spatial_b, a_width)
    yb = _mixer_b(proj, 2 * a_width, bw, lora_col, lora_w, b_shift_mu, b_decay_up, b_decay_base, b_iclr_up,
                  b_iclr_base, b_gate_up, b_kk_scale, b_ka_scale, b_bonus.reshape(-1), b_gn_g, b_gn_b)
    x1 = _out_projection(ya, yb, w_out, x, g1)

    n2 = norm2_g.reshape(1, d)
    top_e, gate, pos_sel, counts, h_packed, base = _router(
        x1, n2, sh2, sc2, router_w, router_bias, sh_w_gate, sh_w_up, sh_w_down, g2)
    starts, seg_end, blk_tables, n_rows = _segment_tables(counts[0], t)
    dest = _dest_rows(starts, top_e, pos_sel)[:, :TOP_K]
    x_sorted = _dispatch(h_packed, seg_end, blk_tables[1], dest, n_rows)
    y_sorted = _routed_experts(x_sorted, *blk_tables, exp_w_gate, exp_w_up, exp_w_down)
    return base, g2, y_sorted, dest, gate


def kernel(x, c, mod_w, mod_b, norm1_g, norm2_g, w_in, w_out, a_ln_g, a_ln_b, a_spatial_w, a_spatial_b, b_shift_mu, b_decay_up, b_decay_base, b_iclr_up, b_iclr_base, b_gate_up, b_kk_scale, b_ka_scale, b_bonus, b_gn_g, b_gn_b, router_w, router_bias, exp_w_gate, exp_w_up, exp_w_down, sh_w_gate, sh_w_up, sh_w_down, final_g):
    batch, seq, d = x.shape
    assert batch == 1 and mod_w.shape[0] == 1, "single sequence, single layer"
    layer = [p[0] for p in (mod_w, mod_b, norm1_g, norm2_g, w_in, w_out, a_ln_g, a_ln_b, a_spatial_w,
                            a_spatial_b, b_shift_mu, b_decay_up, b_decay_base, b_iclr_up, b_iclr_base,
                            b_gate_up, b_kk_scale, b_ka_scale, b_bonus, b_gn_g, b_gn_b, router_w,
                            router_bias, exp_w_gate, exp_w_up, exp_w_down, sh_w_gate, sh_w_up, sh_w_down)]
    base, g2, y_sorted, dest, gate = _layer(x[0], c, *layer)
    out = _combine(y_sorted, dest, gate, base, g2, final_g)
    return out.reshape(batch, seq, d)
```

```python
import functools

import jax
import jax.numpy as jnp
from jax import lax
from jax.experimental import pallas as pl
from jax.experimental.pallas import tpu as pltpu

A_GROUP_DIM = 128
B_HEAD_DIM = 64
DECAY_LORA = 96
ICLR_LORA = 96
GATE_LORA = 256
TOP_K = 8
N_GROUPS = 8
TOPK_GROUPS = 4
ROUTED_SCALE = 2.5
NORM_EPS = 1e-6
LN_EPS = 1e-5
GN_EPS = 64e-5
KK_EPS = 1e-12

LANES = 128
SUBLANES = 8
MXU_WIDTH = 256
VMEM_LIMIT_BYTES = 58 * 1024 * 1024

ROW_TILE = 256
MATMUL_ROWS = 1024
MATMUL_COLS = 512
SCAN_CHUNK = 64
MOE_ROWS = 256
COMBINE_TOKENS = 256

F32 = jnp.float32
BF16 = jnp.bfloat16


def _params(sem):
    return pltpu.CompilerParams(dimension_semantics=sem, vmem_limit_bytes=VMEM_LIMIT_BYTES)


def _dot(a, b):
    return jnp.dot(a, b, preferred_element_type=F32)


def _even_matmul(x, w_ref):
    hm = x.shape[0] // 2
    tiles = w_ref.shape[1] // MXU_WIDTH
    rows = [jnp.concatenate([_dot(x[r * hm:(r + 1) * hm],
                                  w_ref[:, c * MXU_WIDTH:(c + 1) * MXU_WIDTH].astype(BF16))
                             for c in range(tiles)], axis=1) for r in range(2)]
    return jnp.concatenate(rows, axis=0)


def _mod_body(c_ref, w_ref, b_ref, o_ref, s_ref):
    @pl.when(pl.program_id(0) == 0)
    def _():
        c = c_ref[...]
        s_ref[...] = jnp.broadcast_to(c * jax.nn.sigmoid(c), s_ref.shape)

    s = s_ref[...]
    cols = [jnp.sum(w_ref[:, j * LANES:(j + 1) * LANES] * s, axis=0, keepdims=True)
            for j in range(w_ref.shape[1] // LANES)]
    o_ref[...] = jnp.concatenate(cols, axis=1) + b_ref[...]


def _mod_vector(c, mod_w, mod_b, tn=MATMUL_COLS):
    d, n = mod_w.shape
    return pl.pallas_call(
        _mod_body,
        out_shape=jax.ShapeDtypeStruct((1, n), F32),
        grid=(n // tn,),
        in_specs=[pl.BlockSpec((d, 1), lambda j: (0, 0)),
                  pl.BlockSpec((d, tn), lambda j: (0, j)),
                  pl.BlockSpec((1, tn), lambda j: (0, j))],
        out_specs=pl.BlockSpec((1, tn), lambda j: (0, j)),
        scratch_shapes=[pltpu.VMEM((d, LANES), F32)],
        compiler_params=_params(("arbitrary",)),
        name="mod_vector",
    )(c.reshape(d, 1), mod_w, mod_b.reshape(1, n))


def _norm_modulate(x, g, shift, scale):
    ms = jnp.mean(x * x, axis=-1, keepdims=True)
    y = x * lax.rsqrt(ms + NORM_EPS) * g
    return y * (1.0 + scale) + shift


def _normmod_body(x_ref, g_ref, sh_ref, sc_ref, o_ref):
    o_ref[...] = _norm_modulate(x_ref[...], g_ref[...], sh_ref[...], sc_ref[...]).astype(o_ref.dtype)


def _normmod(x, g, shift, scale, tm=ROW_TILE):
    t, d = x.shape
    tm = min(tm, t)
    row = pl.BlockSpec((tm, d), lambda i: (i, 0))
    vec = pl.BlockSpec((1, d), lambda i: (0, 0))
    return pl.pallas_call(
        _normmod_body,
        out_shape=jax.ShapeDtypeStruct((t, d), BF16),
        grid=(t // tm,),
        in_specs=[row, vec, vec, vec],
        out_specs=row,
        compiler_params=_params(("arbitrary",)),
        name="norm_modulate",
    )(x, g, shift, scale)


def _proj_body(h_ref, wt_ref, o_ref, wb_ref, *, n_valid):
    @pl.when(pl.program_id(1) == 0)
    def _():
        wt = wt_ref[...]
        col = pl.program_id(0) * wt.shape[0] + lax.broadcasted_iota(jnp.int32, wt.shape, 0)
        wb_ref[...] = jnp.where(col < n_valid, wt, 0.0).T.astype(wb_ref.dtype)

    o_ref[...] = _dot(h_ref[...], wb_ref[...])


def _in_projection(h, w_in_t, tm=MATMUL_ROWS, tn=MATMUL_COLS):
    t, d = h.shape
    n = w_in_t.shape[0]
    tm = min(tm, t)
    n_pad = pl.cdiv(n, tn) * tn
    return pl.pallas_call(
        functools.partial(_proj_body, n_valid=n),
        out_shape=jax.ShapeDtypeStruct((t, n_pad), F32),
        grid=(n_pad // tn, t // tm),
        in_specs=[pl.BlockSpec((tm, d), lambda j, i: (i, 0)),
                  pl.BlockSpec((tn, d), lambda j, i: (j, 0))],
        out_specs=pl.BlockSpec((tm, tn), lambda j, i: (i, j)),
        scratch_shapes=[pltpu.VMEM((d, tn), BF16)],
        compiler_params=_params(("arbitrary", "arbitrary")),
        name="in_projection",
    )(h, w_in_t)


def _gelu(x):
    return 0.5 * x * (1.0 + lax.erf(x * (2.0 ** -0.5)))


def _mixer_a_body(u_ref, v_ref, lng_ref, lnb_ref, ws_ref, sb_ref, o_ref):
    tm = u_ref.shape[0]
    gd = A_GROUP_DIM
    row = lax.broadcasted_iota(jnp.int32, (gd, gd), 0)
    col = lax.broadcasted_iota(jnp.int32, (gd, gd), 1)
    causal = col <= row
    for g in range(u_ref.shape[1] // gd):
        sl = slice(g * gd, (g + 1) * gd)
        u = _gelu(u_ref[:, sl])
        v = _gelu(v_ref[:, sl])
        mu = jnp.mean(v, axis=-1, keepdims=True)
        dv = v - mu
        var = jnp.mean(dv * dv, axis=-1, keepdims=True)
        vn = ((dv * lax.rsqrt(var + LN_EPS)) * lng_ref[:, sl] + lnb_ref[:, sl]).astype(BF16)
        w = jnp.where(causal, ws_ref[g], 0.0).astype(BF16)
        for n in range(tm // gd):
            rs = slice(n * gd, (n + 1) * gd)
            mixed = _dot(w, vn[rs]) + sb_ref[g]
            o_ref[rs, sl] = (u[rs] * mixed).astype(o_ref.dtype)


def _mixer_a(proj, ln_g, ln_b, spatial_w, spatial_b, a_width, tm=ROW_TILE):
    t = proj.shape[0]
    tm = min(tm, t)
    groups = a_width // A_GROUP_DIM
    sb = jnp.broadcast_to(spatial_b[:, :, None], (groups, A_GROUP_DIM, A_GROUP_DIM))
    vec = pl.BlockSpec((1, a_width), lambda i: (0, 0))
    mat = pl.BlockSpec((groups, A_GROUP_DIM, A_GROUP_DIM), lambda i: (0, 0, 0))
    return pl.pallas_call(
        _mixer_a_body,
        out_shape=jax.ShapeDtypeStruct((t, a_width), BF16),
        grid=(t // tm,),
        in_specs=[pl.BlockSpec((tm, a_width), lambda i: (i, 0)),
                  pl.BlockSpec((tm, a_width), lambda i: (i, 1)),
                  vec, vec, mat, mat],
        out_specs=pl.BlockSpec((tm, a_width), lambda i: (i, 0)),
        compiler_params=_params(("arbitrary",)),
        name="mixer_a",
    )(proj, proj, ln_g.reshape(1, -1), ln_b.reshape(1, -1), spatial_w, sb)


def _split_bf16(x, parts):
    out = []
    for _ in range(parts - 1):
        hi = x.astype(BF16)
        out.append(hi)
        x = x - hi.astype(F32)
    out.append(x.astype(BF16))
    return out


def _head_sums(x):
    head0 = lax.broadcasted_iota(jnp.int32, (x.shape[0], LANES), 1) < B_HEAD_DIM
    parts = []
    for s in range(x.shape[1] // LANES):
        xs = x[:, s * LANES:(s + 1) * LANES]
        s0 = jnp.sum(jnp.where(head0, xs, 0.0), axis=-1, keepdims=True)
        s1 = jnp.sum(jnp.where(head0, 0.0, xs), axis=-1, keepdims=True)
        parts.append(jnp.where(head0, s0, s1))
    return jnp.concatenate(parts, axis=1)


def _softplus(x):
    return jnp.maximum(x, 0.0) + jnp.log1p(jnp.exp(-jnp.abs(x)))


def _scan_chunk(r_in, w_in, k_in, v_in, a_in, b_in, s_ref):
    c = SCAN_CHUNK
    ti = lax.broadcasted_iota(jnp.int32, (c, c), 0)
    tj = lax.broadcasted_iota(jnp.int32, (c, c), 1)
    tri_incl = (tj <= ti).astype(BF16)
    head0 = lax.broadcasted_iota(jnp.int32, (c, LANES), 1) < B_HEAD_DIM
    i2 = lax.broadcasted_iota(jnp.int32, (2 * c, 2 * c), 0)
    j2 = lax.broadcasted_iota(jnp.int32, (2 * c, 2 * c), 1)
    eye = (i2 == j2).astype(F32)
    strict_bd = (i2 // c == j2 // c) & (j2 < i2)
    ic = lax.broadcasted_iota(jnp.int32, (c, 2 * c), 0)
    jc = lax.broadcasted_iota(jnp.int32, (c, 2 * c), 1)
    incl_cat = (jc % c) <= ic
    contract0 = (((0,), (0,)), ((), ()))
    contract1 = (((1,), (1,)), ((), ()))

    def stack(x):
        return jnp.concatenate([jnp.where(head0, x, 0.0), jnp.where(head0, 0.0, x)], axis=0)

    def cat(xs, axis):
        return jnp.concatenate(xs, axis=axis)

    pairs = range(r_in.shape[1] // LANES)
    sls = [slice(p * LANES, (p + 1) * LANES) for p in pairs]
    w = [w_in[:, sl] for sl in sls]
    cum3 = [_dot(tri_incl, cat(_split_bf16(w[p], 3), 1)) for p in pairs]
    cum = [x[:, :LANES] + x[:, LANES:2 * LANES] + x[:, 2 * LANES:] for x in cum3]
    cum_last = [x[c - 1:c, :] for x in cum]
    e_neg = [jnp.exp(-cum[p]) for p in pairs]
    e_tail = [jnp.exp(cum_last[p] - cum[p]) for p in pairs]
    r_t = [r_in[:, sls[p]] * jnp.exp(cum[p]) for p in pairs]
    a_st = [stack(a_in[:, sls[p]] * jnp.exp(cum[p] - w[p])) for p in pairs]
    g_c = [jnp.exp(x) for x in cum_last]
    b = [b_in[:, sl] for sl in sls]
    k = [k_in[:, sl] for sl in sls]
    v_bd = [stack(v_in[:, sl]).astype(BF16) for sl in sls]
    lhs = [cat([a_st[p], r_t[p]], 0).astype(BF16) for p in pairs]
    rhs = [cat([stack(b[p] * e_neg[p]), stack(k[p] * e_neg[p])], 0).astype(BF16) for p in pairs]
    big = [lax.dot_general(lhs[p], rhs[p], contract1, preferred_element_type=F32) for p in pairs]
    a_ak = [jnp.where(strict_bd, x[:2 * c, 2 * c:], 0.0).astype(BF16) for x in big]
    a_rb = [jnp.where(incl_cat, x[2 * c:, :2 * c], 0.0).astype(BF16) for x in big]
    a_rk = [jnp.where(incl_cat, x[2 * c:, 2 * c:], 0.0).astype(BF16) for x in big]
    akv = [_dot(a_ak[p], v_bd[p]) for p in pairs]

    x = [jnp.where(strict_bd, y[:2 * c, :2 * c], 0.0) for y in big]
    t_inv = [eye + y for y in x]
    xb = [y.astype(BF16) for y in x]
    x = [_dot(y, y) for y in xb]
    span = 2
    while 2 * span < c:
        prod = [_dot(x[p].astype(BF16), cat([x[p], t_inv[p]], 1).astype(BF16)) for p in pairs]
        x = [y[:, :2 * c] for y in prod]
        t_inv = [t_inv[p] + prod[p][:, 2 * c:] for p in pairs]
        span *= 2
    t_inv = [t_inv[p] + _dot(x[p].astype(BF16), t_inv[p].astype(BF16)) for p in pairs]

    tub = [_dot(t_inv[p].astype(BF16), cat([a_st[p], akv[p]], 1).astype(BF16)).astype(BF16)
           for p in pairs]
    ry = [_dot(a_rb[p], tub[p]) for p in pairs]
    y0 = [ry[p][:, LANES:] + _dot(a_rk[p], v_bd[p]) for p in pairs]
    r_hat = [r_t[p] + ry[p][:, :LANES] for p in pairs]
    bg = [stack(b[p] * e_tail[p]).astype(BF16) for p in pairs]
    kg = [stack(k[p] * e_tail[p]).astype(BF16) for p in pairs]
    mn = [lax.dot_general(bg[p], tub[p], contract0, preferred_element_type=F32) for p in pairs]
    n = [mn[p][:, LANES:] + lax.dot_general(kg[p], v_bd[p], contract0, preferred_element_type=F32)
         for p in pairs]

    s0 = [s_ref[p] for p in pairs]
    fin = [_dot(cat([r_hat[p], mn[p][:, :LANES]], 0).astype(BF16), cat(_split_bf16(s0[p], 2), 1))
           for p in pairs]
    g_col = [jnp.sum(eye * g_c[p], axis=1, keepdims=True) for p in pairs]
    for p in pairs:
        s_ref[p] = g_col[p] * s0[p] + (fin[p][c:, :LANES] + fin[p][c:, LANES:]) + n[p]
    return cat([fin[p][:c, :LANES] + fin[p][:c, LANES:] + y0[p] for p in pairs], 1)


def _mixer_b_body(r_ref, k_ref, v_ref, l_ref, mur_ref, muk_ref, muv_ref, mul_ref, dup_ref, iup_ref, gup_ref,
                  dbase_ref, ibase_ref, kks_ref, kas_ref, bonus_ref, gng_ref, gnb_ref,
                  o_ref, s_ref, pr_ref, pk_ref, pv_ref, pl_ref):
    @pl.when(pl.program_id(0) == 0)
    def _():
        for ref in (s_ref, pr_ref, pk_ref, pv_ref, pl_ref):
            ref[...] = jnp.zeros(ref.shape, F32)

    def shifted(p_ref, prev_ref, mu_ref):
        p = p_ref[...]
        rowid = lax.broadcasted_iota(jnp.int32, p.shape, 0)
        prev = jnp.where(rowid == 0, prev_ref[...], pltpu.roll(p, 1, 0))
        prev_ref[...] = p[p.shape[0] - 1:, :]
        return p + mu_ref[...] * (prev - p)

    r = shifted(r_ref, pr_ref, mur_ref)
    k = shifted(k_ref, pk_ref, muk_ref)
    v = shifted(v_ref, pv_ref, muv_ref)
    lora = shifted(l_ref, pl_ref, mul_ref)

    dec_in = dbase_ref[...] + _dot(jnp.tanh(lora).astype(BF16), dup_ref[...])
    w_log = -_softplus(-dec_in) - 0.5
    log_decay = -jnp.exp(w_log)
    a = jax.nn.sigmoid(ibase_ref[...] + _dot(lora.astype(BF16), iup_ref[...]))
    g = _dot(jax.nn.sigmoid(lora).astype(BF16), gup_ref[...])

    kk = k * kks_ref[...]
    kk = kk / jnp.maximum(jnp.sqrt(_head_sums(kk * kk)), KK_EPS)
    k = k * (1.0 + (a - 1.0) * kas_ref[...])
    bonus_v = _head_sums(r * k * bonus_ref[...]) * v

    y = _scan_chunk(r, log_decay, k, v, -kk, kk * a, s_ref)

    inv_n = 1.0 / B_HEAD_DIM
    mu = _head_sums(y) * inv_n
    d = y - mu
    var = _head_sums(d * d) * inv_n
    yn = (d * lax.rsqrt(var + GN_EPS)) * gng_ref[...] + gnb_ref[...]
    o_ref[...] = ((yn + bonus_v) * g).astype(o_ref.dtype)


def _mixer_b(proj, a_cols, bw, lora_col, lora_w, shift_mu, decay_up, decay_base, iclr_up, iclr_base,
             gate_up, kk_scale, ka_scale, bonus, gn_g, gn_b):
    t = proj.shape[0]
    c = SCAN_CHUNK
    cb = a_cols // bw
    lb = lora_col // lora_w
    n_lora = DECAY_LORA + ICLR_LORA + GATE_LORA

    def pad_rows(w, start):
        return jnp.zeros((lora_w, bw), F32).at[start:start + w.shape[0]].set(w).astype(BF16)

    dup = pad_rows(decay_up, 0)
    iup = pad_rows(iclr_up, DECAY_LORA)
    gup = pad_rows(gate_up, DECAY_LORA + ICLR_LORA)
    mu = shift_mu.reshape(1, -1)
    mu_l = jnp.zeros((1, lora_w), F32).at[:, :n_lora].set(mu[:, 3 * bw:])

    def cur(width, blk):
        return pl.BlockSpec((c, width), lambda i: (i, blk))

    vec = pl.BlockSpec((1, bw), lambda i: (0, 0))
    vecl = pl.BlockSpec((1, lora_w), lambda i: (0, 0))
    up = pl.BlockSpec((lora_w, bw), lambda i: (0, 0))
    row = lambda x: x.reshape(1, -1)
    return pl.pallas_call(
        _mixer_b_body,
        out_shape=jax.ShapeDtypeStruct((t, bw), BF16),
        grid=(t // c,),
        in_specs=[cur(bw, cb), cur(bw, cb + 1), cur(bw, cb + 2), cur(lora_w, lb),
                  vec, vec, vec, vecl, up, up, up, vec, vec, vec, vec, vec, vec, vec],
        out_specs=pl.BlockSpec((c, bw), lambda i: (i, 0)),
        scratch_shapes=[pltpu.VMEM((bw // LANES, LANES, LANES), F32),
                        pltpu.VMEM((1, bw), F32), pltpu.VMEM((1, bw), F32), pltpu.VMEM((1, bw), F32),
                        pltpu.VMEM((1, lora_w), F32)],
        compiler_params=_params(("arbitrary",)),
        name="mixer_b",
    )(proj, proj, proj, proj, mu[:, :bw], mu[:, bw:2 * bw], mu[:, 2 * bw:3 * bw], mu_l, dup, iup, gup,
      row(decay_base), row(iclr_base), row(kk_scale), row(ka_scale), row(bonus), row(gn_g), row(gn_b))


def _out_proj_body(ya_ref, yb_ref, w_ref, x_ref, g1_ref, o_ref, wb_ref):
    @pl.when(pl.program_id(1) == 0)
    def _():
        wb_ref[...] = w_ref[...].astype(wb_ref.dtype)

    half = ya_ref.shape[1]
    mix = _dot(ya_ref[...], wb_ref[:half, :]) + _dot(yb_ref[...], wb_ref[half:, :])
    o_ref[...] = x_ref[...] + g1_ref[...] * mix


def _out_projection(ya, yb, w_out, x, g1, tm=MATMUL_ROWS, tn=MATMUL_COLS):
    t, d = x.shape
    tm = min(tm, t)
    half = ya.shape[1]
    act = pl.BlockSpec((tm, half), lambda j, i: (i, 0))
    tile = pl.BlockSpec((tm, tn), lambda j, i: (i, j))
    return pl.pallas_call(
        _out_proj_body,
        out_shape=jax.ShapeDtypeStruct((t, d), F32),
        grid=(d // tn, t // tm),
        in_specs=[act, act, pl.BlockSpec((2 * half, tn), lambda j, i: (0, j)), tile,
                  pl.BlockSpec((1, tn), lambda j, i: (0, j))],
        out_specs=tile,
        scratch_shapes=[pltpu.VMEM((2 * half, tn), BF16)],
        compiler_params=_params(("arbitrary", "arbitrary")),
        name="out_projection",
    )(ya, yb, w_out, x, g1)


def _router_body(x_ref, g_ref, sh_ref, sc_ref, rw_ref, rb_ref, sgu_ref, sd_ref, g2_ref,
                 te_ref, gate_ref, pos_ref, cnt_ref, hp_ref, base_ref, carry_ref):
    @pl.when(pl.program_id(0) == 0)
    def _():
        carry_ref[...] = jnp.zeros(carry_ref.shape, F32)

    x = x_ref[...]
    h = _norm_modulate(x, g_ref[...], sh_ref[...], sc_ref[...])
    hp_ref[...] = _pack_halves(h)
    ds = sd_ref.shape[0]
    gu = _even_matmul(h.astype(BF16), sgu_ref)
    act = (jax.nn.silu(gu[:, :ds]) * gu[:, ds:]).astype(BF16)
    base_ref[...] = x + g2_ref[...] * _dot(act, sd_ref[...])
    tm = x.shape[0]
    n_exp = rw_ref.shape[1]
    per_group = n_exp // N_GROUPS
    neg_inf = -jnp.inf

    h_hi, h_lo = _split_bf16(h, 2)
    w_hi, w_lo = _split_bf16(rw_ref[...], 2)
    scores = jax.nn.sigmoid(_dot(h_hi, w_hi) + (_dot(h_hi, w_lo) + _dot(h_lo, w_hi)))
    sel = scores + rb_ref[...]
    lane = lax.broadcasted_iota(jnp.int32, (tm, n_exp), 1)
    gid = lane // per_group

    gscore = jnp.zeros((tm, n_exp), F32)
    for gi in range(N_GROUPS):
        in_g = gid == gi
        m1 = jnp.max(jnp.where(in_g, sel, neg_inf), axis=-1, keepdims=True)
        i1 = jnp.min(jnp.where(in_g & (sel == m1), lane, n_exp), axis=-1, keepdims=True)
        m2 = jnp.max(jnp.where(in_g & (lane != i1), sel, neg_inf), axis=-1, keepdims=True)
        gscore = jnp.where(in_g, m1 + m2, gscore)
    beaten = jnp.zeros((tm, n_exp), jnp.int32)
    for s in range(1, N_GROUPS):
        other = pltpu.roll(gscore, s * per_group, 1)
        wins = (other > gscore) | ((other == gscore) & (gid >= s))
        beaten = beaten + wins.astype(jnp.int32)
    allowed = beaten < TOPK_GROUPS

    masked = jnp.where(allowed, sel, neg_inf)
    chosen = jnp.zeros((tm, n_exp), F32)
    idxs, gates = [], []
    for _ in range(TOP_K):
        m = jnp.max(masked, axis=-1, keepdims=True)
        idx = jnp.min(jnp.where(masked == m, lane, n_exp), axis=-1, keepdims=True)
        hit = lane == idx
        gates.append(jnp.sum(jnp.where(hit, scores, 0.0), axis=-1, keepdims=True))
        idxs.append(idx)
        chosen = jnp.where(hit, 1.0, chosen)
        masked = jnp.where(hit, neg_inf, masked)
    gsum = gates[0]
    for gk in gates[1:]:
        gsum = gsum + gk

    ri = lax.broadcasted_iota(jnp.int32, (tm, tm), 0)
    ci = lax.broadcasted_iota(jnp.int32, (tm, tm), 1)
    before = (ci < ri).astype(BF16)
    pos = carry_ref[...] + _dot(before, chosen.astype(BF16))
    carry_ref[...] = carry_ref[...] + jnp.sum(chosen, axis=0, keepdims=True)
    cnt_ref[...] = jnp.broadcast_to(carry_ref[...], cnt_ref.shape).astype(jnp.int32)

    te = jnp.zeros((tm, n_exp), jnp.int32)
    gt = jnp.zeros((tm, n_exp), F32)
    ps = jnp.zeros((tm, n_exp), jnp.int32)
    for kk in range(TOP_K):
        slot = lane == kk
        hit = lane == idxs[kk]
        pk = jnp.sum(jnp.where(hit, pos, 0.0), axis=-1, keepdims=True)
        te = jnp.where(slot, idxs[kk], te)
        gt = jnp.where(slot, gates[kk] / gsum * ROUTED_SCALE, gt)
        ps = jnp.where(slot, pk.astype(jnp.int32), ps)
    te_ref[...] = te
    gate_ref[...] = gt
    pos_ref[...] = ps


def _router(x1, norm_g, shift, scale, router_w, router_bias, sh_w_gate, sh_w_up, sh_w_down, g2, tm=ROW_TILE):
    t, d = x1.shape
    tm = min(tm, t)
    n_exp = router_w.shape[1]
    ds = sh_w_gate.shape[1]
    sgu = jnp.concatenate([sh_w_gate, sh_w_up], axis=1).astype(BF16)
    row = pl.BlockSpec((tm, d), lambda i: (i, 0))
    vec = pl.BlockSpec((1, d), lambda i: (0, 0))
    small = pl.BlockSpec((tm, n_exp), lambda i: (i, 0))
    return pl.pallas_call(
        _router_body,
        out_shape=[jax.ShapeDtypeStruct((t, n_exp), jnp.int32), jax.ShapeDtypeStruct((t, n_exp), F32),
                   jax.ShapeDtypeStruct((t, n_exp), jnp.int32), jax.ShapeDtypeStruct((8, n_exp), jnp.int32),
                   jax.ShapeDtypeStruct((t, d // 2), jnp.uint32), jax.ShapeDtypeStruct((t, d), F32)],
        grid=(t // tm,),
        in_specs=[row, vec, vec, vec,
                  pl.BlockSpec((d, n_exp), lambda i: (0, 0)), pl.BlockSpec((1, n_exp), lambda i: (0, 0)),
                  pl.BlockSpec((d, 2 * ds), lambda i: (0, 0)), pl.BlockSpec((ds, d), lambda i: (0, 0)), vec],
        out_specs=[small, small, small, pl.BlockSpec((8, n_exp), lambda i: (0, 0)),
                   pl.BlockSpec((tm, d // 2), lambda i: (i, 0)), row],
        scratch_shapes=[pltpu.VMEM((1, n_exp), F32)],
        compiler_params=_params(("arbitrary",)),
        name="moe_router",
    )(x1, norm_g, shift, scale, router_w, router_bias.reshape(1, -1), sgu, sh_w_down.astype(BF16), g2)


def _dest_body(starts_ref, te_ref, pos_ref, o_ref):
    te = te_ref[...]
    acc = jnp.zeros(te.shape, jnp.int32)
    for e in range(starts_ref.shape[0]):
        acc = jnp.where(te == e, starts_ref[e], acc)
    o_ref[...] = acc + pos_ref[...]


def _dest_rows(starts, top_e, pos_sel):
    t = top_e.shape[0]
    shape = (t * TOP_K // LANES, LANES)
    te = top_e[:, :TOP_K].reshape(shape)
    ps = pos_sel[:, :TOP_K].reshape(shape)
    blk = pl.BlockSpec(shape, lambda i: (0, 0))
    dest = pl.pallas_call(
        _dest_body,
        out_shape=jax.ShapeDtypeStruct(shape, jnp.int32),
        grid=(1,),
        in_specs=[pl.BlockSpec(memory_space=pltpu.SMEM), blk, blk],
        out_specs=blk,
        compiler_params=_params(("arbitrary",)),
        name="moe_dest",
    )(starts, te, ps)
    return dest.reshape(t, TOP_K)


def _pack_halves(x):
    half = x.shape[1] // 2
    lo = pltpu.bitcast(x[:, :half].astype(BF16).astype(F32), jnp.uint32)
    hi = pltpu.bitcast(x[:, half:].astype(BF16).astype(F32), jnp.uint32)
    return (lo >> 16) | (hi & jnp.uint32(0xFFFF0000))


def _unpack_halves(p):
    lo = pltpu.bitcast(p << 16, F32)
    hi = pltpu.bitcast(p & jnp.uint32(0xFFFF0000), F32)
    return lo, hi


def _dispatch_body(end_ref, nb_ref, dst_ref, hp_ref, xs_hbm, hp_buf, zero_buf, sem, zsem):
    i = pl.program_id(0)
    n = pl.num_programs(0)
    tm = hp_ref.shape[0]
    rows = zero_buf.shape[0]
    n_exp = end_ref.shape[0]
    n_blocks = xs_hbm.shape[0] // rows

    def drain(slot):
        for _ in range(TOP_K):
            pltpu.make_async_copy(hp_buf.at[slot], xs_hbm.at[pl.ds(0, tm), :], sem.at[slot]).wait()

    @pl.when(i == 0)
    def _():
        zero_buf[...] = jnp.zeros(zero_buf.shape, zero_buf.dtype)
        nb = nb_ref[0]

        def zero_rows(start, count):
            first = start if count == 1 else pl.multiple_of(start, 8)
            return pltpu.make_async_copy(zero_buf.at[pl.ds(0, count), :], xs_hbm.at[pl.ds(first, count), :], zsem)

        def issue(start, stop, count):
            def body(j, carry):
                zero_rows(start + j * count, count).start()
                return carry
            n_copies = (stop - start) // count
            lax.fori_loop(0, n_copies, body, 0)
            return n_copies

        def fill(e, carry):
            n1, n8 = carry
            end = end_ref[e]
            aligned = (end + 7) // 8 * 8
            stop = (end + rows - 1) // rows * rows
            return n1 + issue(end, aligned, 1), n8 + issue(aligned, stop, 8)
        n1, n8 = lax.fori_loop(0, n_exp, fill, (0, 0))
        n_tail = issue(nb * rows, n_blocks * rows, rows)

        for count, n_copies in ((1, n1), (8, n8), (rows, n_tail)):
            def wait(j, carry, count=count):
                zero_rows(0, count).wait()
                return carry
            lax.fori_loop(0, n_copies, wait, 0)

    slot = i % 2

    @pl.when(i >= 2)
    def _():
        drain(slot)

    hp_buf[slot] = hp_ref[...]

    def body(g, carry):
        t0 = pl.multiple_of(g * SUBLANES, SUBLANES)
        for j in range(SUBLANES):
            for kk in range(TOP_K):
                pltpu.make_async_copy(hp_buf.at[slot, pl.ds(t0 + j, 1), :],
                                      xs_hbm.at[pl.ds(dst_ref[0, 0, (t0 + j) * TOP_K + kk], 1), :],
                                      sem.at[slot]).start()
        return carry
    lax.fori_loop(0, tm // SUBLANES, body, 0)

    @pl.when(i == n - 1)
    def _():
        drain(slot)

        @pl.when(n > 1)
        def _():
            drain(1 - slot)


def _dispatch(h_packed, seg_end, n_used, dest, n_rows, tm=ROW_TILE):
    t, half = h_packed.shape
    tm = min(tm, t)
    nt = t // tm
    dest3 = dest.reshape(nt, 1, tm * TOP_K)
    grid_spec = pltpu.PrefetchScalarGridSpec(
        num_scalar_prefetch=2,
        grid=(nt,),
        in_specs=[pl.BlockSpec((1, 1, tm * TOP_K), lambda i, e, nb: (i, 0, 0), memory_space=pltpu.SMEM),
                  pl.BlockSpec((tm, half), lambda i, e, nb: (i, 0))],
        out_specs=pl.BlockSpec(memory_space=pl.ANY),
        scratch_shapes=[pltpu.VMEM((2, tm, half), jnp.uint32), pltpu.VMEM((MOE_ROWS, half), jnp.uint32),
                        pltpu.SemaphoreType.DMA((2,)), pltpu.SemaphoreType.DMA(())],
    )
    return pl.pallas_call(
        _dispatch_body,
        out_shape=jax.ShapeDtypeStruct((n_rows, half), jnp.uint32),
        grid_spec=grid_spec,
        compiler_params=_params(("arbitrary",)),
        name="moe_dispatch",
    )(seg_end, n_used, dest3, h_packed)


def _expert_body(be_ref, nb_ref, first_ref, next_ref, slot_ref, x_ref, wg_hbm, wu_hbm, wd_hbm, o_ref,
                 wgu_buf, wd_buf, sem):
    b = pl.program_id(0)
    nb = nb_ref[0]
    de = wd_buf.shape[1]

    def weight_copies(e, slot):
        return [pltpu.make_async_copy(wg_hbm.at[e], wgu_buf.at[slot, :, pl.ds(0, de)], sem.at[slot]),
                pltpu.make_async_copy(wu_hbm.at[e], wgu_buf.at[slot, :, pl.ds(de, de)], sem.at[slot]),
                pltpu.make_async_copy(wd_hbm.at[e], wd_buf.at[slot], sem.at[slot])]

    @pl.when((b == 0) & (nb > 0))
    def _():
        for cp in weight_copies(be_ref[0], 0):
            cp.start(priority=1)

    @pl.when(b < nb)
    def _():
        slot = slot_ref[b]

        @pl.when(first_ref[b] == 1)
        def _():
            for cp in weight_copies(0, slot):
                cp.wait()

            @pl.when(next_ref[b] >= 0)
            def _():
                for cp in weight_copies(next_ref[b], 1 - slot):
                    cp.start(priority=1)

        lo, hi = _unpack_halves(x_ref[...])
        x = jnp.concatenate([lo.astype(BF16), hi.astype(BF16)], axis=1)
        gu = _even_matmul(x, wgu_buf.at[slot])
        act = (jax.nn.silu(gu[:, :de]) * gu[:, de:]).astype(BF16)
        o_ref[...] = _pack_halves(_dot(act, wd_buf[slot].astype(BF16)))

    @pl.when(b >= nb)
    def _():
        o_ref[...] = jnp.zeros(o_ref.shape, o_ref.dtype)


def _routed_experts(x_sorted, blk_expert, n_used, blk_first, blk_next, blk_slot, w_gate, w_up, w_down):
    n_exp, d, de = w_gate.shape
    rows = MOE_ROWS
    nb = x_sorted.shape[0] // rows

    def last_used(b, be, nbr, *_):
        return (jnp.maximum(jnp.minimum(b, nbr[0] - 1), 0), 0)

    any_space = pl.BlockSpec(memory_space=pl.ANY)
    grid_spec = pltpu.PrefetchScalarGridSpec(
        num_scalar_prefetch=5,
        grid=(nb,),
        in_specs=[pl.BlockSpec((rows, d // 2), last_used), any_space, any_space, any_space],
        out_specs=pl.BlockSpec((rows, d // 2), lambda b, *_: (b, 0)),
        scratch_shapes=[pltpu.VMEM((2, d, 2 * de), F32), pltpu.VMEM((2, de, d), F32),
                        pltpu.SemaphoreType.DMA((2,))],
    )
    return pl.pallas_call(
        _expert_body,
        out_shape=jax.ShapeDtypeStruct((nb * rows, d // 2), jnp.uint32),
        grid_spec=grid_spec,
        compiler_params=_params(("arbitrary",)),
        name="routed_experts",
    )(blk_expert, n_used, blk_first, blk_next, blk_slot, x_sorted, w_gate, w_up, w_down)


def _combine_body(dst_ref, dstn_ref, y_hbm, gate_ref, base_ref, g2_ref, fg_ref, o_ref, ybuf, sem):
    i = pl.program_id(0)
    n = pl.num_programs(0)
    tm = base_ref.shape[0]

    def issue(dref, slot):
        def body(g, carry):
            t0 = pl.multiple_of(g * SUBLANES, SUBLANES)
            for j in range(SUBLANES):
                for kk in range(TOP_K):
                    pltpu.make_async_copy(y_hbm.at[pl.ds(dref[0, 0, (t0 + j) * TOP_K + kk], 1), :],
                                          ybuf.at[slot, kk, pl.ds(t0 + j, 1), :], sem.at[slot]).start(priority=kk % 2)
            return carry
        lax.fori_loop(0, tm // SUBLANES, body, 0)

    @pl.when(i == 0)
    def _():
        issue(dst_ref, 0)

    @pl.when(i + 1 < n)
    def _():
        issue(dstn_ref, (i + 1) % 2)

    slot = i % 2
    for kk in range(TOP_K):
        pltpu.make_async_copy(y_hbm.at[pl.ds(0, tm), :], ybuf.at[slot, kk], sem.at[slot]).wait()
    half = ybuf.shape[-1]
    r_lo = r_hi = None
    for kk in range(TOP_K):
        lo, hi = _unpack_halves(ybuf[slot, kk])
        gk = gate_ref[:, kk:kk + 1]
        r_lo = gk * lo if r_lo is None else r_lo + gk * lo
        r_hi = gk * hi if r_hi is None else r_hi + gk * hi
    x_lo = base_ref[:, :half] + g2_ref[:, :half] * r_lo
    x_hi = base_ref[:, half:] + g2_ref[:, half:] * r_hi
    ssq = jnp.sum(x_lo * x_lo, axis=-1, keepdims=True) + jnp.sum(x_hi * x_hi, axis=-1, keepdims=True)
    inv = lax.rsqrt(ssq / (2 * half) + NORM_EPS)
    o_ref[:, :half] = x_lo * inv * fg_ref[:, :half]
    o_ref[:, half:] = x_hi * inv * fg_ref[:, half:]


def _combine(y_sorted, dest, gate, base, g2, final_g):
    t, d = base.shape
    tm = min(COMBINE_TOKENS, t)
    nt = t // tm
    dest3 = dest.reshape(nt, 1, tm * TOP_K)
    n_exp = gate.shape[1]
    vec = pl.BlockSpec((1, d), lambda i: (0, 0))
    return pl.pallas_call(
        _combine_body,
        out_shape=jax.ShapeDtypeStruct((t, d), F32),
        grid=(nt,),
        in_specs=[pl.BlockSpec((1, 1, tm * TOP_K), lambda i: (i, 0, 0), memory_space=pltpu.SMEM),
                  pl.BlockSpec((1, 1, tm * TOP_K), lambda i: (jnp.minimum(i + 1, nt - 1), 0, 0),
                               memory_space=pltpu.SMEM),
                  pl.BlockSpec(memory_space=pl.ANY),
                  pl.BlockSpec((tm, n_exp), lambda i: (i, 0)),
                  pl.BlockSpec((tm, d), lambda i: (i, 0)), vec, vec],
        out_specs=pl.BlockSpec((tm, d), lambda i: (i, 0)),
        scratch_shapes=[pltpu.VMEM((2, TOP_K, tm, d // 2), jnp.uint32), pltpu.SemaphoreType.DMA((2,))],
        compiler_params=_params(("arbitrary",)),
        name="moe_combine",
    )(dest3, dest3, y_sorted, gate, base, g2, final_g.reshape(1, -1))


def _segment_tables(counts, n_tok):
    n_exp = counts.shape[0]
    rows = MOE_ROWS
    padded = (counts + rows - 1) // rows * rows
    ends = jnp.cumsum(padded)
    starts = ends - padded
    nb = n_tok * TOP_K // rows + n_exp + 1
    first_row = jnp.arange(nb, dtype=jnp.int32) * rows
    blk_expert = jnp.sum((ends[None, :] <= first_row[:, None]).astype(jnp.int32), axis=1)
    blk_expert = jnp.minimum(blk_expert, n_exp - 1)
    n_used = ends[-1] // rows
    blk = jnp.arange(nb, dtype=jnp.int32)
    prev_expert = jnp.concatenate([jnp.full((1,), -1, jnp.int32), blk_expert[:-1]])
    blk_first = ((blk_expert != prev_expert) & (blk < n_used)).astype(jnp.int32)
    blk_slot = (jnp.cumsum(blk_first) - 1) % 2
    after = ends[blk_expert] // rows
    blk_next = jnp.where(after < n_used, blk_expert[jnp.minimum(after, nb - 1)], -1)
    tables = [t.astype(jnp.int32) for t in (blk_expert, n_used.reshape(1), blk_first, blk_next, blk_slot)]
    return starts.astype(jnp.int32), (starts + counts).astype(jnp.int32), tables, nb * rows


def _layer(x, c, mod_w, mod_b, norm1_g, norm2_g, w_in, w_out, a_ln_g, a_ln_b, a_spatial_w, a_spatial_b,
           b_shift_mu, b_decay_up, b_decay_base, b_iclr_up, b_iclr_base, b_gate_up, b_kk_scale,
           b_ka_scale, b_bonus, b_gn_g, b_gn_b, router_w, router_bias, exp_w_gate, exp_w_up,
           exp_w_down, sh_w_gate, sh_w_up, sh_w_down):
    t, d = x.shape
    a_width = a_ln_g.shape[0]
    bw = b_decay_base.shape[0]
    lora_w = MATMUL_COLS
    lora_col = 2 * a_width + 3 * bw
    assert lora_col % lora_w == 0 and w_in.shape[1] - lora_col <= lora_w

    mod = _mod_vector(c, mod_w, mod_b)
    sh1, sc1, g1, sh2, sc2, g2 = [mod[:, i * d:(i + 1) * d] for i in range(6)]

    h1 = _normmod(x, norm1_g.reshape(1, d), sh1, sc1)
    proj = _in_projection(h1, w_in.T, tn=lora_w)
    ya = _mixer_a(proj, a_ln_g, a_ln_b, a_spatial_w, a_spatial_b, a_width)
    yb = _mixer_b(proj, 2 * a_width, bw, lora_col, lora_w, b_shift_mu, b_decay_up, b_decay_base, b_iclr_up,
                  b_iclr_base, b_gate_up, b_kk_scale, b_ka_scale, b_bonus.reshape(-1), b_gn_g, b_gn_b)
    x1 = _out_projection(ya, yb, w_out, x, g1)

    n2 = norm2_g.reshape(1, d)
    top_e, gate, pos_sel, counts, h_packed, base = _router(
        x1, n2, sh2, sc2, router_w, router_bias, sh_w_gate, sh_w_up, sh_w_down, g2)
    starts, seg_end, blk_tables, n_rows = _segment_tables(counts[0], t)
    dest = _dest_rows(starts, top_e, pos_sel)
    x_sorted = _dispatch(h_packed, seg_end, blk_tables[1], dest, n_rows)
    y_sorted = _routed_experts(x_sorted, *blk_tables, exp_w_gate, exp_w_up, exp_w_down)
    return base, g2, y_sorted, dest, gate


def kernel(x, c, mod_w, mod_b, norm1_g, norm2_g, w_in, w_out, a_ln_g, a_ln_b, a_spatial_w, a_spatial_b, b_shift_mu, b_decay_up, b_decay_base, b_iclr_up, b_iclr_base, b_gate_up, b_kk_scale, b_ka_scale, b_bonus, b_gn_g, b_gn_b, router_w, router_bias, exp_w_gate, exp_w_up, exp_w_down, sh_w_gate, sh_w_up, sh_w_down, final_g):
    batch, seq, d = x.shape
    assert batch == 1 and mod_w.shape[0] == 1, "single sequence, single layer"
    layer = [p[0] for p in (mod_w, mod_b, norm1_g, norm2_g, w_in, w_out, a_ln_g, a_ln_b, a_spatial_w,
                            a_spatial_b, b_shift_mu, b_decay_up, b_decay_base, b_iclr_up, b_iclr_base,
                            b_gate_up, b_kk_scale, b_ka_scale, b_bonus, b_gn_g, b_gn_b, router_w,
                            router_bias, exp_w_gate, exp_w_up, exp_w_down, sh_w_gate, sh_w_up, sh_w_down)]
    base, g2, y_sorted, dest, gate = _layer(x[0], c, *layer)
    out = _combine(y_sorted, dest, gate, base, g2, final_g)
    return out.reshape(batch, seq, d)
```

```python
import functools

import jax
import jax.numpy as jnp
from jax import lax
from jax.experimental import pallas as pl
from jax.experimental.pallas import tpu as pltpu

A_GROUP_DIM = 128
B_HEAD_DIM = 64
DECAY_LORA = 96
ICLR_LORA = 96
GATE_LORA = 256
TOP_K = 8
N_GROUPS = 8
TOPK_GROUPS = 4
ROUTED_SCALE = 2.5
NORM_EPS = 1e-6
LN_EPS = 1e-5
GN_EPS = 64e-5
KK_EPS = 1e-12

LANES = 128
SUBLANES = 8
MXU_WIDTH = 256
VMEM_LIMIT_BYTES = 58 * 1024 * 1024

ROW_TILE = 256
MATMUL_ROWS = 1024
MATMUL_COLS = 512
SCAN_CHUNK = 64
MOE_ROWS = 256
COMBINE_TOKENS = 128

F32 = jnp.float32
BF16 = jnp.bfloat16


def _params(sem):
    return pltpu.CompilerParams(dimension_semantics=sem, vmem_limit_bytes=VMEM_LIMIT_BYTES)


def _dot(a, b):
    return jnp.dot(a, b, preferred_element_type=F32)


def _even_matmul(x, w_ref):
    hm = x.shape[0] // 2
    tiles = w_ref.shape[1] // MXU_WIDTH
    rows = [jnp.concatenate([_dot(x[r * hm:(r + 1) * hm],
                                  w_ref[:, c * MXU_WIDTH:(c + 1) * MXU_WIDTH].astype(BF16))
                             for c in range(tiles)], axis=1) for r in range(2)]
    return jnp.concatenate(rows, axis=0)


def _mod_body(c_ref, w_ref, b_ref, o_ref, s_ref):
    @pl.when(pl.program_id(0) == 0)
    def _():
        c = c_ref[...]
        s_ref[...] = jnp.broadcast_to(c * jax.nn.sigmoid(c), s_ref.shape)

    s = s_ref[...]
    cols = [jnp.sum(w_ref[:, j * LANES:(j + 1) * LANES] * s, axis=0, keepdims=True)
            for j in range(w_ref.shape[1] // LANES)]
    o_ref[...] = jnp.concatenate(cols, axis=1) + b_ref[...]


def _mod_vector(c, mod_w, mod_b, tn=MATMUL_COLS):
    d, n = mod_w.shape
    return pl.pallas_call(
        _mod_body,
        out_shape=jax.ShapeDtypeStruct((1, n), F32),
        grid=(n // tn,),
        in_specs=[pl.BlockSpec((d, 1), lambda j: (0, 0)),
                  pl.BlockSpec((d, tn), lambda j: (0, j)),
                  pl.BlockSpec((1, tn), lambda j: (0, j))],
        out_specs=pl.BlockSpec((1, tn), lambda j: (0, j)),
        scratch_shapes=[pltpu.VMEM((d, LANES), F32)],
        compiler_params=_params(("arbitrary",)),
        name="mod_vector",
    )(c.reshape(d, 1), mod_w, mod_b.reshape(1, n))


def _norm_modulate(x, g, shift, scale):
    ms = jnp.mean(x * x, axis=-1, keepdims=True)
    y = x * lax.rsqrt(ms + NORM_EPS) * g
    return y * (1.0 + scale) + shift


def _normmod_body(x_ref, g_ref, sh_ref, sc_ref, o_ref):
    o_ref[...] = _norm_modulate(x_ref[...], g_ref[...], sh_ref[...], sc_ref[...]).astype(o_ref.dtype)


def _normmod(x, g, shift, scale, tm=ROW_TILE):
    t, d = x.shape
    tm = min(tm, t)
    row = pl.BlockSpec((tm, d), lambda i: (i, 0))
    vec = pl.BlockSpec((1, d), lambda i: (0, 0))
    return pl.pallas_call(
        _normmod_body,
        out_shape=jax.ShapeDtypeStruct((t, d), BF16),
        grid=(t // tm,),
        in_specs=[row, vec, vec, vec],
        out_specs=row,
        compiler_params=_params(("arbitrary",)),
        name="norm_modulate",
    )(x, g, shift, scale)


def _proj_body(h_ref, wt_ref, o_ref, wb_ref, *, n_valid):
    @pl.when(pl.program_id(1) == 0)
    def _():
        wt = wt_ref[...]
        col = pl.program_id(0) * wt.shape[0] + lax.broadcasted_iota(jnp.int32, wt.shape, 0)
        wb_ref[...] = jnp.where(col < n_valid, wt, 0.0).T.astype(wb_ref.dtype)

    o_ref[...] = _dot(h_ref[...], wb_ref[...])


def _in_projection(h, w_in_t, tm=MATMUL_ROWS, tn=MATMUL_COLS):
    t, d = h.shape
    n = w_in_t.shape[0]
    tm = min(tm, t)
    n_pad = pl.cdiv(n, tn) * tn
    return pl.pallas_call(
        functools.partial(_proj_body, n_valid=n),
        out_shape=jax.ShapeDtypeStruct((t, n_pad), F32),
        grid=(n_pad // tn, t // tm),
        in_specs=[pl.BlockSpec((tm, d), lambda j, i: (i, 0)),
                  pl.BlockSpec((tn, d), lambda j, i: (j, 0))],
        out_specs=pl.BlockSpec((tm, tn), lambda j, i: (i, j)),
        scratch_shapes=[pltpu.VMEM((d, tn), BF16)],
        compiler_params=_params(("arbitrary", "arbitrary")),
        name="in_projection",
    )(h, w_in_t)


def _gelu(x):
    return 0.5 * x * (1.0 + lax.erf(x * (2.0 ** -0.5)))


def _mixer_a_body(u_ref, v_ref, lng_ref, lnb_ref, ws_ref, sb_ref, o_ref):
    tm = u_ref.shape[0]
    gd = A_GROUP_DIM
    row = lax.broadcasted_iota(jnp.int32, (gd, gd), 0)
    col = lax.broadcasted_iota(jnp.int32, (gd, gd), 1)
    causal = col <= row
    for g in range(u_ref.shape[1] // gd):
        sl = slice(g * gd, (g + 1) * gd)
        u = _gelu(u_ref[:, sl])
        v = _gelu(v_ref[:, sl])
        mu = jnp.mean(v, axis=-1, keepdims=True)
        dv = v - mu
        var = jnp.mean(dv * dv, axis=-1, keepdims=True)
        vn = ((dv * lax.rsqrt(var + LN_EPS)) * lng_ref[:, sl] + lnb_ref[:, sl]).astype(BF16)
        w = jnp.where(causal, ws_ref[g], 0.0).astype(BF16)
        for n in range(tm // gd):
            rs = slice(n * gd, (n + 1) * gd)
            mixed = _dot(w, vn[rs]) + sb_ref[g]
            o_ref[rs, sl] = (u[rs] * mixed).astype(o_ref.dtype)


def _mixer_a(proj, ln_g, ln_b, spatial_w, spatial_b, a_width, tm=ROW_TILE):
    t = proj.shape[0]
    tm = min(tm, t)
    groups = a_width // A_GROUP_DIM
    sb = jnp.broadcast_to(spatial_b[:, :, None], (groups, A_GROUP_DIM, A_GROUP_DIM))
    vec = pl.BlockSpec((1, a_width), lambda i: (0, 0))
    mat = pl.BlockSpec((groups, A_GROUP_DIM, A_GROUP_DIM), lambda i: (0, 0, 0))
    return pl.pallas_call(
        _mixer_a_body,
        out_shape=jax.ShapeDtypeStruct((t, a_width), BF16),
        grid=(t // tm,),
        in_specs=[pl.BlockSpec((tm, a_width), lambda i: (i, 0)),
                  pl.BlockSpec((tm, a_width), lambda i: (i, 1)),
                  vec, vec, mat, mat],
        out_specs=pl.BlockSpec((tm, a_width), lambda i: (i, 0)),
        compiler_params=_params(("arbitrary",)),
        name="mixer_a",
    )(proj, proj, ln_g.reshape(1, -1), ln_b.reshape(1, -1), spatial_w, sb)


def _split_bf16(x, parts):
    out = []
    for _ in range(parts - 1):
        hi = x.astype(BF16)
        out.append(hi)
        x = x - hi.astype(F32)
    out.append(x.astype(BF16))
    return out


def _head_sums(x):
    head0 = lax.broadcasted_iota(jnp.int32, (x.shape[0], LANES), 1) < B_HEAD_DIM
    parts = []
    for s in range(x.shape[1] // LANES):
        xs = x[:, s * LANES:(s + 1) * LANES]
        s0 = jnp.sum(jnp.where(head0, xs, 0.0), axis=-1, keepdims=True)
        s1 = jnp.sum(jnp.where(head0, 0.0, xs), axis=-1, keepdims=True)
        parts.append(jnp.where(head0, s0, s1))
    return jnp.concatenate(parts, axis=1)


def _softplus(x):
    return jnp.maximum(x, 0.0) + jnp.log1p(jnp.exp(-jnp.abs(x)))


def _scan_chunk(r_in, w_in, k_in, v_in, a_in, b_in, s_ref):
    c = SCAN_CHUNK
    ti = lax.broadcasted_iota(jnp.int32, (c, c), 0)
    tj = lax.broadcasted_iota(jnp.int32, (c, c), 1)
    tri_incl = (tj <= ti).astype(BF16)
    head0 = lax.broadcasted_iota(jnp.int32, (c, LANES), 1) < B_HEAD_DIM
    i2 = lax.broadcasted_iota(jnp.int32, (2 * c, 2 * c), 0)
    j2 = lax.broadcasted_iota(jnp.int32, (2 * c, 2 * c), 1)
    eye = (i2 == j2).astype(F32)
    strict_bd = (i2 // c == j2 // c) & (j2 < i2)
    ic = lax.broadcasted_iota(jnp.int32, (c, 2 * c), 0)
    jc = lax.broadcasted_iota(jnp.int32, (c, 2 * c), 1)
    incl_cat = (jc % c) <= ic
    contract0 = (((0,), (0,)), ((), ()))
    contract1 = (((1,), (1,)), ((), ()))

    def stack(x):
        return jnp.concatenate([jnp.where(head0, x, 0.0), jnp.where(head0, 0.0, x)], axis=0)

    def cat(xs, axis):
        return jnp.concatenate(xs, axis=axis)

    pairs = range(r_in.shape[1] // LANES)
    sls = [slice(p * LANES, (p + 1) * LANES) for p in pairs]
    w = [w_in[:, sl] for sl in sls]
    cum3 = [_dot(tri_incl, cat(_split_bf16(w[p], 3), 1)) for p in pairs]
    cum = [x[:, :LANES] + x[:, LANES:2 * LANES] + x[:, 2 * LANES:] for x in cum3]
    cum_last = [x[c - 1:c, :] for x in cum]
    e_neg = [jnp.exp(-cum[p]) for p in pairs]
    e_tail = [jnp.exp(cum_last[p] - cum[p]) for p in pairs]
    r_t = [r_in[:, sls[p]] * jnp.exp(cum[p]) for p in pairs]
    a_st = [stack(a_in[:, sls[p]] * jnp.exp(cum[p] - w[p])) for p in pairs]
    g_c = [jnp.exp(x) for x in cum_last]
    b = [b_in[:, sl] for sl in sls]
    k = [k_in[:, sl] for sl in sls]
    v_bd = [stack(v_in[:, sl]).astype(BF16) for sl in sls]
    lhs = [cat([a_st[p], r_t[p]], 0).astype(BF16) for p in pairs]
    rhs = [cat([stack(b[p] * e_neg[p]), stack(k[p] * e_neg[p])], 0).astype(BF16) for p in pairs]
    big = [lax.dot_general(lhs[p], rhs[p], contract1, preferred_element_type=F32) for p in pairs]
    a_ak = [jnp.where(strict_bd, x[:2 * c, 2 * c:], 0.0).astype(BF16) for x in big]
    a_rb = [jnp.where(incl_cat, x[2 * c:, :2 * c], 0.0).astype(BF16) for x in big]
    a_rk = [jnp.where(incl_cat, x[2 * c:, 2 * c:], 0.0).astype(BF16) for x in big]
    akv = [_dot(a_ak[p], v_bd[p]) for p in pairs]

    x = [jnp.where(strict_bd, y[:2 * c, :2 * c], 0.0) for y in big]
    t_inv = [eye + y for y in x]
    xb = [y.astype(BF16) for y in x]
    x = [_dot(y, y) for y in xb]
    span = 2
    while 2 * span < c:
        prod = [_dot(x[p].astype(BF16), cat([x[p], t_inv[p]], 1).astype(BF16)) for p in pairs]
        x = [y[:, :2 * c] for y in prod]
        t_inv = [t_inv[p] + prod[p][:, 2 * c:] for p in pairs]
        span *= 2
    t_inv = [t_inv[p] + _dot(x[p].astype(BF16), t_inv[p].astype(BF16)) for p in pairs]

    tub = [_dot(t_inv[p].astype(BF16), cat([a_st[p], akv[p]], 1).astype(BF16)).astype(BF16)
           for p in pairs]
    ry = [_dot(a_rb[p], tub[p]) for p in pairs]
    y0 = [ry[p][:, LANES:] + _dot(a_rk[p], v_bd[p]) for p in pairs]
    r_hat = [r_t[p] + ry[p][:, :LANES] for p in pairs]
    bg = [stack(b[p] * e_tail[p]).astype(BF16) for p in pairs]
    kg = [stack(k[p] * e_tail[p]).astype(BF16) for p in pairs]
    mn = [lax.dot_general(bg[p], tub[p], contract0, preferred_element_type=F32) for p in pairs]
    n = [mn[p][:, LANES:] + lax.dot_general(kg[p], v_bd[p], contract0, preferred_element_type=F32)
         for p in pairs]

    s0 = [s_ref[p] for p in pairs]
    fin = [_dot(cat([r_hat[p], mn[p][:, :LANES]], 0).astype(BF16), cat(_split_bf16(s0[p], 2), 1))
           for p in pairs]
    g_col = [jnp.sum(eye * g_c[p], axis=1, keepdims=True) for p in pairs]
    for p in pairs:
        s_ref[p] = g_col[p] * s0[p] + (fin[p][c:, :LANES] + fin[p][c:, LANES:]) + n[p]
    return cat([fin[p][:c, :LANES] + fin[p][:c, LANES:] + y0[p] for p in pairs], 1)


def _mixer_b_body(r_ref, k_ref, v_ref, l_ref, mur_ref, muk_ref, muv_ref, mul_ref, dup_ref, iup_ref, gup_ref,
                  dbase_ref, ibase_ref, kks_ref, kas_ref, bonus_ref, gng_ref, gnb_ref,
                  o_ref, s_ref, pr_ref, pk_ref, pv_ref, pl_ref):
    @pl.when(pl.program_id(0) == 0)
    def _():
        for ref in (s_ref, pr_ref, pk_ref, pv_ref, pl_ref):
            ref[...] = jnp.zeros(ref.shape, F32)

    def shifted(p_ref, prev_ref, mu_ref):
        p = p_ref[...]
        rowid = lax.broadcasted_iota(jnp.int32, p.shape, 0)
        prev = jnp.where(rowid == 0, prev_ref[...], pltpu.roll(p, 1, 0))
        prev_ref[...] = p[p.shape[0] - 1:, :]
        return p + mu_ref[...] * (prev - p)

    r = shifted(r_ref, pr_ref, mur_ref)
    k = shifted(k_ref, pk_ref, muk_ref)
    v = shifted(v_ref, pv_ref, muv_ref)
    lora = shifted(l_ref, pl_ref, mul_ref)

    dec_in = dbase_ref[...] + _dot(jnp.tanh(lora).astype(BF16), dup_ref[...])
    w_log = -_softplus(-dec_in) - 0.5
    log_decay = -jnp.exp(w_log)
    a = jax.nn.sigmoid(ibase_ref[...] + _dot(lora.astype(BF16), iup_ref[...]))
    g = _dot(jax.nn.sigmoid(lora).astype(BF16), gup_ref[...])

    kk = k * kks_ref[...]
    kk = kk / jnp.maximum(jnp.sqrt(_head_sums(kk * kk)), KK_EPS)
    k = k * (1.0 + (a - 1.0) * kas_ref[...])
    bonus_v = _head_sums(r * k * bonus_ref[...]) * v

    ins = (r, log_decay, k, v, -kk, kk * a)
    gw = r.shape[1] // 2
    gp = gw // LANES
    y = jnp.concatenate([_scan_chunk(*(x[:, h * gw:(h + 1) * gw] for x in ins), s_ref.at[pl.ds(h * gp, gp)])
                         for h in range(2)], axis=1)

    inv_n = 1.0 / B_HEAD_DIM
    mu = _head_sums(y) * inv_n
    d = y - mu
    var = _head_sums(d * d) * inv_n
    yn = (d * lax.rsqrt(var + GN_EPS)) * gng_ref[...] + gnb_ref[...]
    o_ref[...] = ((yn + bonus_v) * g).astype(o_ref.dtype)


def _mixer_b(proj, a_cols, bw, lora_col, lora_w, shift_mu, decay_up, decay_base, iclr_up, iclr_base,
             gate_up, kk_scale, ka_scale, bonus, gn_g, gn_b):
    t = proj.shape[0]
    c = SCAN_CHUNK
    cb = a_cols // bw
    lb = lora_col // lora_w
    n_lora = DECAY_LORA + ICLR_LORA + GATE_LORA

    def pad_rows(w, start):
        return jnp.zeros((lora_w, bw), F32).at[start:start + w.shape[0]].set(w).astype(BF16)

    dup = pad_rows(decay_up, 0)
    iup = pad_rows(iclr_up, DECAY_LORA)
    gup = pad_rows(gate_up, DECAY_LORA + ICLR_LORA)
    mu = shift_mu.reshape(1, -1)
    mu_l = jnp.zeros((1, lora_w), F32).at[:, :n_lora].set(mu[:, 3 * bw:])

    def cur(width, blk):
        return pl.BlockSpec((c, width), lambda i: (i, blk))

    vec = pl.BlockSpec((1, bw), lambda i: (0, 0))
    vecl = pl.BlockSpec((1, lora_w), lambda i: (0, 0))
    up = pl.BlockSpec((lora_w, bw), lambda i: (0, 0))
    row = lambda x: x.reshape(1, -1)
    return pl.pallas_call(
        _mixer_b_body,
        out_shape=jax.ShapeDtypeStruct((t, bw), BF16),
        grid=(t // c,),
        in_specs=[cur(bw, cb), cur(bw, cb + 1), cur(bw, cb + 2), cur(lora_w, lb),
                  vec, vec, vec, vecl, up, up, up, vec, vec, vec, vec, vec, vec, vec],
        out_specs=pl.BlockSpec((c, bw), lambda i: (i, 0)),
        scratch_shapes=[pltpu.VMEM((bw // LANES, LANES, LANES), F32),
                        pltpu.VMEM((1, bw), F32), pltpu.VMEM((1, bw), F32), pltpu.VMEM((1, bw), F32),
                        pltpu.VMEM((1, lora_w), F32)],
        compiler_params=_params(("arbitrary",)),
        name="mixer_b",
    )(proj, proj, proj, proj, mu[:, :bw], mu[:, bw:2 * bw], mu[:, 2 * bw:3 * bw], mu_l, dup, iup, gup,
      row(decay_base), row(iclr_base), row(kk_scale), row(ka_scale), row(bonus), row(gn_g), row(gn_b))


def _out_proj_body(ya_ref, yb_ref, w_ref, x_ref, g1_ref, o_ref, wb_ref):
    @pl.when(pl.program_id(1) == 0)
    def _():
        wb_ref[...] = w_ref[...].astype(wb_ref.dtype)

    half = ya_ref.shape[1]
    mix = _dot(ya_ref[...], wb_ref[:half, :]) + _dot(yb_ref[...], wb_ref[half:, :])
    o_ref[...] = x_ref[...] + g1_ref[...] * mix


def _out_projection(ya, yb, w_out, x, g1, tm=MATMUL_ROWS, tn=MATMUL_COLS):
    t, d = x.shape
    tm = min(tm, t)
    half = ya.shape[1]
    act = pl.BlockSpec((tm, half), lambda j, i: (i, 0))
    tile = pl.BlockSpec((tm, tn), lambda j, i: (i, j))
    return pl.pallas_call(
        _out_proj_body,
        out_shape=jax.ShapeDtypeStruct((t, d), F32),
        grid=(d // tn, t // tm),
        in_specs=[act, act, pl.BlockSpec((2 * half, tn), lambda j, i: (0, j)), tile,
                  pl.BlockSpec((1, tn), lambda j, i: (0, j))],
        out_specs=tile,
        scratch_shapes=[pltpu.VMEM((2 * half, tn), BF16)],
        compiler_params=_params(("arbitrary", "arbitrary")),
        name="out_projection",
    )(ya, yb, w_out, x, g1)


def _router_body(x_ref, g_ref, sh_ref, sc_ref, rw_ref, rb_ref, sgu_ref, sd_ref, g2_ref,
                 te_ref, gate_ref, pos_ref, cnt_ref, hp_ref, base_ref, carry_ref):
    @pl.when(pl.program_id(0) == 0)
    def _():
        carry_ref[...] = jnp.zeros(carry_ref.shape, F32)

    x = x_ref[...]
    h = _norm_modulate(x, g_ref[...], sh_ref[...], sc_ref[...])
    hp_ref[...] = _pack_halves(h)
    ds = sd_ref.shape[0]
    gu = _even_matmul(h.astype(BF16), sgu_ref)
    act = (jax.nn.silu(gu[:, :ds]) * gu[:, ds:]).astype(BF16)
    base_ref[...] = x + g2_ref[...] * _dot(act, sd_ref[...])
    tm = x.shape[0]
    n_exp = rw_ref.shape[1]
    per_group = n_exp // N_GROUPS
    neg_inf = -jnp.inf

    h_hi, h_lo = _split_bf16(h, 2)
    w_hi, w_lo = _split_bf16(rw_ref[...], 2)
    scores = jax.nn.sigmoid(_dot(h_hi, w_hi) + (_dot(h_hi, w_lo) + _dot(h_lo, w_hi)))
    sel = scores + rb_ref[...]
    lane = lax.broadcasted_iota(jnp.int32, (tm, n_exp), 1)
    gid = lane // per_group

    gscore = jnp.zeros((tm, n_exp), F32)
    for gi in range(N_GROUPS):
        in_g = gid == gi
        m1 = jnp.max(jnp.where(in_g, sel, neg_inf), axis=-1, keepdims=True)
        i1 = jnp.min(jnp.where(in_g & (sel == m1), lane, n_exp), axis=-1, keepdims=True)
        m2 = jnp.max(jnp.where(in_g & (lane != i1), sel, neg_inf), axis=-1, keepdims=True)
        gscore = jnp.where(in_g, m1 + m2, gscore)
    beaten = jnp.zeros((tm, n_exp), jnp.int32)
    for s in range(1, N_GROUPS):
        other = pltpu.roll(gscore, s * per_group, 1)
        wins = (other > gscore) | ((other == gscore) & (gid >= s))
        beaten = beaten + wins.astype(jnp.int32)
    allowed = beaten < TOPK_GROUPS

    masked = jnp.where(allowed, sel, neg_inf)
    chosen = jnp.zeros((tm, n_exp), F32)
    idxs, gates = [], []
    for _ in range(TOP_K):
        m = jnp.max(masked, axis=-1, keepdims=True)
        idx = jnp.min(jnp.where(masked == m, lane, n_exp), axis=-1, keepdims=True)
        hit = lane == idx
        gates.append(jnp.sum(jnp.where(hit, scores, 0.0), axis=-1, keepdims=True))
        idxs.append(idx)
        chosen = jnp.where(hit, 1.0, chosen)
        masked = jnp.where(hit, neg_inf, masked)
    gsum = gates[0]
    for gk in gates[1:]:
        gsum = gsum + gk

    ri = lax.broadcasted_iota(jnp.int32, (tm, tm), 0)
    ci = lax.broadcasted_iota(jnp.int32, (tm, tm), 1)
    before = (ci < ri).astype(BF16)
    pos = carry_ref[...] + _dot(before, chosen.astype(BF16))
    carry_ref[...] = carry_ref[...] + jnp.sum(chosen, axis=0, keepdims=True)
    cnt_ref[...] = jnp.broadcast_to(carry_ref[...], cnt_ref.shape).astype(jnp.int32)

    te = jnp.zeros((tm, n_exp), jnp.int32)
    gt = jnp.zeros((tm, n_exp), F32)
    ps = jnp.zeros((tm, n_exp), jnp.int32)
    for kk in range(TOP_K):
        slot = lane == kk
        hit = lane == idxs[kk]
        pk = jnp.sum(jnp.where(hit, pos, 0.0), axis=-1, keepdims=True)
        te = jnp.where(slot, idxs[kk], te)
        gt = jnp.where(slot, gates[kk] / gsum * ROUTED_SCALE, gt)
        ps = jnp.where(slot, pk.astype(jnp.int32), ps)
    te_ref[...] = te
    gate_ref[...] = gt
    pos_ref[...] = ps


def _router(x1, norm_g, shift, scale, router_w, router_bias, sh_w_gate, sh_w_up, sh_w_down, g2, tm=ROW_TILE):
    t, d = x1.shape
    tm = min(tm, t)
    n_exp = router_w.shape[1]
    ds = sh_w_gate.shape[1]
    sgu = jnp.concatenate([sh_w_gate, sh_w_up], axis=1).astype(BF16)
    row = pl.BlockSpec((tm, d), lambda i: (i, 0))
    vec = pl.BlockSpec((1, d), lambda i: (0, 0))
    small = pl.BlockSpec((tm, n_exp), lambda i: (i, 0))
    return pl.pallas_call(
        _router_body,
        out_shape=[jax.ShapeDtypeStruct((t, n_exp), jnp.int32), jax.ShapeDtypeStruct((t, n_exp), F32),
                   jax.ShapeDtypeStruct((t, n_exp), jnp.int32), jax.ShapeDtypeStruct((8, n_exp), jnp.int32),
                   jax.ShapeDtypeStruct((t, d // 2), jnp.uint32), jax.ShapeDtypeStruct((t, d), F32)],
        grid=(t // tm,),
        in_specs=[row, vec, vec, vec,
                  pl.BlockSpec((d, n_exp), lambda i: (0, 0)), pl.BlockSpec((1, n_exp), lambda i: (0, 0)),
                  pl.BlockSpec((d, 2 * ds), lambda i: (0, 0)), pl.BlockSpec((ds, d), lambda i: (0, 0)), vec],
        out_specs=[small, small, small, pl.BlockSpec((8, n_exp), lambda i: (0, 0)),
                   pl.BlockSpec((tm, d // 2), lambda i: (i, 0)), row],
        scratch_shapes=[pltpu.VMEM((1, n_exp), F32)],
        compiler_params=_params(("arbitrary",)),
        name="moe_router",
    )(x1, norm_g, shift, scale, router_w, router_bias.reshape(1, -1), sgu, sh_w_down.astype(BF16), g2)


def _dest_body(starts_ref, te_ref, pos_ref, o_ref):
    te = te_ref[...]
    acc = jnp.zeros(te.shape, jnp.int32)
    for e in range(starts_ref.shape[0]):
        acc = jnp.where(te == e, starts_ref[e], acc)
    o_ref[...] = acc + pos_ref[...]


def _dest_rows(starts, top_e, pos_sel):
    t = top_e.shape[0]
    shape = (t * TOP_K // LANES, LANES)
    te = top_e[:, :TOP_K].reshape(shape)
    ps = pos_sel[:, :TOP_K].reshape(shape)
    blk = pl.BlockSpec(shape, lambda i: (0, 0))
    dest = pl.pallas_call(
        _dest_body,
        out_shape=jax.ShapeDtypeStruct(shape, jnp.int32),
        grid=(1,),
        in_specs=[pl.BlockSpec(memory_space=pltpu.SMEM), blk, blk],
        out_specs=blk,
        compiler_params=_params(("arbitrary",)),
        name="moe_dest",
    )(starts, te, ps)
    return dest.reshape(t, TOP_K)


def _pack_halves(x):
    half = x.shape[1] // 2
    lo = pltpu.bitcast(x[:, :half].astype(BF16).astype(F32), jnp.uint32)
    hi = pltpu.bitcast(x[:, half:].astype(BF16).astype(F32), jnp.uint32)
    return (lo >> 16) | (hi & jnp.uint32(0xFFFF0000))


def _unpack_halves(p):
    lo = pltpu.bitcast(p << 16, F32)
    hi = pltpu.bitcast(p & jnp.uint32(0xFFFF0000), F32)
    return lo, hi


def _dispatch_body(end_ref, nb_ref, dst_ref, hp_ref, xs_hbm, hp_buf, zero_buf, sem, zsem):
    i = pl.program_id(0)
    n = pl.num_programs(0)
    tm = hp_ref.shape[0]
    rows = zero_buf.shape[0]
    n_exp = end_ref.shape[0]
    n_blocks = xs_hbm.shape[0] // rows

    def drain(slot):
        for _ in range(TOP_K):
            pltpu.make_async_copy(hp_buf.at[slot], xs_hbm.at[pl.ds(0, tm), :], sem.at[slot]).wait()

    @pl.when(i == 0)
    def _():
        zero_buf[...] = jnp.zeros(zero_buf.shape, zero_buf.dtype)
        nb = nb_ref[0]

        def zero_rows(start, count):
            first = start if count == 1 else pl.multiple_of(start, 8)
            return pltpu.make_async_copy(zero_buf.at[pl.ds(0, count), :], xs_hbm.at[pl.ds(first, count), :], zsem)

        def issue(start, stop, count):
            def body(j, carry):
                zero_rows(start + j * count, count).start()
                return carry
            n_copies = (stop - start) // count
            lax.fori_loop(0, n_copies, body, 0)
            return n_copies

        def fill(e, carry):
            n1, n8 = carry
            end = end_ref[e]
            aligned = (end + 7) // 8 * 8
            stop = (end + rows - 1) // rows * rows
            return n1 + issue(end, aligned, 1), n8 + issue(aligned, stop, 8)
        n1, n8 = lax.fori_loop(0, n_exp, fill, (0, 0))
        n_tail = issue(nb * rows, n_blocks * rows, rows)

        for count, n_copies in ((1, n1), (8, n8), (rows, n_tail)):
            def wait(j, carry, count=count):
                zero_rows(0, count).wait()
                return carry
            lax.fori_loop(0, n_copies, wait, 0)

    slot = i % 2

    @pl.when(i >= 2)
    def _():
        drain(slot)

    hp_buf[slot] = hp_ref[...]

    def body(g, carry):
        t0 = pl.multiple_of(g * SUBLANES, SUBLANES)
        for j in range(SUBLANES):
            for kk in range(TOP_K):
                pltpu.make_async_copy(hp_buf.at[slot, pl.ds(t0 + j, 1), :],
                                      xs_hbm.at[pl.ds(dst_ref[0, 0, (t0 + j) * TOP_K + kk], 1), :],
                                      sem.at[slot]).start()
        return carry
    lax.fori_loop(0, tm // SUBLANES, body, 0)

    @pl.when(i == n - 1)
    def _():
        drain(slot)

        @pl.when(n > 1)
        def _():
            drain(1 - slot)


def _dispatch(h_packed, seg_end, n_used, dest, n_rows, tm=ROW_TILE):
    t, half = h_packed.shape
    tm = min(tm, t)
    nt = t // tm
    dest3 = dest.reshape(nt, 1, tm * TOP_K)
    grid_spec = pltpu.PrefetchScalarGridSpec(
        num_scalar_prefetch=2,
        grid=(nt,),
        in_specs=[pl.BlockSpec((1, 1, tm * TOP_K), lambda i, e, nb: (i, 0, 0), memory_space=pltpu.SMEM),
                  pl.BlockSpec((tm, half), lambda i, e, nb: (i, 0))],
        out_specs=pl.BlockSpec(memory_space=pl.ANY),
        scratch_shapes=[pltpu.VMEM((2, tm, half), jnp.uint32), pltpu.VMEM((MOE_ROWS, half), jnp.uint32),
                        pltpu.SemaphoreType.DMA((2,)), pltpu.SemaphoreType.DMA(())],
    )
    return pl.pallas_call(
        _dispatch_body,
        out_shape=jax.ShapeDtypeStruct((n_rows, half), jnp.uint32),
        grid_spec=grid_spec,
        compiler_params=_params(("arbitrary",)),
        name="moe_dispatch",
    )(seg_end, n_used, dest3, h_packed)


def _expert_body(be_ref, nb_ref, first_ref, next_ref, slot_ref, x_ref, wg_hbm, wu_hbm, wd_hbm, o_ref,
                 wgu_buf, wd_buf, sem):
    b = pl.program_id(0)
    nb = nb_ref[0]
    de = wd_buf.shape[1]

    def weight_copies(e, slot):
        return [pltpu.make_async_copy(wg_hbm.at[e], wgu_buf.at[slot, :, pl.ds(0, de)], sem.at[slot]),
                pltpu.make_async_copy(wu_hbm.at[e], wgu_buf.at[slot, :, pl.ds(de, de)], sem.at[slot]),
                pltpu.make_async_copy(wd_hbm.at[e], wd_buf.at[slot], sem.at[slot])]

    @pl.when((b == 0) & (nb > 0))
    def _():
        for cp in weight_copies(be_ref[0], 0):
            cp.start(priority=1)

    @pl.when(b < nb)
    def _():
        slot = slot_ref[b]

        @pl.when(first_ref[b] == 1)
        def _():
            for cp in weight_copies(0, slot):
                cp.wait()

            @pl.when(next_ref[b] >= 0)
            def _():
                for cp in weight_copies(next_ref[b], 1 - slot):
                    cp.start(priority=1)

        lo, hi = _unpack_halves(x_ref[...])
        x = jnp.concatenate([lo.astype(BF16), hi.astype(BF16)], axis=1)
        gu = _even_matmul(x, wgu_buf.at[slot])
        act = (jax.nn.silu(gu[:, :de]) * gu[:, de:]).astype(BF16)
        o_ref[...] = _pack_halves(_dot(act, wd_buf[slot].astype(BF16)))

    @pl.when(b >= nb)
    def _():
        o_ref[...] = jnp.zeros(o_ref.shape, o_ref.dtype)


def _routed_experts(x_sorted, blk_expert, n_used, blk_first, blk_next, blk_slot, w_gate, w_up, w_down):
    n_exp, d, de = w_gate.shape
    rows = MOE_ROWS
    nb = x_sorted.shape[0] // rows

    def last_used(b, be, nbr, *_):
        return (jnp.maximum(jnp.minimum(b, nbr[0] - 1), 0), 0)

    any_space = pl.BlockSpec(memory_space=pl.ANY)
    grid_spec = pltpu.PrefetchScalarGridSpec(
        num_scalar_prefetch=5,
        grid=(nb,),
        in_specs=[pl.BlockSpec((rows, d // 2), last_used), any_space, any_space, any_space],
        out_specs=pl.BlockSpec((rows, d // 2), lambda b, *_: (b, 0)),
        scratch_shapes=[pltpu.VMEM((2, d, 2 * de), F32), pltpu.VMEM((2, de, d), F32),
                        pltpu.SemaphoreType.DMA((2,))],
    )
    return pl.pallas_call(
        _expert_body,
        out_shape=jax.ShapeDtypeStruct((nb * rows, d // 2), jnp.uint32),
        grid_spec=grid_spec,
        compiler_params=_params(("arbitrary",)),
        name="routed_experts",
    )(blk_expert, n_used, blk_first, blk_next, blk_slot, x_sorted, w_gate, w_up, w_down)


def _combine_body(dst_ref, dstn_ref, y_hbm, gate_ref, base_ref, g2_ref, fg_ref, o_ref, ybuf, sem):
    i = pl.program_id(0)
    n = pl.num_programs(0)
    tm = base_ref.shape[0]

    def issue(dref, slot):
        def body(g, carry):
            t0 = pl.multiple_of(g * SUBLANES, SUBLANES)
            for j in range(SUBLANES):
                for kk in range(TOP_K):
                    pltpu.make_async_copy(y_hbm.at[pl.ds(dref[0, 0, (t0 + j) * TOP_K + kk], 1), :],
                                          ybuf.at[slot, kk, pl.ds(t0 + j, 1), :], sem.at[slot]).start(priority=kk % 2)
            return carry
        lax.fori_loop(0, tm // SUBLANES, body, 0)

    @pl.when(i == 0)
    def _():
        issue(dst_ref, 0)

    @pl.when(i + 1 < n)
    def _():
        issue(dstn_ref, (i + 1) % 2)

    slot = i % 2
    for kk in range(TOP_K):
        pltpu.make_async_copy(y_hbm.at[pl.ds(0, tm), :], ybuf.at[slot, kk], sem.at[slot]).wait()
    half = ybuf.shape[-1]
    r_lo = r_hi = None
    for kk in range(TOP_K):
        lo, hi = _unpack_halves(ybuf[slot, kk])
        gk = gate_ref[:, kk:kk + 1]
        r_lo = gk * lo if r_lo is None else r_lo + gk * lo
        r_hi = gk * hi if r_hi is None else r_hi + gk * hi
    x_lo = base_ref[:, :half] + g2_ref[:, :half] * r_lo
    x_hi = base_ref[:, half:] + g2_ref[:, half:] * r_hi
    ssq = jnp.sum(x_lo * x_lo, axis=-1, keepdims=True) + jnp.sum(x_hi * x_hi, axis=-1, keepdims=True)
    inv = lax.rsqrt(ssq / (2 * half) + NORM_EPS)
    o_ref[:, :half] = x_lo * inv * fg_ref[:, :half]
    o_ref[:, half:] = x_hi * inv * fg_ref[:, half:]


def _combine(y_sorted, dest, gate, base, g2, final_g):
    t, d = base.shape
    tm = min(COMBINE_TOKENS, t)
    nt = t // tm
    dest3 = dest.reshape(nt, 1, tm * TOP_K)
    n_exp = gate.shape[1]
    vec = pl.BlockSpec((1, d), lambda i: (0, 0))
    return pl.pallas_call(
        _combine_body,
        out_shape=jax.ShapeDtypeStruct((t, d), F32),
        grid=(nt,),
        in_specs=[pl.BlockSpec((1, 1, tm * TOP_K), lambda i: (i, 0, 0), memory_space=pltpu.SMEM),
                  pl.BlockSpec((1, 1, tm * TOP_K), lambda i: (jnp.minimum(i + 1, nt - 1), 0, 0),
                               memory_space=pltpu.SMEM),
                  pl.BlockSpec(memory_space=pl.ANY),
                  pl.BlockSpec((tm, n_exp), lambda i: (i, 0)),
                  pl.BlockSpec((tm, d), lambda i: (i, 0)), vec, vec],
        out_specs=pl.BlockSpec((tm, d), lambda i: (i, 0)),
        scratch_shapes=[pltpu.VMEM((2, TOP_K, tm, d // 2), jnp.uint32), pltpu.SemaphoreType.DMA((2,))],
        compiler_params=_params(("arbitrary",)),
        name="moe_combine",
    )(dest3, dest3, y_sorted, gate, base, g2, final_g.reshape(1, -1))


def _segment_tables(counts, n_tok):
    n_exp = counts.shape[0]
    rows = MOE_ROWS
    padded = (counts + rows - 1) // rows * rows
    ends = jnp.cumsum(padded)
    starts = ends - padded
    nb = n_tok * TOP_K // rows + n_exp + 1
    first_row = jnp.arange(nb, dtype=jnp.int32) * rows
    blk_expert = jnp.sum((ends[None, :] <= first_row[:, None]).astype(jnp.int32), axis=1)
    blk_expert = jnp.minimum(blk_expert, n_exp - 1)
    n_used = ends[-1] // rows
    blk = jnp.arange(nb, dtype=jnp.int32)
    prev_expert = jnp.concatenate([jnp.full((1,), -1, jnp.int32), blk_expert[:-1]])
    blk_first = ((blk_expert != prev_expert) & (blk < n_used)).astype(jnp.int32)
    blk_slot = (jnp.cumsum(blk_first) - 1) % 2
    after = ends[blk_expert] // rows
    blk_next = jnp.where(after < n_used, blk_expert[jnp.minimum(after, nb - 1)], -1)
    tables = [t.astype(jnp.int32) for t in (blk_expert, n_used.reshape(1), blk_first, blk_next, blk_slot)]
    return starts.astype(jnp.int32), (starts + counts).astype(jnp.int32), tables, nb * rows


def _layer(x, c, mod_w, mod_b, norm1_g, norm2_g, w_in, w_out, a_ln_g, a_ln_b, a_spatial_w, a_spatial_b,
           b_shift_mu, b_decay_up, b_decay_base, b_iclr_up, b_iclr_base, b_gate_up, b_kk_scale,
           b_ka_scale, b_bonus, b_gn_g, b_gn_b, router_w, router_bias, exp_w_gate, exp_w_up,
           exp_w_down, sh_w_gate, sh_w_up, sh_w_down):
    t, d = x.shape
    a_width = a_ln_g.shape[0]
    bw = b_decay_base.shape[0]
    lora_w = MATMUL_COLS
    lora_col = 2 * a_width + 3 * bw
    assert lora_col % lora_w == 0 and w_in.shape[1] - lora_col <= lora_w

    mod = _mod_vector(c, mod_w, mod_b)
    sh1, sc1, g1, sh2, sc2, g2 = [mod[:, i * d:(i + 1) * d] for i in range(6)]

    h1 = _normmod(x, norm1_g.reshape(1, d), sh1, sc1)
    proj = _in_projection(h1, w_in.T, tn=lora_w)
    ya = _mixer_a(proj, a_ln_g, a_ln_b, a_spatial_w, a_spatial_b, a_width)
    yb = _mixer_b(proj, 2 * a_width, bw, lora_col, lora_w, b_shift_mu, b_decay_up, b_decay_base, b_iclr_up,
                  b_iclr_base, b_gate_up, b_kk_scale, b_ka_scale, b_bonus.reshape(-1), b_gn_g, b_gn_b)
    x1 = _out_projection(ya, yb, w_out, x, g1)

    n2 = norm2_g.reshape(1, d)
    top_e, gate, pos_sel, counts, h_packed, base = _router(
        x1, n2, sh2, sc2, router_w, router_bias, sh_w_gate, sh_w_up, sh_w_down, g2)
    starts, seg_end, blk_tables, n_rows = _segment_tables(counts[0], t)
    dest = _dest_rows(starts, top_e, pos_sel)
    x_sorted = _dispatch(h_packed, seg_end, blk_tables[1], dest, n_rows)
    y_sorted = _routed_experts(x_sorted, *blk_tables, exp_w_gate, exp_w_up, exp_w_down)
    return base, g2, y_sorted, dest, gate


def kernel(x, c, mod_w, mod_b, norm1_g, norm2_g, w_in, w_out, a_ln_g, a_ln_b, a_spatial_w, a_spatial_b, b_shift_mu, b_decay_up, b_decay_base, b_iclr_up, b_iclr_base, b_gate_up, b_kk_scale, b_ka_scale, b_bonus, b_gn_g, b_gn_b, router_w, router_bias, exp_w_gate, exp_w_up, exp_w_down, sh_w_gate, sh_w_up, sh_w_down, final_g):
    batch, seq, d = x.shape
    assert batch == 1 and mod_w.shape[0] == 1, "single sequence, single layer"
    layer = [p[0] for p in (mod_w, mod_b, norm1_g, norm2_g, w_in, w_out, a_ln_g, a_ln_b, a_spatial_w,
                            a_spatial_b, b_shift_mu, b_decay_up, b_decay_base, b_iclr_up, b_iclr_base,
                            b_gate_up, b_kk_scale, b_ka_scale, b_bonus, b_gn_g, b_gn_b, router_w,
                            router_bias, exp_w_gate, exp_w_up, exp_w_down, sh_w_gate, sh_w_up, sh_w_down)]
    base, g2, y_sorted, dest, gate = _layer(x[0], c, *layer)
    out = _combine(y_sorted, dest, gate, base, g2, final_g)
    return out.reshape(batch, seq, d)
```
